```python
import jax, jax.numpy as jnp
from jax import lax
import numpy as np

D_MODEL = 2048
BATCH = 2
SEQ = 8192
DEPTH = 2

N_A_LAYERS = DEPTH // 2
N_B_LAYERS = DEPTH - N_A_LAYERS
N_DENSE = (DEPTH + 1) // 2
N_MOE = DEPTH // 2

HG_HEADS = 16
HG_KDIM = 128
HG_VDIM = D_MODEL // HG_HEADS
HG_FDIM = HG_HEADS * HG_KDIM
HG_CHUNK = 64

MLA_HEADS = 16
Q_LORA = 512
KV_LORA = 512
QK_NOPE = 128
QK_ROPE = 64
V_HEAD = 128
MLA_SCALE = (QK_NOPE + QK_ROPE) ** -0.5
ROPE_THETA = 10000.0
Q_BLOCK = 128

D_FF = 5632
N_EXPERTS = 8
TOP_K = 2
MOE_ROWS = 256

EPS = 1e-6

kernel_name = 'hybrid_hgrn2_mla_moe_yoco'


def rms_norm(x, g):
    xf = x.astype(jnp.float32)
    y = xf * lax.rsqrt(jnp.mean(xf * xf, axis=-1, keepdims=True) + EPS)
    return (y * g.astype(jnp.float32)).astype(x.dtype)


def modulation(sc, w, b, n):
    m = sc @ w + b
    return [t[:, None, :] for t in jnp.split(m, n, axis=-1)]


def rope_tables(positions):
    inv = 1.0 / (ROPE_THETA ** (jnp.arange(0, QK_ROPE, 2, dtype=jnp.float32) / QK_ROPE))
    ang = positions.astype(jnp.float32)[..., None] * inv
    return jnp.cos(ang), jnp.sin(ang)


def apply_rope(x, cos, sin):
    x1, x2 = jnp.split(x, 2, axis=-1)
    return jnp.concatenate([x1 * cos - x2 * sin, x2 * cos + x1 * sin], axis=-1).astype(x.dtype)


def hgrn2_mixer(u, w_in, lower_bound, out_norm_g, w_out):
    B, S, _ = u.shape
    n_chunks = S // HG_CHUNK
    proj = u @ w_in
    q, f_logit, i, g = jnp.split(proj, [HG_FDIM, 2 * HG_FDIM, 2 * HG_FDIM + D_MODEL], axis=-1)
    f = lower_bound + (1.0 - lower_bound) * jax.nn.sigmoid(f_logit.astype(jnp.float32))
    k = 1.0 - f
    log_f = jnp.log(f)

    def chunks(t, d):
        return t.astype(jnp.float32).reshape(B, n_chunks, HG_CHUNK, HG_HEADS, d).transpose(1, 0, 3, 2, 4)

    xs = (chunks(q, HG_KDIM), chunks(k, HG_KDIM), chunks(log_f, HG_KDIM), chunks(i, HG_VDIM))
    causal = jnp.tril(jnp.ones((HG_CHUNK, HG_CHUNK), dtype=bool))[:, :, None]

    def step(state, inp):
        qc, kc, lfc, vc = inp
        b = jnp.cumsum(lfc, axis=2)
        o_inter = jnp.einsum('bhtk,bhkv->bhtv', qc * jnp.exp(b), state)
        rel = b[:, :, :, None, :] - b[:, :, None, :, :]
        decay = jnp.exp(jnp.where(causal, rel, -jnp.inf))
        scores = jnp.einsum('bhtk,bhsk,bhtsk->bhts', qc, kc, decay)
        o_intra = jnp.einsum('bhts,bhsv->bhtv', scores, vc)
        b_last = b[:, :, -1:, :]
        state = (jnp.exp(b_last[:, :, 0, :])[..., None] * state
                 + jnp.einsum('bhsk,bhsv->bhkv', kc * jnp.exp(b_last - b), vc))
        return state, o_inter + o_intra

    s0 = jnp.zeros((B, HG_HEADS, HG_KDIM, HG_VDIM), jnp.float32)
    _, o = lax.scan(step, s0, xs)
    o = o.transpose(1, 0, 3, 2, 4).reshape(B, S, HG_HEADS * HG_VDIM).astype(u.dtype)
    o = rms_norm(o, out_norm_g) * jax.nn.silu(g)
    return o @ w_out


def mla_shared_kv(xs, w_kv_a, kv_norm_g, w_kv_b, cos, sin):
    B, S, _ = xs.shape
    c_kv, k_rope = jnp.split(xs @ w_kv_a, [KV_LORA], axis=-1)
    kv = (rms_norm(c_kv, kv_norm_g) @ w_kv_b).reshape(B, S, MLA_HEADS, QK_NOPE + V_HEAD)
    k_nope, v = jnp.split(kv, [QK_NOPE], axis=-1)
    k_rope = apply_rope(k_rope, cos, sin)
    return k_nope, k_rope, v


def mla_attention(u, k_nope, k_rope, v, w_q_a, q_norm_g, w_q_b, w_o, cos, sin):
    B, S, _ = u.shape
    q = (rms_norm(u @ w_q_a, q_norm_g) @ w_q_b).reshape(B, S, MLA_HEADS, QK_NOPE + QK_ROPE)
    q_nope, q_rope = jnp.split(q, [QK_NOPE], axis=-1)
    q_rope = apply_rope(q_rope, cos[:, :, None, :], sin[:, :, None, :])
    n_blk = S // Q_BLOCK

    def blocks(t):
        return t.reshape(B, n_blk, Q_BLOCK, *t.shape[2:]).swapaxes(0, 1)

    key_pos = jnp.arange(S)

    def attend(inp):
        qn, qr, blk = inp
        s = (jnp.einsum('bqhd,bkhd->bhqk', qn, k_nope)
             + jnp.einsum('bqhr,bkr->bhqk', qr, k_rope)).astype(jnp.float32) * MLA_SCALE
        q_pos = blk * Q_BLOCK + jnp.arange(Q_BLOCK)
        s = jnp.where(key_pos[None, :] <= q_pos[:, None], s, -jnp.inf)
        p = jax.nn.softmax(s, axis=-1).astype(v.dtype)
        return jnp.einsum('bhqk,bkhd->bqhd', p, v)

    o = lax.map(attend, (blocks(q_nope), blocks(q_rope), jnp.arange(n_blk)))
    o = o.swapaxes(0, 1).reshape(B, S, MLA_HEADS * V_HEAD)
    return o @ w_o


def swiglu(h, w_gu, w_down):
    gt, up = jnp.split(h @ w_gu, 2, axis=-1)
    return (jax.nn.silu(gt) * up) @ w_down


def moe_swiglu(h, w_router, w_gu, w_down):
    B, S, D = h.shape
    n_tok = B * S
    hf = h.reshape(n_tok, D)
    logits = (hf @ w_router).astype(jnp.float32)
    top_logit, top_idx = lax.top_k(logits, TOP_K)
    top_w = jax.nn.softmax(top_logit, axis=-1)
    n_asg = n_tok * TOP_K
    asg_e = top_idx.reshape(n_asg)
    asg_t = jnp.arange(n_asg, dtype=jnp.int32) // TOP_K
    asg_w = top_w.reshape(n_asg)
    order = jnp.argsort(asg_e)
    sorted_e = asg_e[order]
    counts = jnp.bincount(asg_e, length=N_EXPERTS)
    padded = (counts + MOE_ROWS - 1) // MOE_ROWS * MOE_ROWS
    start = jnp.cumsum(counts) - counts
    pend = jnp.cumsum(padded)
    pstart = pend - padded
    dest = pstart[sorted_e] + (jnp.arange(n_asg) - start[sorted_e])
    n_blocks = (n_asg + N_EXPERTS * (MOE_ROWS - 1)) // MOE_ROWS
    n_rows = n_blocks * MOE_ROWS
    row_tok = jnp.zeros((n_rows,), jnp.int32).at[dest].set(asg_t[order])
    row_w = jnp.zeros((n_rows,), jnp.float32).at[dest].set(asg_w[order])
    blk_e = jnp.minimum(jnp.searchsorted(pend, jnp.arange(n_blocks) * MOE_ROWS, side='right'),
                        N_EXPERTS - 1)

    def expert_block(inp):
        tok, e = inp
        return swiglu(hf[tok], w_gu[e], w_down[e])

    y = lax.map(expert_block, (row_tok.reshape(n_blocks, MOE_ROWS), blk_e))
    y = y.reshape(n_rows, D) * row_w[:, None].astype(h.dtype)
    out = jnp.zeros_like(hf).at[row_tok].add(y)
    return out.reshape(B, S, D)


def setup_inputs(seed: int = 0) -> dict:
    key = jax.random.key(seed)
    ks = jax.random.split(key, 26)
    f32 = jnp.float32
    D = D_MODEL

    def nrm(k, shape, fan_in, gain=1.0):
        return jax.random.normal(k, shape, f32) * (gain * fan_in ** -0.5)

    def gains(k, shape):
        return 1.0 + 0.02 * jax.random.normal(k, shape, f32)

    x = jax.random.normal(ks[0], (BATCH, SEQ, D), f32)
    c = jax.random.normal(ks[1], (BATCH, D), f32)
    positions = (jax.random.randint(ks[2], (BATCH, 1), 0, 1024)
                 + jnp.arange(SEQ)[None, :]).astype(jnp.int32)
    return {
        'x': x,
        'c': c,
        'positions': positions,
        'ada_w': nrm(ks[3], (DEPTH, 2, D, 3 * D), D, 0.5),
        'ada_b': 0.02 * jax.random.normal(ks[4], (DEPTH, 2, 3 * D), f32),
        'norm_g': gains(ks[5], (DEPTH, 2, 2, D)),
        'hg_w_in': nrm(ks[6], (N_A_LAYERS, D, 2 * HG_FDIM + 2 * D), D),
        'hg_lb_logits': 0.5 * jax.random.normal(ks[7], (N_A_LAYERS + 1, HG_FDIM), f32),
        'hg_out_norm_g': gains(ks[8], (N_A_LAYERS, D)),
        'hg_w_out': nrm(ks[9], (N_A_LAYERS, D, D), D),
        'kv_src_norm_g': gains(ks[10], (D,)),
        'kv_src_ada_w': nrm(ks[11], (D, 2 * D), D, 0.5),
        'kv_src_ada_b': 0.02 * jax.random.normal(ks[12], (2 * D,), f32),
        'mla_w_kv_a': nrm(ks[13], (D, KV_LORA + QK_ROPE), D),
        'mla_kv_norm_g': gains(ks[14], (KV_LORA,)),
        'mla_w_kv_b': nrm(ks[15], (KV_LORA, MLA_HEADS * (QK_NOPE + V_HEAD)), KV_LORA),
        'mla_w_q_a': nrm(ks[16], (N_B_LAYERS, D, Q_LORA), D),
        'mla_q_norm_g': gains(ks[17], (N_B_LAYERS, Q_LORA)),
        'mla_w_q_b': nrm(ks[18], (N_B_LAYERS, Q_LORA, MLA_HEADS * (QK_NOPE + QK_ROPE)), Q_LORA),
        'mla_w_o': nrm(ks[19], (N_B_LAYERS, MLA_HEADS * V_HEAD, D), MLA_HEADS * V_HEAD),
        'ffn_w_gu': nrm(ks[20], (N_DENSE, D, 2 * D_FF), D),
        'ffn_w_down': nrm(ks[21], (N_DENSE, D_FF, D), D_FF),
        'moe_w_router': nrm(ks[22], (N_MOE, D, N_EXPERTS), D),
        'moe_w_gu': nrm(ks[23], (N_MOE, N_EXPERTS, D, 2 * D_FF), D),
        'moe_w_down': nrm(ks[24], (N_MOE, N_EXPERTS, D_FF, D), D_FF),
    }


def reference(x, c, positions, ada_w, ada_b, norm_g, hg_w_in, hg_lb_logits, hg_out_norm_g,
              hg_w_out, kv_src_norm_g, kv_src_ada_w, kv_src_ada_b, mla_w_kv_a, mla_kv_norm_g,
              mla_w_kv_b, mla_w_q_a, mla_q_norm_g, mla_w_q_b, mla_w_o, ffn_w_gu, ffn_w_down,
              moe_w_router, moe_w_gu, moe_w_down):
    lb_all = jnp.cumsum(jax.nn.softmax(hg_lb_logits.astype(jnp.float32), axis=0), axis=0)
    cos, sin = rope_tables(positions)
    sc = jax.nn.silu(c)
    h = x
    k_nope = k_rope = v = None
    for layer in range(DEPTH):
        if layer == N_A_LAYERS:
            shift, scale = modulation(sc, kv_src_ada_w, kv_src_ada_b, 2)
            xs = rms_norm(h, kv_src_norm_g) * (1.0 + scale) + shift
            k_nope, k_rope, v = mla_shared_kv(xs, mla_w_kv_a, mla_kv_norm_g, mla_w_kv_b, cos, sin)

        shift, scale, gate = modulation(sc, ada_w[layer, 0], ada_b[layer, 0], 3)
        u = rms_norm(h, norm_g[layer, 0, 0]) * (1.0 + scale) + shift
        if layer < N_A_LAYERS:
            a = layer
            y = hgrn2_mixer(u, hg_w_in[a], lb_all[a], hg_out_norm_g[a], hg_w_out[a])
        else:
            bl = layer - N_A_LAYERS
            y = mla_attention(u, k_nope, k_rope, v, mla_w_q_a[bl], mla_q_norm_g[bl],
                              mla_w_q_b[bl], mla_w_o[bl], cos, sin)
        h = h + gate * rms_norm(y, norm_g[layer, 0, 1])

        shift, scale, gate = modulation(sc, ada_w[layer, 1], ada_b[layer, 1], 3)
        u = rms_norm(h, norm_g[layer, 1, 0]) * (1.0 + scale) + shift
        j = layer // 2
        if layer % 2 == 0:
            y = swiglu(u, ffn_w_gu[j], ffn_w_down[j])
        else:
            y = moe_swiglu(u, moe_w_router[j], moe_w_gu[j], moe_w_down[j])
        h = h + gate * rms_norm(y, norm_g[layer, 1, 1])
    return h
```

```python
import functools

import jax
import jax.numpy as jnp
from jax import lax
from jax.experimental import pallas as pl
from jax.experimental.pallas import tpu as pltpu

F32 = jnp.float32
BF16 = jnp.bfloat16

EPS = 1e-6
LANES = 128

HG_HEADS = 16
HG_KDIM = 128
HG_CHUNK = 64
HG_SUB = 16
HG_EXP_CLAMP = 80.0

MLA_HEADS = 16
QK_NOPE = 128
QK_ROPE = 64
V_HEAD = 128
QK_PAD = 256
MLA_SCALE = (QK_NOPE + QK_ROPE) ** -0.5
ROPE_THETA = 10000.0

N_EXPERTS = 8

VMEM_LIMIT = 56 * 1024 * 1024


def _cparams(sem):
    return pltpu.CompilerParams(dimension_semantics=sem, vmem_limit_bytes=VMEM_LIMIT)


def _rms(x):
    return x * lax.rsqrt(jnp.mean(x * x, axis=-1, keepdims=True) + EPS)


def _sigmoid(x):
    return 1.0 / (1.0 + jnp.exp(-x))


def _dot(a, b):
    return jnp.dot(a, b, preferred_element_type=F32)


def _dot_nt(a, b):
    return lax.dot_general(a, b, (((1,), (1,)), ((), ())), preferred_element_type=F32)


def _dot_tn(a, b):
    return lax.dot_general(a, b, (((0,), (0,)), ((), ())), preferred_element_type=F32)


def _mod_kernel(c_ref, w_ref, b_ref, o_ref):
    c = c_ref[...]
    sc = (c * _sigmoid(c)).astype(BF16)
    o_ref[0] = _dot(sc, w_ref[0].astype(BF16)) + b_ref[0]


def _modulation(c8, w, b, tn=1024):
    g, d, n = w.shape
    return pl.pallas_call(
        _mod_kernel,
        grid=(g, n // tn),
        in_specs=[pl.BlockSpec((8, d), lambda i, j: (0, 0)),
                  pl.BlockSpec((1, d, tn), lambda i, j: (i, 0, j)),
                  pl.BlockSpec((1, 1, tn), lambda i, j: (i, 0, j))],
        out_specs=pl.BlockSpec((1, 8, tn), lambda i, j: (i, 0, j)),
        out_shape=jax.ShapeDtypeStruct((g, 8, n), F32),
        compiler_params=_cparams(("arbitrary", "arbitrary")),
        name="modulation",
    )(c8, w, b.reshape(g, 1, n))


def _rope_kernel(pos_ref, inv_ref, cos_ref, sin_ref):
    ang = pos_ref[...] * inv_ref[...]
    lane = lax.broadcasted_iota(jnp.int32, ang.shape, 1)
    half = QK_ROPE // 2
    cos_ref[...] = jnp.where(lane < QK_ROPE, jnp.cos(ang), 0.0)
    sin_ref[...] = jnp.where(lane < half, -jnp.sin(ang),
                             jnp.where(lane < QK_ROPE, jnp.sin(ang), 0.0))


def _rope_tables(pos, tm=512):
    m = pos.shape[0]
    half = QK_ROPE // 2
    inv = 1.0 / (ROPE_THETA ** (jnp.arange(0, QK_ROPE, 2, dtype=F32) / QK_ROPE))
    inv128 = jnp.concatenate([inv, inv, jnp.zeros((LANES - 2 * half,), F32)]).reshape(1, LANES)
    return pl.pallas_call(
        _rope_kernel,
        grid=(m // tm,),
        in_specs=[pl.BlockSpec((tm, 1), lambda i: (i, 0)),
                  pl.BlockSpec((1, LANES), lambda i: (0, 0))],
        out_specs=[pl.BlockSpec((tm, LANES), lambda i: (i, 0))] * 2,
        out_shape=[jax.ShapeDtypeStruct((m, LANES), F32)] * 2,
        compiler_params=_cparams(("arbitrary",)),
        name="rope_tables",
    )(pos, inv128)


def _hg_proj_kernel(h_ref, g_ref, sc_ref, sh_ref, w_ref, q_ref, f_ref, i_ref, gg_ref, u_scr):
    j = pl.program_id(1)

    @pl.when(j == 0)
    def _():
        u = _rms(h_ref[...]) * g_ref[...] * (1.0 + sc_ref[0]) + sh_ref[0]
        u_scr[...] = u.astype(BF16)

    r = _dot(u_scr[...], w_ref[...])
    for idx, ref in enumerate((q_ref, f_ref, i_ref, gg_ref)):
        @pl.when(j == idx)
        def _(ref=ref):
            ref[...] = r.astype(ref.dtype)


def _hg_proj(h, g, scale, shift, w_in, rows_per_batch, tm=512):
    m, d = h.shape
    bpb = rows_per_batch // tm
    row = lambda i, j: (i, 0)
    mod = lambda i, j: (i // bpb, 0, 0)
    return pl.pallas_call(
        _hg_proj_kernel,
        grid=(m // tm, 4),
        in_specs=[pl.BlockSpec((tm, d), row),
                  pl.BlockSpec((1, d), lambda i, j: (0, 0)),
                  pl.BlockSpec((1, 1, d), mod),
                  pl.BlockSpec((1, 1, d), mod),
                  pl.BlockSpec((d, d), lambda i, j: (0, j))],
        out_specs=[pl.BlockSpec((tm, d), row)] * 4,
        out_shape=[jax.ShapeDtypeStruct((m, d), BF16), jax.ShapeDtypeStruct((m, d), F32),
                   jax.ShapeDtypeStruct((m, d), BF16), jax.ShapeDtypeStruct((m, d), BF16)],
        scratch_shapes=[pltpu.VMEM((tm, d), BF16)],
        compiler_params=_cparams(("arbitrary", "arbitrary")),
        name="hg_proj",
    )(h, g, scale, shift, w_in)


def _hgrn2_kernel(q_ref, f_ref, v_ref, lb_ref, o_ref, st_ref, *, n_chunks):
    c, sub = HG_CHUNK, HG_SUB
    n_sub = c // sub

    @pl.when(pl.program_id(2) == 0)
    def _():
        st_ref[...] = jnp.zeros_like(st_ref)

    lb = lb_ref[...]
    row = lax.broadcasted_iota(jnp.int32, (c, c), 0)
    col = lax.broadcasted_iota(jnp.int32, (c, c), 1)
    tri = (row >= col).astype(BF16)
    row2 = lax.broadcasted_iota(jnp.int32, (c, 2 * c), 0)
    col2 = lax.broadcasted_iota(jnp.int32, (c, 2 * c), 1)
    mask2 = ((col2 >> 6) == ((row2 >> 4) & 1)) & ((col2 & (c - 1)) <= row2)

    st = st_ref[...]
    for ci in range(n_chunks):
        sl = slice(ci * c, (ci + 1) * c)
        q = q_ref[sl, :].astype(F32)
        v = v_ref[sl, :]
        f = lb + (1.0 - lb) * _sigmoid(f_ref[sl, :])
        k = 1.0 - f
        lf = jnp.log(f)
        lf_hi = lf.astype(BF16)
        lf_lo = (lf - lf_hi.astype(F32)).astype(BF16)
        b = _dot(tri, lf_hi) + _dot(tri, lf_lo)
        b_last = b[c - 1:c, :]
        refs = [jnp.zeros_like(b_last)] + [b[j * sub - 1:j * sub, :] for j in range(1, n_sub)]
        rrow = jnp.concatenate([jnp.broadcast_to(r, (sub, r.shape[1])) for r in refs], axis=0)
        qe = q * jnp.exp(b - rrow)
        qb = qe * jnp.exp(rrow)
        ke = jnp.concatenate(
            [k * jnp.exp(jnp.minimum(r - b, HG_EXP_CLAMP)) for r in refs], axis=0)
        a_full = _dot_nt(qe.astype(BF16), ke.astype(BF16))
        a2 = jnp.concatenate(
            [a_full[j * sub:(j + 1) * sub, (j // 2) * 2 * c:(j // 2 + 1) * 2 * c]
             for j in range(n_sub)], axis=0)
        a2 = jnp.where(mask2, a2, 0.0).astype(BF16)
        vv = jnp.concatenate([v, v], axis=0)
        o_intra = _dot(a2, vv)
        o_inter = _dot_nt(qb.astype(BF16), st.astype(BF16))
        o_ref[sl, :] = o_inter + o_intra
        kdec = (k * jnp.exp(b_last - b)).astype(BF16)
        st = st * jnp.exp(b_last) + _dot_tn(v, kdec)
    st_ref[...] = st


def _hgrn2(q, f_logit, v, lb, batch, tb=512):
    m, d = q.shape
    s = m // batch
    nt = s // tb
    blk = lambda b, h, t: (b * nt + t, h)
    return pl.pallas_call(
        functools.partial(_hgrn2_kernel, n_chunks=tb // HG_CHUNK),
        grid=(batch, HG_HEADS, nt),
        in_specs=[pl.BlockSpec((tb, HG_KDIM), blk)] * 3
        + [pl.BlockSpec((1, HG_KDIM), lambda b, h, t: (0, h))],
        out_specs=pl.BlockSpec((tb, HG_KDIM), blk),
        out_shape=jax.ShapeDtypeStruct((m, d), F32),
        scratch_shapes=[pltpu.VMEM((HG_KDIM, HG_KDIM), F32)],
        compiler_params=_cparams(("arbitrary", "arbitrary", "arbitrary")),
        name="hgrn2_recurrence",
    )(q, f_logit, v, lb)


def _oproj_kernel(*refs, gated):
    if gated:
        o_ref, gg_ref, gout_ref, w_ref, h_ref, gate_ref, g2_ref, out_ref = refs
        gg = gg_ref[...].astype(F32)
        x = _rms(o_ref[...]) * gout_ref[...] * (gg * _sigmoid(gg))
    else:
        o_ref, w_ref, h_ref, gate_ref, g2_ref, out_ref = refs
        x = o_ref[...]
    y = _dot(x.astype(BF16), w_ref[...])
    out_ref[...] = h_ref[...] + gate_ref[0] * (_rms(y) * g2_ref[...])


def _oproj(o, w, h, gate, g2, rows_per_batch, gg=None, gout=None, tm=512):
    m, d = h.shape
    bpb = rows_per_batch // tm
    row = pl.BlockSpec((tm, d), lambda i: (i, 0))
    vec = pl.BlockSpec((1, d), lambda i: (0, 0))
    mod = pl.BlockSpec((1, 1, d), lambda i: (i // bpb, 0, 0))
    wsp = pl.BlockSpec(w.shape, lambda i: (0, 0))
    gated = gg is not None
    if gated:
        args, specs = (o, gg, gout, w, h, gate, g2), [row, row, vec, wsp, row, mod, vec]
    else:
        args, specs = (o, w, h, gate, g2), [row, wsp, row, mod, vec]
    return pl.pallas_call(
        functools.partial(_oproj_kernel, gated=gated),
        grid=(m // tm,),
        in_specs=specs,
        out_specs=row,
        out_shape=jax.ShapeDtypeStruct((m, d), F32),
        compiler_params=_cparams(("arbitrary",)),
        name="hg_out_proj" if gated else "mla_out_proj",
    )(*args)


def _ffn_kernel(h_ref, g_ref, sc_ref, sh_ref, wg_ref, wu_ref, wd_ref, gate_ref, g2_ref,
                out_ref, u_scr, acc_scr):
    j = pl.program_id(1)

    @pl.when(j == 0)
    def _():
        u = _rms(h_ref[...]) * g_ref[...] * (1.0 + sc_ref[0]) + sh_ref[0]
        u_scr[...] = u.astype(BF16)
        acc_scr[...] = jnp.zeros_like(acc_scr)

    u = u_scr[...]
    gt = _dot(u, wg_ref[...])
    up = _dot(u, wu_ref[...])
    a = (gt * _sigmoid(gt) * up).astype(BF16)
    acc_scr[...] += _dot(a, wd_ref[...])

    @pl.when(j == pl.num_programs(1) - 1)
    def _():
        out_ref[...] = h_ref[...] + gate_ref[0] * (_rms(acc_scr[...]) * g2_ref[...])


def _ffn(h, g, scale, shift, w_gu, w_down, gate, g2, rows_per_batch, tm=512, tf=512):
    m, d = h.shape
    ff = w_down.shape[0]
    nf = ff // tf
    bpb = rows_per_batch // tm
    row = pl.BlockSpec((tm, d), lambda i, j: (i, 0))
    vec = pl.BlockSpec((1, d), lambda i, j: (0, 0))
    mod = pl.BlockSpec((1, 1, d), lambda i, j: (i // bpb, 0, 0))
    return pl.pallas_call(
        _ffn_kernel,
        grid=(m // tm, nf),
        in_specs=[row, vec, mod, mod,
                  pl.BlockSpec((d, tf), lambda i, j: (0, j)),
                  pl.BlockSpec((d, tf), lambda i, j: (0, nf + j)),
                  pl.BlockSpec((tf, d), lambda i, j: (j, 0)),
                  mod, vec],
        out_specs=row,
        out_shape=jax.ShapeDtypeStruct((m, d), F32),
        scratch_shapes=[pltpu.VMEM((tm, d), BF16), pltpu.VMEM((tm, d), F32)],
        compiler_params=_cparams(("arbitrary", "arbitrary")),
        name="dense_ffn",
    )(h, g, scale, shift, w_gu, w_gu, w_down, gate, g2)


def _mla_proj_kernel(h_ref, gq_ref, scq_ref, shq_ref, gk_ref, sck_ref, shk_ref,
                     wqa_ref, qg_ref, wqm_ref, wqr_ref,
                     wkva_ref, wkr_ref, wkrr_ref, kg_ref, wk_ref, wv_ref,
                     cos_ref, sin_ref, q_ref, k_ref, v_ref):
    xhat = _rms(h_ref[...])
    uq = (xhat * gq_ref[...] * (1.0 + scq_ref[0]) + shq_ref[0]).astype(BF16)
    xk = (xhat * gk_ref[...] * (1.0 + sck_ref[0]) + shk_ref[0]).astype(BF16)
    cos = cos_ref[...]
    sin = sin_ref[...]
    qn = (_rms(_dot(uq, wqa_ref[...])) * qg_ref[...]).astype(BF16)
    cn = (_rms(_dot(xk, wkva_ref[...])) * kg_ref[...]).astype(BF16)
    krope = (_dot(xk, wkr_ref[...]) * cos + _dot(xk, wkrr_ref[...]) * sin).astype(BF16)
    for hd in range(MLA_HEADS):
        qs = slice(hd * QK_PAD, (hd + 1) * QK_PAD)
        hs = slice(hd * LANES, (hd + 1) * LANES)
        qm = _dot(qn, wqm_ref[:, qs]) * MLA_SCALE
        qr = _dot(qn, wqr_ref[:, hs]) * MLA_SCALE
        q_ref[:, hd * QK_PAD:hd * QK_PAD + QK_NOPE] = qm[:, :QK_NOPE].astype(BF16)
        q_ref[:, hd * QK_PAD + QK_NOPE:(hd + 1) * QK_PAD] = (qm[:, QK_NOPE:] * cos + qr * sin).astype(BF16)
        k_ref[:, hd * QK_PAD:hd * QK_PAD + QK_NOPE] = _dot(cn, wk_ref[:, hs]).astype(BF16)
        k_ref[:, hd * QK_PAD + QK_NOPE:(hd + 1) * QK_PAD] = krope
        v_ref[:, hs] = _dot(cn, wv_ref[:, hs]).astype(BF16)


def _mla_proj(h, gq, scq, shq, gk, sck, shk, wts, cos, sin, rows_per_batch, tm=256):
    m, d = h.shape
    bpb = rows_per_batch // tm
    row = lambda w: pl.BlockSpec((tm, w), lambda i: (i, 0))
    vec = lambda w: pl.BlockSpec((1, w), lambda i: (0, 0))
    mod = pl.BlockSpec((1, 1, d), lambda i: (i // bpb, 0, 0))
    full = lambda a: pl.BlockSpec(a.shape, lambda i: (0, 0))
    wqa, qg, wqm, wqr, wkva, wkr, wkrr, kg, wk, wv = wts
    return pl.pallas_call(
        _mla_proj_kernel,
        grid=(m // tm,),
        in_specs=[row(d), vec(d), mod, mod, vec(d), mod, mod,
                  full(wqa), full(qg), full(wqm), full(wqr),
                  full(wkva), full(wkr), full(wkrr), full(kg), full(wk), full(wv),
                  row(LANES), row(LANES)],
        out_specs=[row(MLA_HEADS * QK_PAD), row(MLA_HEADS * QK_PAD), row(MLA_HEADS * V_HEAD)],
        out_shape=[jax.ShapeDtypeStruct((m, MLA_HEADS * QK_PAD), BF16),
                   jax.ShapeDtypeStruct((m, MLA_HEADS * QK_PAD), BF16),
                   jax.ShapeDtypeStruct((m, MLA_HEADS * V_HEAD), BF16)],
        compiler_params=_cparams(("arbitrary",)),
        name="mla_proj",
    )(h, gq, scq, shq, gk, sck, shk, wqa, qg, wqm, wqr, wkva, wkr, wkrr, kg, wk, wv, cos, sin)


def _flash_kernel(q_ref, k_ref, v_ref, o_ref, *, tq, tk):
    qi = pl.program_id(2)
    q = q_ref[...]

    def step(j, carry, masked):
        m_i, l_i, acc = carry
        start = pl.multiple_of(j * tk, tk)
        k = k_ref[pl.ds(start, tk), :]
        v = v_ref[pl.ds(start, tk), :]
        s = _dot_nt(q, k)
        if masked:
            rowp = qi * tq + lax.broadcasted_iota(jnp.int32, (tq, tk), 0)
            colp = j * tk + lax.broadcasted_iota(jnp.int32, (tq, tk), 1)
            s = jnp.where(colp <= rowp, s, -jnp.inf)
        m_new = jnp.maximum(m_i, jnp.max(s, axis=-1, keepdims=True))
        p = jnp.exp(s - m_new)
        alpha = jnp.exp(m_i - m_new)
        l_new = alpha * l_i + jnp.sum(p, axis=-1, keepdims=True)
        acc = alpha * acc + _dot(p.astype(BF16), v)
        return m_new, l_new, acc

    init = (jnp.full((tq, 1), -jnp.inf, F32), jnp.zeros((tq, 1), F32),
            jnp.zeros((tq, V_HEAD), F32))
    n_full = (qi * tq) // tk
    carry = lax.fori_loop(0, n_full, functools.partial(step, masked=False), init)
    for dgn in range(tq // tk):
        carry = step(n_full + dgn, carry, True)
    _, l_i, acc = carry
    o_ref[...] = (acc / l_i).astype(o_ref.dtype)


def _flash(q, k, v, batch, tq=512, tk=512):
    m = q.shape[0]
    s = m // batch
    nq = s // tq
    return pl.pallas_call(
        functools.partial(_flash_kernel, tq=tq, tk=tk),
        grid=(batch, MLA_HEADS, nq),
        in_specs=[pl.BlockSpec((tq, QK_PAD), lambda b, h, i: (b * nq + i, h)),
                  pl.BlockSpec((s, QK_PAD), lambda b, h, i: (b, h)),
                  pl.BlockSpec((s, V_HEAD), lambda b, h, i: (b, h))],
        out_specs=pl.BlockSpec((tq, V_HEAD), lambda b, h, i: (b * nq + i, h)),
        out_shape=jax.ShapeDtypeStruct((m, MLA_HEADS * V_HEAD), BF16),
        compiler_params=_cparams(("arbitrary", "arbitrary", "arbitrary")),
        name="mla_flash",
    )(q, k, v)


def _split3(x):
    hi = x.astype(BF16)
    lo = (x - hi.astype(F32)).astype(BF16)
    return hi, lo


def _moe_pre_kernel(h_ref, g_ref, sc_ref, sh_ref, wr_ref, xs_ref, meta_ref, wts_ref, cnt_ref,
                    run_scr):
    tm, d = h_ref.shape

    @pl.when(pl.program_id(0) == 0)
    def _():
        run_scr[...] = jnp.zeros_like(run_scr)

    u = _rms(h_ref[...]) * g_ref[...] * (1.0 + sc_ref[0]) + sh_ref[0]
    half = d // 2
    lo = pltpu.bitcast(u[:, :half].astype(BF16).astype(F32), jnp.uint32) >> 16
    hi = pltpu.bitcast(u[:, half:].astype(BF16).astype(F32), jnp.uint32)
    xs_ref[...] = hi | lo

    u_hi, u_lo = _split3(u)
    w_hi, w_lo = _split3(wr_ref[...])
    logits = _dot(u_hi, w_hi) + (_dot(u_hi, w_lo) + _dot(u_lo, w_hi))
    lane = lax.broadcasted_iota(jnp.int32, logits.shape, 1)
    lg = jnp.where(lane < N_EXPERTS, logits, -jnp.inf)
    m1 = jnp.max(lg, axis=-1, keepdims=True)
    i1 = jnp.min(jnp.where(lg == m1, lane, LANES), axis=-1, keepdims=True)
    lg2 = jnp.where(lane == i1, -jnp.inf, lg)
    m2 = jnp.max(lg2, axis=-1, keepdims=True)
    i2 = jnp.min(jnp.where(lg2 == m2, lane, LANES), axis=-1, keepdims=True)
    e = jnp.exp(m2 - m1)
    w1 = 1.0 / (1.0 + e)
    w2 = e * w1

    oh1 = lane == i1
    oh2 = lane == i2
    cnt = oh1.astype(F32) + oh2.astype(F32)
    r = lax.broadcasted_iota(jnp.int32, (tm, tm), 0)
    c = lax.broadcasted_iota(jnp.int32, (tm, tm), 1)
    strict = (r > c).astype(BF16)
    prefix = _dot(strict, cnt.astype(BF16)) + run_scr[...]
    rank1 = jnp.sum(jnp.where(oh1, prefix, 0.0), axis=-1, keepdims=True).astype(jnp.int32)
    rank2 = jnp.sum(jnp.where(oh2, prefix, 0.0), axis=-1, keepdims=True).astype(jnp.int32)
    run_scr[...] += jnp.sum(cnt, axis=0, keepdims=True)

    meta_ref[...] = jnp.where(lane == 0, i1, jnp.where(lane == 1, i2,
                              jnp.where(lane == 2, rank1, jnp.where(lane == 3, rank2, 0))))
    wts_ref[...] = jnp.where(lane == 0, w1, jnp.where(lane == 1, w2, 0.0))
    cnt_ref[...] = run_scr[...]


def _moe_pre(h, g, scale, shift, w_router_pad, rows_per_batch, tm=512):
    m, d = h.shape
    bpb = rows_per_batch // tm
    row = lambda w: pl.BlockSpec((tm, w), lambda i: (i, 0))
    vec = lambda w: pl.BlockSpec((1, w), lambda i: (0, 0))
    mod = pl.BlockSpec((1, 1, d), lambda i: (i // bpb, 0, 0))
    return pl.pallas_call(
        _moe_pre_kernel,
        grid=(m // tm,),
        in_specs=[row(d), vec(d), mod, mod, pl.BlockSpec((d, LANES), lambda i: (0, 0))],
        out_specs=[row(d // 2), row(LANES), row(LANES), vec(LANES)],
        out_shape=[jax.ShapeDtypeStruct((m, d // 2), jnp.uint32),
                   jax.ShapeDtypeStruct((m, LANES), jnp.int32),
                   jax.ShapeDtypeStruct((m, LANES), F32),
                   jax.ShapeDtypeStruct((1, LANES), F32)],
        scratch_shapes=[pltpu.VMEM((1, LANES), F32)],
        compiler_params=_cparams(("arbitrary",)),
        name="moe_route",
    )(h, g, scale, shift, w_router_pad)


def _moe_dispatch_kernel(dest_ref, x_ref, xs_in_ref, xs_ref, sem):
    del xs_in_ref
    tm = x_ref.shape[0]

    def row_copy(r, k):
        return pltpu.make_async_copy(x_ref.at[pl.ds(r, 1), :],
                                     xs_ref.at[pl.ds(dest_ref[0, 0, 2 * r + k], 1), :], sem)

    def issue(r, carry):
        row_copy(r, 0).start()
        row_copy(r, 1).start()
        return carry

    lax.fori_loop(0, tm, issue, 0)

    def drain(r, carry):
        row_copy(r, 0).wait()
        row_copy(r, 1).wait()
        return carry

    lax.fori_loop(0, tm, drain, 0)


def _moe_dispatch(x, dest, xs_init, tm=256):
    m, w = x.shape
    nb = m // tm
    return pl.pallas_call(
        _moe_dispatch_kernel,
        grid=(nb,),
        in_specs=[pl.BlockSpec((1, 1, 2 * tm), lambda i: (i, 0, 0), memory_space=pltpu.SMEM),
                  pl.BlockSpec((tm, w), lambda i: (i, 0)),
                  pl.BlockSpec(memory_space=pl.ANY)],
        out_specs=pl.BlockSpec(memory_space=pl.ANY),
        out_shape=jax.ShapeDtypeStruct(xs_init.shape, xs_init.dtype),
        scratch_shapes=[pltpu.SemaphoreType.DMA(())],
        input_output_aliases={2: 0},
        compiler_params=_cparams(("arbitrary",)),
        name="moe_dispatch",
    )(dest.reshape(nb, 1, 2 * tm), x, xs_init)


def _moe_ffn_kernel(blk_e_ref, n_used_ref, xs_ref, wg_ref, wu_ref, wd_ref, y_ref, u_scr):
    i = pl.program_id(0)
    j = pl.program_id(1)

    @pl.when(i < n_used_ref[0])
    def _():
        @pl.when(j == 0)
        def _():
            w = xs_ref[...]
            half = w.shape[1]
            u_scr[:, :half] = pltpu.bitcast(w << 16, F32).astype(BF16)
            u_scr[:, half:] = pltpu.bitcast(w & jnp.uint32(0xFFFF0000), F32).astype(BF16)
            y_ref[...] = jnp.zeros_like(y_ref)

        u = u_scr[...]
        gt = _dot(u, wg_ref[0])
        up = _dot(u, wu_ref[0])
        a = (gt * _sigmoid(gt) * up).astype(BF16)
        y_ref[...] += _dot(a, wd_ref[0])

    @pl.when((i >= n_used_ref[0]) & (j == 0))
    def _():
        y_ref[...] = jnp.zeros_like(y_ref)


def _moe_ffn(xs, blk_e, n_used, w_gu, w_down, tmb, tf=512):
    n_rows, half = xs.shape
    d = 2 * half
    ff = w_down.shape[1]
    nf = ff // tf
    nb = n_rows // tmb

    def blk(i, n_used_ref):
        return jnp.minimum(i, n_used_ref[0] - 1)

    def fidx(i, j, n_used_ref):
        return jnp.where(i < n_used_ref[0], j, nf - 1)

    grid_spec = pltpu.PrefetchScalarGridSpec(
        num_scalar_prefetch=2,
        grid=(nb, nf),
        in_specs=[
            pl.BlockSpec((tmb, half), lambda i, j, be, nu: (blk(i, nu), 0)),
            pl.BlockSpec((1, d, tf), lambda i, j, be, nu: (be[blk(i, nu)], 0, fidx(i, j, nu))),
            pl.BlockSpec((1, d, tf), lambda i, j, be, nu: (be[blk(i, nu)], 0, nf + fidx(i, j, nu))),
            pl.BlockSpec((1, tf, d), lambda i, j, be, nu: (be[blk(i, nu)], fidx(i, j, nu), 0)),
        ],
        out_specs=pl.BlockSpec((tmb, d), lambda i, j, be, nu: (i, 0)),
        scratch_shapes=[pltpu.VMEM((tmb, d), BF16)],
    )
    return pl.pallas_call(
        _moe_ffn_kernel,
        grid_spec=grid_spec,
        out_shape=jax.ShapeDtypeStruct((n_rows, d), F32),
        compiler_params=_cparams(("arbitrary", "arbitrary")),
        name="moe_grouped_ffn",
    )(blk_e, n_used, xs, w_gu, w_gu, w_down)


def _moe_combine_kernel(dest_ref, ys_ref, wts_ref, h_ref, gate_ref, g2_ref, out_ref,
                        buf0, buf1, sem):
    tm = h_ref.shape[0]
    bufs = (buf0, buf1)

    def row_copy(r, k):
        return pltpu.make_async_copy(ys_ref.at[pl.ds(dest_ref[0, 0, 2 * r + k], 1), :],
                                     bufs[k].at[pl.ds(r, 1), :], sem)

    def issue(r, carry):
        row_copy(r, 0).start()
        row_copy(r, 1).start()
        return carry

    lax.fori_loop(0, tm, issue, 0)

    def drain(r, carry):
        row_copy(r, 0).wait()
        row_copy(r, 1).wait()
        return carry

    lax.fori_loop(0, tm, drain, 0)

    wts = wts_ref[...]
    y = wts[:, 0:1] * buf0[...] + wts[:, 1:2] * buf1[...]
    out_ref[...] = h_ref[...] + gate_ref[0] * (_rms(y) * g2_ref[...])


def _moe_combine(ys, dest, wts, h, gate, g2, rows_per_batch, tm=256):
    m, d = h.shape
    nb = m // tm
    bpb = rows_per_batch // tm
    row = lambda w: pl.BlockSpec((tm, w), lambda i: (i, 0))
    return pl.pallas_call(
        _moe_combine_kernel,
        grid=(nb,),
        in_specs=[pl.BlockSpec((1, 1, 2 * tm), lambda i: (i, 0, 0), memory_space=pltpu.SMEM),
                  pl.BlockSpec(memory_space=pl.ANY),
                  row(LANES), row(d),
                  pl.BlockSpec((1, 1, d), lambda i: (i // bpb, 0, 0)),
                  pl.BlockSpec((1, d), lambda i: (0, 0))],
        out_specs=row(d),
        out_shape=jax.ShapeDtypeStruct((m, d), F32),
        scratch_shapes=[pltpu.VMEM((tm, d), F32), pltpu.VMEM((tm, d), F32),
                        pltpu.SemaphoreType.DMA(())],
        compiler_params=_cparams(("arbitrary",)),
        name="moe_combine",
    )(dest.reshape(nb, 1, 2 * tm), ys, wts, h, gate, g2)


def _moe(h, g, scale, shift, w_router, w_gu, w_down, gate, g2, rows_per_batch, tmb=512):
    m, d = h.shape
    w_router_pad = jnp.pad(w_router, ((0, 0), (0, LANES - N_EXPERTS)))
    x_packed, meta, wts, cnt = _moe_pre(h, g, scale, shift, w_router_pad, rows_per_batch)

    counts = cnt[0, :N_EXPERTS].astype(jnp.int32)
    padded = (counts + tmb - 1) // tmb * tmb
    pend = jnp.cumsum(padded)
    pstart = pend - padded
    dest = (pstart[meta[:, 0:2]] + meta[:, 2:4]).astype(jnp.int32).reshape(-1)
    n_rows = (2 * m + N_EXPERTS * (tmb - 1)) // tmb * tmb
    nb = n_rows // tmb
    blk_e = jnp.minimum(jnp.searchsorted(pend, jnp.arange(nb, dtype=jnp.int32) * tmb, side='right'),
                        N_EXPERTS - 1).astype(jnp.int32)
    n_used = (pend[-1:] // tmb).astype(jnp.int32)

    xs = _moe_dispatch(x_packed, dest, jnp.zeros((n_rows, d // 2), jnp.uint32))
    ys = _moe_ffn(xs, blk_e, n_used, w_gu, w_down, tmb)
    return _moe_combine(ys, dest, wts, h, gate, g2, rows_per_batch)


def _swap_halves(w):
    half = w.shape[-1] // 2
    return jnp.concatenate([w[..., half:], w[..., :half]], axis=-1)


def _mla_weights(w_q_a, q_norm_g, w_q_b, w_kv_a, kv_norm_g, w_kv_b):
    ql = w_q_b.shape[0]
    kvl = w_kv_b.shape[0]
    wq = w_q_b.reshape(ql, MLA_HEADS, QK_NOPE + QK_ROPE)
    rope = wq[:, :, QK_NOPE:]
    pad_r = LANES - QK_ROPE
    wqm = jnp.concatenate([wq, jnp.zeros((ql, MLA_HEADS, QK_PAD - QK_NOPE - QK_ROPE), F32)], axis=-1)
    wqr = jnp.concatenate([_swap_halves(rope), jnp.zeros((ql, MLA_HEADS, pad_r), F32)], axis=-1)
    wkv = w_kv_b.reshape(kvl, MLA_HEADS, QK_NOPE + V_HEAD)
    kr = w_kv_a[:, kvl:]
    dm = w_kv_a.shape[0]
    wkr = jnp.concatenate([kr, jnp.zeros((dm, pad_r), F32)], axis=-1)
    wkrr = jnp.concatenate([_swap_halves(kr), jnp.zeros((dm, pad_r), F32)], axis=-1)
    bf = lambda a: a.astype(BF16)
    return (bf(w_q_a), q_norm_g.reshape(1, -1), bf(wqm.reshape(ql, -1)), bf(wqr.reshape(ql, -1)),
            bf(w_kv_a[:, :kvl]), bf(wkr), bf(wkrr), kv_norm_g.reshape(1, -1),
            bf(wkv[:, :, :QK_NOPE].reshape(kvl, -1)), bf(wkv[:, :, QK_NOPE:].reshape(kvl, -1)))


def kernel(x, c, positions, ada_w, ada_b, norm_g, hg_w_in, hg_lb_logits, hg_out_norm_g, hg_w_out, kv_src_norm_g, kv_src_ada_w, kv_src_ada_b, mla_w_kv_a, mla_kv_norm_g, mla_w_kv_b, mla_w_q_a, mla_q_norm_g, mla_w_q_b, mla_w_o, ffn_w_gu, ffn_w_down, moe_w_router, moe_w_gu, moe_w_down):
    batch, seq, d = x.shape
    m = batch * seq
    bf = lambda a: a.astype(BF16)

    c8 = jnp.pad(c, ((0, 8 - batch), (0, 0)))
    ada = _modulation(c8, ada_w.reshape(4, d, 3 * d), ada_b.reshape(4, 3 * d))[:, :batch]
    kvm = _modulation(c8, kv_src_ada_w[None], kv_src_ada_b[None])[0, :batch]

    def mods(idx):
        a = ada[idx]
        return [a[:, i * d:(i + 1) * d].reshape(batch, 1, d) for i in range(3)]

    vec = lambda a: a.reshape(1, -1)
    lb_all = jnp.cumsum(jax.nn.softmax(hg_lb_logits.astype(F32), axis=0), axis=0)

    h = x.reshape(m, d)

    shift, scale, gate = mods(0)
    q, f_logit, iv, gg = _hg_proj(h, vec(norm_g[0, 0, 0]), scale, shift, bf(hg_w_in[0]), seq)
    o = _hgrn2(q, f_logit, iv, vec(lb_all[0]), batch)
    h = _oproj(o, bf(hg_w_out[0]), h, gate, vec(norm_g[0, 0, 1]), seq,
               gg=gg, gout=vec(hg_out_norm_g[0]))

    shift, scale, gate = mods(1)
    h = _ffn(h, vec(norm_g[0, 1, 0]), scale, shift, bf(ffn_w_gu[0]), bf(ffn_w_down[0]),
             gate, vec(norm_g[0, 1, 1]), seq)

    shift_k, scale_k = [kvm[:, i * d:(i + 1) * d].reshape(batch, 1, d) for i in range(2)]
    shift, scale, gate = mods(2)
    cos, sin = _rope_tables(positions.reshape(m, 1).astype(F32))
    wts = _mla_weights(mla_w_q_a[0], mla_q_norm_g[0], mla_w_q_b[0],
                       mla_w_kv_a, mla_kv_norm_g, mla_w_kv_b)
    qh, kh, vh = _mla_proj(h, vec(norm_g[1, 0, 0]), scale, shift,
                           vec(kv_src_norm_g), scale_k, shift_k, wts, cos, sin, seq)
    att = _flash(qh, kh, vh, batch)
    h = _oproj(att, bf(mla_w_o[0]), h, gate, vec(norm_g[1, 0, 1]), seq)

    shift, scale, gate = mods(3)
    h = _moe(h, vec(norm_g[1, 1, 0]), scale, shift, moe_w_router[0],
             bf(moe_w_gu[0]), bf(moe_w_down[0]), gate, vec(norm_g[1, 1, 1]), seq)
    return h.reshape(batch, seq, d)
```

```python
import functools

import jax
import jax.numpy as jnp
from jax import lax
from jax.experimental import pallas as pl
from jax.experimental.pallas import tpu as pltpu

F32 = jnp.float32
BF16 = jnp.bfloat16

EPS = 1e-6
LANES = 128
BF16_SUBLANES = 16
SIDE_BLOCK_BYTES = 2 * 1024 * 1024

HG_HEADS = 16
HG_KDIM = 128
HG_CHUNK = 64
HG_SUB = 16
HG_EXP_CLAMP = 80.0

MLA_HEADS = 16
QK_NOPE = 128
QK_ROPE = 64
V_HEAD = 128
QK_PAD = 256
MLA_SCALE = (QK_NOPE + QK_ROPE) ** -0.5
Q_SCALE = MLA_SCALE * 1.4426950408889634
ROPE_THETA = 10000.0
FLASH_TILE = 512

N_EXPERTS = 8

VMEM_LIMIT = 56 * 1024 * 1024


def _cparams(sem):
    return pltpu.CompilerParams(dimension_semantics=sem, vmem_limit_bytes=VMEM_LIMIT)


def _rms(x):
    return x * lax.rsqrt(jnp.mean(x * x, axis=-1, keepdims=True) + EPS)


def _sigmoid(x):
    return 1.0 / (1.0 + jnp.exp(-x))


def _dot(a, b):
    return jnp.dot(a, b, preferred_element_type=F32)


def _dot_nt(a, b):
    return lax.dot_general(a, b, (((1,), (1,)), ((), ())), preferred_element_type=F32)


def _dot_tn(a, b):
    return lax.dot_general(a, b, (((0,), (0,)), ((), ())), preferred_element_type=F32)


def _mod_kernel(c_ref, w_ref, b_ref, o_ref):
    c = c_ref[...]
    sc = (c * _sigmoid(c)).astype(BF16)
    o_ref[0] = _dot(sc, w_ref[0].astype(BF16)) + b_ref[0]


def _modulation(c8, w, b, tn=1024):
    g, d, n = w.shape
    return pl.pallas_call(
        _mod_kernel,
        grid=(g, n // tn),
        in_specs=[pl.BlockSpec((8, d), lambda i, j: (0, 0)),
                  pl.BlockSpec((1, d, tn), lambda i, j: (i, 0, j)),
                  pl.BlockSpec((1, 1, tn), lambda i, j: (i, 0, j))],
        out_specs=pl.BlockSpec((1, 8, tn), lambda i, j: (i, 0, j)),
        out_shape=jax.ShapeDtypeStruct((g, 8, n), F32),
        compiler_params=_cparams(("arbitrary", "arbitrary")),
        name="modulation",
    )(c8, w, b.reshape(g, 1, n))


def _rope_kernel(pos_ref, inv_ref, cos_ref, sin_ref):
    ang = pos_ref[...] * inv_ref[...]
    lane = lax.broadcasted_iota(jnp.int32, ang.shape, 1)
    half = QK_ROPE // 2
    cos_ref[...] = jnp.where(lane < QK_ROPE, jnp.cos(ang), 0.0)
    sin_ref[...] = jnp.where(lane < half, -jnp.sin(ang),
                             jnp.where(lane < QK_ROPE, jnp.sin(ang), 0.0))


def _rope_tables(pos, tm=512):
    m = pos.shape[0]
    half = QK_ROPE // 2
    inv = 1.0 / (ROPE_THETA ** (jnp.arange(0, QK_ROPE, 2, dtype=F32) / QK_ROPE))
    inv128 = jnp.concatenate([inv, inv, jnp.zeros((LANES - 2 * half,), F32)]).reshape(1, LANES)
    return pl.pallas_call(
        _rope_kernel,
        grid=(m // tm,),
        in_specs=[pl.BlockSpec((tm, 1), lambda i: (i, 0)),
                  pl.BlockSpec((1, LANES), lambda i: (0, 0))],
        out_specs=[pl.BlockSpec((tm, LANES), lambda i: (i, 0))] * 2,
        out_shape=[jax.ShapeDtypeStruct((m, LANES), F32)] * 2,
        compiler_params=_cparams(("arbitrary",)),
        name="rope_tables",
    )(pos, inv128)


def _hg_proj_kernel(h_ref, g_ref, sc_ref, sh_ref, w_ref, q_ref, f_ref, i_ref, gg_ref, u_scr):
    j = pl.program_id(1)

    @pl.when(j == 0)
    def _():
        u = _rms(h_ref[...]) * g_ref[...] * (1.0 + sc_ref[0]) + sh_ref[0]
        u_scr[...] = u.astype(BF16)

    r = _dot(u_scr[...], w_ref[...])
    for idx, ref in enumerate((q_ref, f_ref, i_ref, gg_ref)):
        @pl.when(j == idx)
        def _(ref=ref):
            ref[...] = r.astype(ref.dtype)


def _hg_proj(h, g, scale, shift, w_in, rows_per_batch, tm=512):
    m, d = h.shape
    bpb = rows_per_batch // tm
    row = lambda i, j: (i, 0)
    mod = lambda i, j: (i // bpb, 0, 0)
    return pl.pallas_call(
        _hg_proj_kernel,
        grid=(m // tm, 4),
        in_specs=[pl.BlockSpec((tm, d), row),
                  pl.BlockSpec((1, d), lambda i, j: (0, 0)),
                  pl.BlockSpec((1, 1, d), mod),
                  pl.BlockSpec((1, 1, d), mod),
                  pl.BlockSpec((d, d), lambda i, j: (0, j))],
        out_specs=[pl.BlockSpec((tm, d), row)] * 4,
        out_shape=[jax.ShapeDtypeStruct((m, d), BF16), jax.ShapeDtypeStruct((m, d), F32),
                   jax.ShapeDtypeStruct((m, d), BF16), jax.ShapeDtypeStruct((m, d), BF16)],
        scratch_shapes=[pltpu.VMEM((tm, d), BF16)],
        compiler_params=_cparams(("arbitrary", "arbitrary")),
        name="hg_proj",
    )(h, g, scale, shift, w_in)


def _hgrn2_kernel(q_ref, f_ref, v_ref, lb_ref, o_ref, st_ref, *, n_chunks):
    c, sub = HG_CHUNK, HG_SUB
    n_sub = c // sub

    @pl.when(pl.program_id(2) == 0)
    def _():
        st_ref[...] = jnp.zeros_like(st_ref)

    lb = lb_ref[...]
    row = lax.broadcasted_iota(jnp.int32, (c, c), 0)
    col = lax.broadcasted_iota(jnp.int32, (c, c), 1)
    tri = (row >= col).astype(BF16)
    row2 = lax.broadcasted_iota(jnp.int32, (c, 2 * c), 0)
    col2 = lax.broadcasted_iota(jnp.int32, (c, 2 * c), 1)
    c_bits, sub_bits = c.bit_length() - 1, sub.bit_length() - 1
    mask2 = ((col2 >> c_bits) == ((row2 >> sub_bits) & 1)) & ((col2 & (c - 1)) <= row2)

    chunks = [slice(ci * c, (ci + 1) * c) for ci in range(n_chunks)]
    ks, bs = [], []
    for sl in chunks:
        f = lb + (1.0 - lb) * _sigmoid(f_ref[sl, :])
        lf = jnp.log(f)
        lf_hi = lf.astype(BF16)
        lf_lo = (lf - lf_hi.astype(F32)).astype(BF16)
        ks.append(1.0 - f)
        bs.append(_dot(tri, lf_hi) + _dot(tri, lf_lo))
    a_fulls, qbs, kdecs, decays = [], [], [], []
    for sl, k, b in zip(chunks, ks, bs):
        q = q_ref[sl, :].astype(F32)
        b_last = b[c - 1:c, :]
        refs = [jnp.zeros_like(b_last)] + [b[j * sub - 1:j * sub, :] for j in range(1, n_sub)]
        rrow = jnp.concatenate([jnp.broadcast_to(r, (sub, r.shape[1])) for r in refs], axis=0)
        qe = q * jnp.exp(b - rrow)
        qbs.append((qe * jnp.exp(rrow)).astype(BF16))
        ke = jnp.concatenate(
            [k * jnp.exp(jnp.minimum(r - b, HG_EXP_CLAMP)) for r in refs], axis=0)
        a_fulls.append(_dot_nt(qe.astype(BF16), ke.astype(BF16)))
        kdecs.append((k * jnp.exp(b_last - b)).astype(BF16))
        decays.append(jnp.exp(b_last))
    o_intras, incs = [], []
    for sl, a_full, kdec in zip(chunks, a_fulls, kdecs):
        v = v_ref[sl, :]
        a2 = jnp.concatenate(
            [a_full[j * sub:(j + 1) * sub, (j // 2) * 2 * c:(j // 2 + 1) * 2 * c]
             for j in range(n_sub)], axis=0)
        a2 = jnp.where(mask2, a2, 0.0).astype(BF16)
        o_intras.append(_dot(a2, jnp.concatenate([v, v], axis=0)))
        incs.append(_dot_tn(v, kdec))
    st = st_ref[...]
    for sl, qb, o_intra, inc, decay in zip(chunks, qbs, o_intras, incs, decays):
        o_ref[sl, :] = o_intra + _dot_nt(qb, st.astype(BF16))
        st = st * decay + inc
    st_ref[...] = st


def _hgrn2(q, f_logit, v, lb, batch, tb=512):
    m, d = q.shape
    s = m // batch
    nt = s // tb
    blk = lambda b, h, t: (b * nt + t, h)
    return pl.pallas_call(
        functools.partial(_hgrn2_kernel, n_chunks=tb // HG_CHUNK),
        grid=(batch, HG_HEADS, nt),
        in_specs=[pl.BlockSpec((tb, HG_KDIM), blk)] * 3
        + [pl.BlockSpec((1, HG_KDIM), lambda b, h, t: (0, h))],
        out_specs=pl.BlockSpec((tb, HG_KDIM), blk),
        out_shape=jax.ShapeDtypeStruct((m, d), F32),
        scratch_shapes=[pltpu.VMEM((HG_KDIM, HG_KDIM), F32)],
        compiler_params=_cparams(("arbitrary", "arbitrary", "arbitrary")),
        name="hgrn2_recurrence",
    )(q, f_logit, v, lb)


def _oproj_kernel(*refs, gated):
    if gated:
        o_ref, gg_ref, gout_ref, w_ref, h_ref, gate_ref, g2_ref, out_ref = refs
        gg = gg_ref[...].astype(F32)
        x = _rms(o_ref[...]) * gout_ref[...] * (gg * _sigmoid(gg))
    else:
        o_ref, w_ref, h_ref, gate_ref, g2_ref, out_ref = refs
        x = o_ref[...]
    y = _dot(x.astype(BF16), w_ref[...])
    out_ref[...] = h_ref[...] + gate_ref[0] * (_rms(y) * g2_ref[...])


def _oproj(o, w, h, gate, g2, rows_per_batch, gg=None, gout=None, tm=512):
    m, d = h.shape
    bpb = rows_per_batch // tm
    row = pl.BlockSpec((tm, d), lambda i: (i, 0))
    vec = pl.BlockSpec((1, d), lambda i: (0, 0))
    mod = pl.BlockSpec((1, 1, d), lambda i: (i // bpb, 0, 0))
    wsp = pl.BlockSpec(w.shape, lambda i: (0, 0))
    gated = gg is not None
    if gated:
        args, specs = (o, gg, gout, w, h, gate, g2), [row, row, vec, wsp, row, mod, vec]
    else:
        args, specs = (o, w, h, gate, g2), [row, wsp, row, mod, vec]
    return pl.pallas_call(
        functools.partial(_oproj_kernel, gated=gated),
        grid=(m // tm,),
        in_specs=specs,
        out_specs=row,
        out_shape=jax.ShapeDtypeStruct((m, d), F32),
        compiler_params=_cparams(("arbitrary",)),
        name="hg_out_proj" if gated else "mla_out_proj",
    )(*args)


def _ffn_kernel(h_ref, g_ref, sc_ref, sh_ref, wg_ref, wu_ref, wd_ref, gate_ref, g2_ref,
                out_ref, u_scr, acc_scr):
    j = pl.program_id(1)

    @pl.when(j == 0)
    def _():
        u = _rms(h_ref[...]) * g_ref[...] * (1.0 + sc_ref[0]) + sh_ref[0]
        u_scr[...] = u.astype(BF16)
        acc_scr[...] = jnp.zeros_like(acc_scr)

    u = u_scr[...]
    gt = _dot(u, wg_ref[...])
    up = _dot(u, wu_ref[...])
    a = (gt * _sigmoid(gt) * up).astype(BF16)
    acc_scr[...] += _dot(a, wd_ref[...])

    @pl.when(j == pl.num_programs(1) - 1)
    def _():
        out_ref[...] = h_ref[...] + gate_ref[0] * (_rms(acc_scr[...]) * g2_ref[...])


def _ffn(h, g, scale, shift, w_gu, w_down, gate, g2, rows_per_batch, tm=512, tf=512):
    m, d = h.shape
    ff = w_down.shape[0]
    nf = ff // tf
    bpb = rows_per_batch // tm
    row = pl.BlockSpec((tm, d), lambda i, j: (i, 0))
    vec = pl.BlockSpec((1, d), lambda i, j: (0, 0))
    mod = pl.BlockSpec((1, 1, d), lambda i, j: (i // bpb, 0, 0))
    return pl.pallas_call(
        _ffn_kernel,
        grid=(m // tm, nf),
        in_specs=[row, vec, mod, mod,
                  pl.BlockSpec((d, tf), lambda i, j: (0, j)),
                  pl.BlockSpec((d, tf), lambda i, j: (0, nf + j)),
                  pl.BlockSpec((tf, d), lambda i, j: (j, 0)),
                  mod, vec],
        out_specs=row,
        out_shape=jax.ShapeDtypeStruct((m, d), F32),
        scratch_shapes=[pltpu.VMEM((tm, d), BF16), pltpu.VMEM((tm, d), F32)],
        compiler_params=_cparams(("arbitrary", "arbitrary")),
        name="dense_ffn",
    )(h, g, scale, shift, w_gu, w_gu, w_down, gate, g2)


def _mla_proj_kernel(h_ref, gq_ref, scq_ref, shq_ref, gk_ref, sck_ref, shk_ref,
                     wqa_ref, qg_ref, wqm_ref, wqr_ref,
                     wkva_ref, wkr_ref, wkrr_ref, kg_ref, wk_ref, wv_ref,
                     cos_ref, sin_ref, q_ref, k_ref, v_ref):
    xhat = _rms(h_ref[...])
    uq = (xhat * gq_ref[...] * (1.0 + scq_ref[0]) + shq_ref[0]).astype(BF16)
    xk = (xhat * gk_ref[...] * (1.0 + sck_ref[0]) + shk_ref[0]).astype(BF16)
    cos = cos_ref[...]
    sin = sin_ref[...]
    qn = (_rms(_dot(uq, wqa_ref[...])) * qg_ref[...]).astype(BF16)
    cn = (_rms(_dot(xk, wkva_ref[...])) * kg_ref[...]).astype(BF16)
    krope = (_dot(xk, wkr_ref[...]) * cos + _dot(xk, wkrr_ref[...]) * sin).astype(BF16)
    for hd in range(MLA_HEADS):
        qs = slice(hd * QK_PAD, (hd + 1) * QK_PAD)
        hs = slice(hd * LANES, (hd + 1) * LANES)
        qm = _dot(qn, wqm_ref[:, qs]) * Q_SCALE
        qr = _dot(qn, wqr_ref[:, hs]) * Q_SCALE
        q_ref[:, hd * QK_PAD:hd * QK_PAD + QK_NOPE] = qm[:, :QK_NOPE].astype(BF16)
        q_ref[:, hd * QK_PAD + QK_NOPE:(hd + 1) * QK_PAD] = (qm[:, QK_NOPE:] * cos + qr * sin).astype(BF16)
        k_ref[:, hd * QK_PAD:hd * QK_PAD + QK_NOPE] = _dot(cn, wk_ref[:, hs]).astype(BF16)
        k_ref[:, hd * QK_PAD + QK_NOPE:(hd + 1) * QK_PAD] = krope
        v_ref[:, 2 * hd * V_HEAD:(2 * hd + 1) * V_HEAD] = _dot(cn, wv_ref[:, hs]).astype(BF16)
        v_ref[:, (2 * hd + 1) * V_HEAD:(2 * hd + 2) * V_HEAD] = jnp.ones((cn.shape[0], V_HEAD), BF16)


def _mla_proj(h, gq, scq, shq, gk, sck, shk, wts, cos, sin, rows_per_batch, tm=256):
    m, d = h.shape
    bpb = rows_per_batch // tm
    row = lambda w: pl.BlockSpec((tm, w), lambda i: (i, 0))
    vec = lambda w: pl.BlockSpec((1, w), lambda i: (0, 0))
    mod = pl.BlockSpec((1, 1, d), lambda i: (i // bpb, 0, 0))
    full = lambda a: pl.BlockSpec(a.shape, lambda i: (0, 0))
    wqa, qg, wqm, wqr, wkva, wkr, wkrr, kg, wk, wv = wts
    return pl.pallas_call(
        _mla_proj_kernel,
        grid=(m // tm,),
        in_specs=[row(d), vec(d), mod, mod, vec(d), mod, mod,
                  full(wqa), full(qg), full(wqm), full(wqr),
                  full(wkva), full(wkr), full(wkrr), full(kg), full(wk), full(wv),
                  row(LANES), row(LANES)],
        out_specs=[row(MLA_HEADS * QK_PAD), row(MLA_HEADS * QK_PAD), row(MLA_HEADS * 2 * V_HEAD)],
        out_shape=[jax.ShapeDtypeStruct((m, MLA_HEADS * QK_PAD), BF16),
                   jax.ShapeDtypeStruct((m, MLA_HEADS * QK_PAD), BF16),
                   jax.ShapeDtypeStruct((m, MLA_HEADS * 2 * V_HEAD), BF16)],
        compiler_params=_cparams(("arbitrary",)),
        name="mla_proj",
    )(h, gq, scq, shq, gk, sck, shk, wqa, qg, wqm, wqr, wkva, wkr, wkrr, kg, wk, wv, cos, sin)


def _flash_kernel(*refs, t, slab, side_blocks):
    n_side = len(side_blocks)
    q_ref, k_ref, v_ref = refs[:3]
    side_in = refs[3:3 + n_side]
    o_ref = refs[3 + n_side]
    side_out = refs[4 + n_side:4 + 2 * n_side]
    s_a, s_b, p_scr, m_scr, al_scr, acc_scr = refs[4 + 2 * n_side:]
    qi = pl.program_id(2)

    step_id = (pl.program_id(0) * pl.num_programs(1) + pl.program_id(1)) * pl.num_programs(2) + qi
    for w_in, w_out, n_blk in zip(side_in, side_out, side_blocks):
        @pl.when(step_id < n_blk)
        def _(w_in=w_in, w_out=w_out):
            w_out[...] = w_in[...].astype(w_out.dtype)

    q = q_ref[...]
    m_scr[...] = jnp.full_like(m_scr, -jnp.inf)
    acc_scr[...] = jnp.zeros_like(acc_scr)
    n_col = t // LANES

    def kv_rows(j):
        return pl.ds(pl.multiple_of(j * t, t), t)

    def scores(j, dst):
        dst[...] = _dot_nt(q, k_ref[kv_rows(j), :])

    def consume(j, src, masked):
        for r0 in range(0, t, slab):
            rows = slice(r0, r0 + slab)
            cols = [src[rows, cb * LANES:(cb + 1) * LANES] for cb in range(n_col)]
            if masked:
                rowp = r0 + lax.broadcasted_iota(jnp.int32, (slab, LANES), 0)
                lane = lax.broadcasted_iota(jnp.int32, (slab, LANES), 1)
                cols = [jnp.where(cb * LANES + lane <= rowp, cols[cb], -jnp.inf)
                        for cb in range(n_col)]
            mx = functools.reduce(jnp.maximum, cols)
            m_old = m_scr[rows, :]
            m_new = jnp.maximum(m_old, jnp.max(mx, axis=-1, keepdims=True))
            al_scr[rows, :] = jnp.exp2(m_old - m_new)
            m_scr[rows, :] = m_new
            for cb in range(n_col):
                p_scr[rows, cb * LANES:(cb + 1) * LANES] = jnp.exp2(cols[cb] - m_new).astype(BF16)
        pv = _dot(p_scr[...], v_ref[kv_rows(j), :])
        al = al_scr[...]
        acc_scr[:, :V_HEAD] = acc_scr[:, :V_HEAD] * al + pv[:, :V_HEAD]
        acc_scr[:, V_HEAD:] = acc_scr[:, V_HEAD:] * al + pv[:, V_HEAD:]

    def step(j, cur, nxt):
        scores(j + 1, nxt)
        consume(j, cur, False)

    scores(0, s_a)

    def pair(pi, carry):
        step(2 * pi, s_a, s_b)
        step(2 * pi + 1, s_b, s_a)
        return carry

    lax.fori_loop(0, qi // 2, pair, 0)

    @pl.when(qi % 2 == 1)
    def _():
        step(qi - 1, s_a, s_b)
        consume(qi, s_b, True)

    @pl.when(qi % 2 == 0)
    def _():
        consume(qi, s_a, True)

    o_ref[...] = (acc_scr[:, :V_HEAD] / acc_scr[:, V_HEAD:]).astype(o_ref.dtype)


def _side_rows(w, n_steps):
    e, r, c = w.shape
    for rb in range(BF16_SUBLANES, r + 1, BF16_SUBLANES):
        if r % rb == 0 and e * (r // rb) <= n_steps:
            return rb if rb * c * 4 <= SIDE_BLOCK_BYTES else None
    return None


def _flash(q, k, v, batch, side=(), t=FLASH_TILE, slab=64):
    m = q.shape[0]
    s = m // batch
    nq = s // t
    n_steps = batch * MLA_HEADS * nq
    side_specs, side_blocks = [], []
    for w in side:
        rb = _side_rows(w, n_steps)
        per = w.shape[1] // rb
        n_blk = w.shape[0] * per
        side_blocks.append(n_blk)

        def idx(b, h, i, per=per, n_blk=n_blk):
            blk = jnp.minimum((b * MLA_HEADS + h) * nq + i, n_blk - 1)
            return (blk // per, blk % per, 0)

        side_specs.append(pl.BlockSpec((1, rb, w.shape[2]), idx))
    outs = pl.pallas_call(
        functools.partial(_flash_kernel, t=t, slab=slab, side_blocks=tuple(side_blocks)),
        grid=(batch, MLA_HEADS, nq),
        in_specs=[pl.BlockSpec((t, QK_PAD), lambda b, h, i: (b * nq + i, h)),
                  pl.BlockSpec((s, QK_PAD), lambda b, h, i: (b, h)),
                  pl.BlockSpec((s, 2 * V_HEAD), lambda b, h, i: (b, h))] + side_specs,
        out_specs=[pl.BlockSpec((t, V_HEAD), lambda b, h, i: (b * nq + i, h))] + side_specs,
        out_shape=[jax.ShapeDtypeStruct((m, MLA_HEADS * V_HEAD), BF16)]
        + [jax.ShapeDtypeStruct(w.shape, BF16) for w in side],
        scratch_shapes=[pltpu.VMEM((t, t), F32), pltpu.VMEM((t, t), F32),
                        pltpu.VMEM((t, t), BF16), pltpu.VMEM((t, LANES), F32),
                        pltpu.VMEM((t, LANES), F32), pltpu.VMEM((t, 2 * V_HEAD), F32)],
        compiler_params=_cparams(("arbitrary", "arbitrary", "arbitrary")),
        name="mla_flash",
    )(q, k, v, *side)
    return outs[0], outs[1:]


def _split3(x):
    hi = x.astype(BF16)
    lo = (x - hi.astype(F32)).astype(BF16)
    return hi, lo


def _moe_pre_kernel(h_ref, g_ref, sc_ref, sh_ref, wr_ref, xs_ref, meta_ref, wts_ref, cnt_ref,
                    run_scr):
    tm, d = h_ref.shape

    @pl.when(pl.program_id(0) == 0)
    def _():
        run_scr[...] = jnp.zeros_like(run_scr)

    u = _rms(h_ref[...]) * g_ref[...] * (1.0 + sc_ref[0]) + sh_ref[0]
    half = d // 2
    lo = pltpu.bitcast(u[:, :half].astype(BF16).astype(F32), jnp.uint32) >> 16
    hi = pltpu.bitcast(u[:, half:].astype(BF16).astype(F32), jnp.uint32)
    xs_ref[...] = hi | lo

    u_hi, u_lo = _split3(u)
    w_hi, w_lo = _split3(wr_ref[...])
    logits = _dot(u_hi, w_hi) + (_dot(u_hi, w_lo) + _dot(u_lo, w_hi))
    lane = lax.broadcasted_iota(jnp.int32, logits.shape, 1)
    lg = jnp.where(lane < N_EXPERTS, logits, -jnp.inf)
    m1 = jnp.max(lg, axis=-1, keepdims=True)
    i1 = jnp.min(jnp.where(lg == m1, lane, LANES), axis=-1, keepdims=True)
    lg2 = jnp.where(lane == i1, -jnp.inf, lg)
    m2 = jnp.max(lg2, axis=-1, keepdims=True)
    i2 = jnp.min(jnp.where(lg2 == m2, lane, LANES), axis=-1, keepdims=True)
    e = jnp.exp(m2 - m1)
    w1 = 1.0 / (1.0 + e)
    w2 = e * w1

    oh1 = lane == i1
    oh2 = lane == i2
    cnt = oh1.astype(F32) + oh2.astype(F32)
    r = lax.broadcasted_iota(jnp.int32, (tm, tm), 0)
    c = lax.broadcasted_iota(jnp.int32, (tm, tm), 1)
    strict = (r > c).astype(BF16)
    prefix = _dot(strict, cnt.astype(BF16)) + run_scr[...]
    rank1 = jnp.sum(jnp.where(oh1, prefix, 0.0), axis=-1, keepdims=True).astype(jnp.int32)
    rank2 = jnp.sum(jnp.where(oh2, prefix, 0.0), axis=-1, keepdims=True).astype(jnp.int32)
    run_scr[...] += jnp.sum(cnt, axis=0, keepdims=True)

    meta_ref[...] = jnp.where(lane == 0, i1, jnp.where(lane == 1, i2,
                              jnp.where(lane == 2, rank1, jnp.where(lane == 3, rank2, 0))))
    wts_ref[...] = jnp.where(lane == 0, w1, jnp.where(lane == 1, w2, 0.0))
    cnt_ref[...] = run_scr[...]


def _moe_pre(h, g, scale, shift, w_router_pad, rows_per_batch, tm=512):
    m, d = h.shape
    bpb = rows_per_batch // tm
    row = lambda w: pl.BlockSpec((tm, w), lambda i: (i, 0))
    vec = lambda w: pl.BlockSpec((1, w), lambda i: (0, 0))
    mod = pl.BlockSpec((1, 1, d), lambda i: (i // bpb, 0, 0))
    return pl.pallas_call(
        _moe_pre_kernel,
        grid=(m // tm,),
        in_specs=[row(d), vec(d), mod, mod, pl.BlockSpec((d, LANES), lambda i: (0, 0))],
        out_specs=[row(d // 2), row(LANES), row(LANES), vec(LANES)],
        out_shape=[jax.ShapeDtypeStruct((m, d // 2), jnp.uint32),
                   jax.ShapeDtypeStruct((m, LANES), jnp.int32),
                   jax.ShapeDtypeStruct((m, LANES), F32),
                   jax.ShapeDtypeStruct((1, LANES), F32)],
        scratch_shapes=[pltpu.VMEM((1, LANES), F32)],
        compiler_params=_cparams(("arbitrary",)),
        name="moe_route",
    )(h, g, scale, shift, w_router_pad)


def _moe_dispatch_kernel(dest_ref, x_ref, xs_in_ref, xs_ref, sem):
    del xs_in_ref
    tm = x_ref.shape[0]

    def row_copy(r, k):
        return pltpu.make_async_copy(x_ref.at[pl.ds(r, 1), :],
                                     xs_ref.at[pl.ds(dest_ref[0, 0, 2 * r + k], 1), :], sem)

    def issue(r, carry):
        row_copy(r, 0).start()
        row_copy(r, 1).start()
        return carry

    lax.fori_loop(0, tm, issue, 0)

    def drain(r, carry):
        row_copy(r, 0).wait()
        row_copy(r, 1).wait()
        return carry

    lax.fori_loop(0, tm, drain, 0)


def _moe_dispatch(x, dest, xs_init, tm=256):
    m, w = x.shape
    nb = m // tm
    return pl.pallas_call(
        _moe_dispatch_kernel,
        grid=(nb,),
        in_specs=[pl.BlockSpec((1, 1, 2 * tm), lambda i: (i, 0, 0), memory_space=pltpu.SMEM),
                  pl.BlockSpec((tm, w), lambda i: (i, 0)),
                  pl.BlockSpec(memory_space=pl.ANY)],
        out_specs=pl.BlockSpec(memory_space=pl.ANY),
        out_shape=jax.ShapeDtypeStruct(xs_init.shape, xs_init.dtype),
        scratch_shapes=[pltpu.SemaphoreType.DMA(())],
        input_output_aliases={2: 0},
        compiler_params=_cparams(("arbitrary",)),
        name="moe_dispatch",
    )(dest.reshape(nb, 1, 2 * tm), x, xs_init)


def _moe_ffn_kernel(blk_e_ref, n_used_ref, xs_ref, wg_ref, wu_ref, wd_ref, y_ref, u_scr):
    i = pl.program_id(0)
    j = pl.program_id(1)

    @pl.when(i < n_used_ref[0])
    def _():
        @pl.when(j == 0)
        def _():
            w = xs_ref[...]
            half = w.shape[1]
            u_scr[:, :half] = pltpu.bitcast(w << 16, F32).astype(BF16)
            u_scr[:, half:] = pltpu.bitcast(w & jnp.uint32(0xFFFF0000), F32).astype(BF16)
            y_ref[...] = jnp.zeros_like(y_ref)

        u = u_scr[...]
        gt = _dot(u, wg_ref[0])
        up = _dot(u, wu_ref[0])
        a = (gt * _sigmoid(gt) * up).astype(BF16)
        y_ref[...] += _dot(a, wd_ref[0])

    @pl.when((i >= n_used_ref[0]) & (j == 0))
    def _():
        y_ref[...] = jnp.zeros_like(y_ref)


def _moe_ffn(xs, blk_e, n_used, w_gu, w_down, tmb, tf=512):
    n_rows, half = xs.shape
    d = 2 * half
    ff = w_down.shape[1]
    nf = ff // tf
    nb = n_rows // tmb

    def blk(i, n_used_ref):
        return jnp.minimum(i, n_used_ref[0] - 1)

    def fidx(i, j, n_used_ref):
        return jnp.where(i < n_used_ref[0], j, nf - 1)

    grid_spec = pltpu.PrefetchScalarGridSpec(
        num_scalar_prefetch=2,
        grid=(nb, nf),
        in_specs=[
            pl.BlockSpec((tmb, half), lambda i, j, be, nu: (blk(i, nu), 0)),
            pl.BlockSpec((1, d, tf), lambda i, j, be, nu: (be[blk(i, nu)], 0, fidx(i, j, nu))),
            pl.BlockSpec((1, d, tf), lambda i, j, be, nu: (be[blk(i, nu)], 0, nf + fidx(i, j, nu))),
            pl.BlockSpec((1, tf, d), lambda i, j, be, nu: (be[blk(i, nu)], fidx(i, j, nu), 0)),
        ],
        out_specs=pl.BlockSpec((tmb, d), lambda i, j, be, nu: (i, 0)),
        scratch_shapes=[pltpu.VMEM((tmb, d), BF16)],
    )
    return pl.pallas_call(
        _moe_ffn_kernel,
        grid_spec=grid_spec,
        out_shape=jax.ShapeDtypeStruct((n_rows, d), F32),
        compiler_params=_cparams(("arbitrary", "arbitrary")),
        name="moe_grouped_ffn",
    )(blk_e, n_used, xs, w_gu, w_gu, w_down)


def _moe_combine_kernel(dest_ref, ys_ref, wts_ref, h_ref, gate_ref, g2_ref, out_ref,
                        buf0, buf1, sem):
    tm = h_ref.shape[0]
    bufs = (buf0, buf1)

    def row_copy(r, k):
        return pltpu.make_async_copy(ys_ref.at[pl.ds(dest_ref[0, 0, 2 * r + k], 1), :],
                                     bufs[k].at[pl.ds(r, 1), :], sem)

    def issue(r, carry):
        row_copy(r, 0).start()
        row_copy(r, 1).start()
        return carry

    lax.fori_loop(0, tm, issue, 0)

    def drain(r, carry):
        row_copy(r, 0).wait()
        row_copy(r, 1).wait()
        return carry

    lax.fori_loop(0, tm, drain, 0)

    wts = wts_ref[...]
    y = wts[:, 0:1] * buf0[...] + wts[:, 1:2] * buf1[...]
    out_ref[...] = h_ref[...] + gate_ref[0] * (_rms(y) * g2_ref[...])


def _moe_combine(ys, dest, wts, h, gate, g2, rows_per_batch, tm=256):
    m, d = h.shape
    nb = m // tm
    bpb = rows_per_batch // tm
    row = lambda w: pl.BlockSpec((tm, w), lambda i: (i, 0))
    return pl.pallas_call(
        _moe_combine_kernel,
        grid=(nb,),
        in_specs=[pl.BlockSpec((1, 1, 2 * tm), lambda i: (i, 0, 0), memory_space=pltpu.SMEM),
                  pl.BlockSpec(memory_space=pl.ANY),
                  row(LANES), row(d),
                  pl.BlockSpec((1, 1, d), lambda i: (i // bpb, 0, 0)),
                  pl.BlockSpec((1, d), lambda i: (0, 0))],
        out_specs=row(d),
        out_shape=jax.ShapeDtypeStruct((m, d), F32),
        scratch_shapes=[pltpu.VMEM((tm, d), F32), pltpu.VMEM((tm, d), F32),
                        pltpu.SemaphoreType.DMA(())],
        compiler_params=_cparams(("arbitrary",)),
        name="moe_combine",
    )(dest.reshape(nb, 1, 2 * tm), ys, wts, h, gate, g2)


def _moe(h, g, scale, shift, w_router, w_gu, w_down, gate, g2, rows_per_batch, tmb=512):
    m, d = h.shape
    w_router_pad = jnp.pad(w_router, ((0, 0), (0, LANES - N_EXPERTS)))
    x_packed, meta, wts, cnt = _moe_pre(h, g, scale, shift, w_router_pad, rows_per_batch)

    counts = cnt[0, :N_EXPERTS].astype(jnp.int32)
    padded = (counts + tmb - 1) // tmb * tmb
    pend = jnp.cumsum(padded)
    pstart = pend - padded
    dest = (pstart[meta[:, 0:2]] + meta[:, 2:4]).astype(jnp.int32).reshape(-1)
    n_rows = (2 * m + N_EXPERTS * (tmb - 1)) // tmb * tmb
    nb = n_rows // tmb
    blk_e = jnp.minimum(jnp.searchsorted(pend, jnp.arange(nb, dtype=jnp.int32) * tmb, side='right'),
                        N_EXPERTS - 1).astype(jnp.int32)
    n_used = (pend[-1:] // tmb).astype(jnp.int32)

    xs = _moe_dispatch(x_packed, dest, jnp.zeros((n_rows, d // 2), jnp.uint32))
    ys = _moe_ffn(xs, blk_e, n_used, w_gu, w_down, tmb)
    return _moe_combine(ys, dest, wts, h, gate, g2, rows_per_batch)


def _swap_halves(w):
    half = w.shape[-1] // 2
    return jnp.concatenate([w[..., half:], w[..., :half]], axis=-1)


def _mla_weights(w_q_a, q_norm_g, w_q_b, w_kv_a, kv_norm_g, w_kv_b):
    ql = w_q_b.shape[0]
    kvl = w_kv_b.shape[0]
    wq = w_q_b.reshape(ql, MLA_HEADS, QK_NOPE + QK_ROPE)
    rope = wq[:, :, QK_NOPE:]
    pad_r = LANES - QK_ROPE
    wqm = jnp.concatenate([wq, jnp.zeros((ql, MLA_HEADS, QK_PAD - QK_NOPE - QK_ROPE), F32)], axis=-1)
    wqr = jnp.concatenate([_swap_halves(rope), jnp.zeros((ql, MLA_HEADS, pad_r), F32)], axis=-1)
    wkv = w_kv_b.reshape(kvl, MLA_HEADS, QK_NOPE + V_HEAD)
    kr = w_kv_a[:, kvl:]
    dm = w_kv_a.shape[0]
    wkr = jnp.concatenate([kr, jnp.zeros((dm, pad_r), F32)], axis=-1)
    wkrr = jnp.concatenate([_swap_halves(kr), jnp.zeros((dm, pad_r), F32)], axis=-1)
    bf = lambda a: a.astype(BF16)
    return (bf(w_q_a), q_norm_g.reshape(1, -1), bf(wqm.reshape(ql, -1)), bf(wqr.reshape(ql, -1)),
            bf(w_kv_a[:, :kvl]), bf(wkr), bf(wkrr), kv_norm_g.reshape(1, -1),
            bf(wkv[:, :, :QK_NOPE].reshape(kvl, -1)), bf(wkv[:, :, QK_NOPE:].reshape(kvl, -1)))


def kernel(x, c, positions, ada_w, ada_b, norm_g, hg_w_in, hg_lb_logits, hg_out_norm_g, hg_w_out, kv_src_norm_g, kv_src_ada_w, kv_src_ada_b, mla_w_kv_a, mla_kv_norm_g, mla_w_kv_b, mla_w_q_a, mla_q_norm_g, mla_w_q_b, mla_w_o, ffn_w_gu, ffn_w_down, moe_w_router, moe_w_gu, moe_w_down):
    batch, seq, d = x.shape
    m = batch * seq
    bf = lambda a: a.astype(BF16)

    c8 = jnp.pad(c, ((0, 8 - batch), (0, 0)))
    ada = _modulation(c8, ada_w.reshape(4, d, 3 * d), ada_b.reshape(4, 3 * d))[:, :batch]
    kvm = _modulation(c8, kv_src_ada_w[None], kv_src_ada_b[None])[0, :batch]

    def mods(idx):
        a = ada[idx]
        return [a[:, i * d:(i + 1) * d].reshape(batch, 1, d) for i in range(3)]

    vec = lambda a: a.reshape(1, -1)
    lb_all = jnp.cumsum(jax.nn.softmax(hg_lb_logits.astype(F32), axis=0), axis=0)

    h = x.reshape(m, d)

    shift, scale, gate = mods(0)
    q, f_logit, iv, gg = _hg_proj(h, vec(norm_g[0, 0, 0]), scale, shift, bf(hg_w_in[0]), seq)
    o = _hgrn2(q, f_logit, iv, vec(lb_all[0]), batch)
    h = _oproj(o, bf(hg_w_out[0]), h, gate, vec(norm_g[0, 0, 1]), seq,
               gg=gg, gout=vec(hg_out_norm_g[0]))

    shift, scale, gate = mods(1)
    h = _ffn(h, vec(norm_g[0, 1, 0]), scale, shift, bf(ffn_w_gu[0]), bf(ffn_w_down[0]),
             gate, vec(norm_g[0, 1, 1]), seq)

    shift_k, scale_k = [kvm[:, i * d:(i + 1) * d].reshape(batch, 1, d) for i in range(2)]
    shift, scale, gate = mods(2)
    cos, sin = _rope_tables(positions.reshape(m, 1).astype(F32))
    wts = _mla_weights(mla_w_q_a[0], mla_q_norm_g[0], mla_w_q_b[0],
                       mla_w_kv_a, mla_kv_norm_g, mla_w_kv_b)
    qh, kh, vh = _mla_proj(h, vec(norm_g[1, 0, 0]), scale, shift,
                           vec(kv_src_norm_g), scale_k, shift_k, wts, cos, sin, seq)
    n_steps = batch * MLA_HEADS * (seq // FLASH_TILE)
    moe_w = (moe_w_gu[0], moe_w_down[0])
    if all(_side_rows(w, n_steps) is not None for w in moe_w):
        att, (w_gu_bf, w_down_bf) = _flash(qh, kh, vh, batch, side=moe_w)
    else:
        att, _ = _flash(qh, kh, vh, batch)
        w_gu_bf, w_down_bf = bf(moe_w_gu[0]), bf(moe_w_down[0])
    h = _oproj(att, bf(mla_w_o[0]), h, gate, vec(norm_g[1, 0, 1]), seq)

    shift, scale, gate = mods(3)
    h = _moe(h, vec(norm_g[1, 1, 0]), scale, shift, moe_w_router[0],
             w_gu_bf, w_down_bf, gate, vec(norm_g[1, 1, 1]), seq)
    return h.reshape(batch, seq, d)
```

```python
import functools

import jax
import jax.numpy as jnp
from jax import lax
from jax.experimental import pallas as pl
from jax.experimental.pallas import tpu as pltpu

F32 = jnp.float32
BF16 = jnp.bfloat16

EPS = 1e-6
LANES = 128
BF16_SUBLANES = 16
SIDE_BLOCK_BYTES = 3 * 1024 * 1024

HG_HEADS = 16
HG_KDIM = 128
HG_CHUNK = 64
HG_SUB = 16
HG_EXP_CLAMP = 80.0

MLA_HEADS = 16
QK_NOPE = 128
QK_ROPE = 64
V_HEAD = 128
QK_PAD = 256
MLA_SCALE = (QK_NOPE + QK_ROPE) ** -0.5
Q_SCALE = MLA_SCALE * 1.4426950408889634
ROPE_THETA = 10000.0
FLASH_TQ = 2048
FLASH_TK = 1024

N_EXPERTS = 8
FFN_TF = 512

VMEM_LIMIT = 56 * 1024 * 1024


def _cparams(sem):
    return pltpu.CompilerParams(dimension_semantics=sem, vmem_limit_bytes=VMEM_LIMIT)


def _rms(x):
    return x * lax.rsqrt(jnp.mean(x * x, axis=-1, keepdims=True) + EPS)


def _sigmoid(x):
    return 1.0 / (1.0 + jnp.exp(-x))


def _dot(a, b):
    return jnp.dot(a, b, preferred_element_type=F32)


def _dot_nt(a, b):
    return lax.dot_general(a, b, (((1,), (1,)), ((), ())), preferred_element_type=F32)


def _dot_tn(a, b):
    return lax.dot_general(a, b, (((0,), (0,)), ((), ())), preferred_element_type=F32)


def _side_rows(w, n_steps):
    e, r, c = w.shape
    for rb in range(BF16_SUBLANES, r + 1, BF16_SUBLANES):
        if r % rb == 0 and e * (r // rb) <= n_steps:
            return rb if rb * c * 4 <= SIDE_BLOCK_BYTES else None
    return None


def _side_ok(weights, n_steps):
    return all(_side_rows(w, n_steps) is not None for w in weights)


def _side_plan(weights, n_steps, step_of):
    specs, blocks = [], []
    for w in weights:
        rb = _side_rows(w, n_steps)
        per = w.shape[1] // rb
        n_blk = w.shape[0] * per
        blocks.append(n_blk)

        def idx(*grid_ids, per=per, n_blk=n_blk):
            blk = jnp.minimum(step_of(*grid_ids), n_blk - 1)
            return (blk // per, blk % per, 0)

        specs.append(pl.BlockSpec((1, rb, w.shape[2]), idx))
    return specs, tuple(blocks)


def _side_cast(step_id, w_ins, w_outs, blocks):
    for w_in, w_out, n_blk in zip(w_ins, w_outs, blocks):
        @pl.when(step_id < n_blk)
        def _(w_in=w_in, w_out=w_out):
            w_out[...] = w_in[...].astype(w_out.dtype)


def _with_side(run, weights, n_steps):
    w3 = tuple(w.reshape((1,) + w.shape) if w.ndim == 2 else w for w in weights)
    if _side_ok(w3, n_steps):
        res, copies = run(w3)
        return res, [cp.reshape(w.shape) for cp, w in zip(copies, weights)]
    res, _ = run(())
    return res, [w.astype(BF16) for w in weights]


def _mod_kernel(c_ref, w_ref, b_ref, o_ref):
    c = c_ref[...]
    sc = (c * _sigmoid(c)).astype(BF16)
    o_ref[0] = _dot(sc, w_ref[0].astype(BF16)) + b_ref[0]


def _modulation(c8, w, b, tn=1024):
    g, d, n = w.shape
    return pl.pallas_call(
        _mod_kernel,
        grid=(g, n // tn),
        in_specs=[pl.BlockSpec((8, d), lambda i, j: (0, 0)),
                  pl.BlockSpec((1, d, tn), lambda i, j: (i, 0, j)),
                  pl.BlockSpec((1, 1, tn), lambda i, j: (i, 0, j))],
        out_specs=pl.BlockSpec((1, 8, tn), lambda i, j: (i, 0, j)),
        out_shape=jax.ShapeDtypeStruct((g, 8, n), F32),
        compiler_params=_cparams(("arbitrary", "arbitrary")),
        name="modulation",
    )(c8, w, b.reshape(g, 1, n))


def _rope_kernel(pos_ref, inv_ref, cos_ref, sin_ref):
    ang = pos_ref[...] * inv_ref[...]
    lane = lax.broadcasted_iota(jnp.int32, ang.shape, 1)
    half = QK_ROPE // 2
    cos_ref[...] = jnp.where(lane < QK_ROPE, jnp.cos(ang), 0.0)
    sin_ref[...] = jnp.where(lane < half, -jnp.sin(ang),
                             jnp.where(lane < QK_ROPE, jnp.sin(ang), 0.0))


def _rope_tables(pos, tm=512):
    m = pos.shape[0]
    half = QK_ROPE // 2
    inv = 1.0 / (ROPE_THETA ** (jnp.arange(0, QK_ROPE, 2, dtype=F32) / QK_ROPE))
    inv128 = jnp.concatenate([inv, inv, jnp.zeros((LANES - 2 * half,), F32)]).reshape(1, LANES)
    return pl.pallas_call(
        _rope_kernel,
        grid=(m // tm,),
        in_specs=[pl.BlockSpec((tm, 1), lambda i: (i, 0)),
                  pl.BlockSpec((1, LANES), lambda i: (0, 0))],
        out_specs=[pl.BlockSpec((tm, LANES), lambda i: (i, 0))] * 2,
        out_shape=[jax.ShapeDtypeStruct((m, LANES), F32)] * 2,
        compiler_params=_cparams(("arbitrary",)),
        name="rope_tables",
    )(pos, inv128)


def _hg_proj_kernel(h_ref, g_ref, sc_ref, sh_ref, w_ref, q_ref, f_ref, i_ref, gg_ref, u_scr):
    j = pl.program_id(1)

    @pl.when(j == 0)
    def _():
        u = _rms(h_ref[...]) * g_ref[...] * (1.0 + sc_ref[0]) + sh_ref[0]
        u_scr[...] = u.astype(BF16)

    r = _dot(u_scr[...], w_ref[...])
    for idx, ref in enumerate((q_ref, f_ref, i_ref, gg_ref)):
        @pl.when(j == idx)
        def _(ref=ref):
            ref[...] = r.astype(ref.dtype)


def _hg_proj(h, g, scale, shift, w_in, rows_per_batch, tm=512):
    m, d = h.shape
    bpb = rows_per_batch // tm
    row = lambda i, j: (i, 0)
    mod = lambda i, j: (i // bpb, 0, 0)
    return pl.pallas_call(
        _hg_proj_kernel,
        grid=(m // tm, 4),
        in_specs=[pl.BlockSpec((tm, d), row),
                  pl.BlockSpec((1, d), lambda i, j: (0, 0)),
                  pl.BlockSpec((1, 1, d), mod),
                  pl.BlockSpec((1, 1, d), mod),
                  pl.BlockSpec((d, d), lambda i, j: (0, j))],
        out_specs=[pl.BlockSpec((tm, d), row)] * 4,
        out_shape=[jax.ShapeDtypeStruct((m, d), BF16), jax.ShapeDtypeStruct((m, d), F32),
                   jax.ShapeDtypeStruct((m, d), BF16), jax.ShapeDtypeStruct((m, d), BF16)],
        scratch_shapes=[pltpu.VMEM((tm, d), BF16)],
        compiler_params=_cparams(("arbitrary", "arbitrary")),
        name="hg_proj",
    )(h, g, scale, shift, w_in)


def _hgrn2_kernel(*refs, n_chunks, side_blocks):
    n_side = len(side_blocks)
    q_ref, f_ref, v_ref, lb_ref = refs[:4]
    o_ref = refs[4 + n_side]
    st_ref = refs[5 + 2 * n_side]
    step_id = ((pl.program_id(0) * pl.num_programs(1) + pl.program_id(1)) * pl.num_programs(2)
               + pl.program_id(2))
    _side_cast(step_id, refs[4:4 + n_side], refs[5 + n_side:5 + 2 * n_side], side_blocks)
    c, sub = HG_CHUNK, HG_SUB
    n_sub = c // sub

    @pl.when(pl.program_id(2) == 0)
    def _():
        st_ref[...] = jnp.zeros_like(st_ref)

    lb = lb_ref[...]
    row = lax.broadcasted_iota(jnp.int32, (c, c), 0)
    col = lax.broadcasted_iota(jnp.int32, (c, c), 1)
    tri = (row >= col).astype(BF16)
    row2 = lax.broadcasted_iota(jnp.int32, (c, 2 * c), 0)
    col2 = lax.broadcasted_iota(jnp.int32, (c, 2 * c), 1)
    c_bits, sub_bits = c.bit_length() - 1, sub.bit_length() - 1
    mask2 = ((col2 >> c_bits) == ((row2 >> sub_bits) & 1)) & ((col2 & (c - 1)) <= row2)

    chunks = [slice(ci * c, (ci + 1) * c) for ci in range(n_chunks)]
    ks, bs = [], []
    for sl in chunks:
        f = lb + (1.0 - lb) * _sigmoid(f_ref[sl, :])
        lf = jnp.log(f)
        lf_hi = lf.astype(BF16)
        lf_lo = (lf - lf_hi.astype(F32)).astype(BF16)
        ks.append(1.0 - f)
        bs.append(_dot(tri, lf_hi) + _dot(tri, lf_lo))
    a_fulls, qbs, kdecs, decays = [], [], [], []
    for sl, k, b in zip(chunks, ks, bs):
        q = q_ref[sl, :].astype(F32)
        b_last = b[c - 1:c, :]
        refs = [jnp.zeros_like(b_last)] + [b[j * sub - 1:j * sub, :] for j in range(1, n_sub)]
        rrow = jnp.concatenate([jnp.broadcast_to(r, (sub, r.shape[1])) for r in refs], axis=0)
        qe = q * jnp.exp(b - rrow)
        qbs.append((qe * jnp.exp(rrow)).astype(BF16))
        ke = jnp.concatenate(
            [k * jnp.exp(jnp.minimum(r - b, HG_EXP_CLAMP)) for r in refs], axis=0)
        a_fulls.append(_dot_nt(qe.astype(BF16), ke.astype(BF16)))
        kdecs.append((k * jnp.exp(b_last - b)).astype(BF16))
        decays.append(jnp.exp(b_last))
    o_intras, incs = [], []
    for sl, a_full, kdec in zip(chunks, a_fulls, kdecs):
        v = v_ref[sl, :]
        a2 = jnp.concatenate(
            [a_full[j * sub:(j + 1) * sub, (j // 2) * 2 * c:(j // 2 + 1) * 2 * c]
             for j in range(n_sub)], axis=0)
        a2 = jnp.where(mask2, a2, 0.0).astype(BF16)
        o_intras.append(_dot(a2, jnp.concatenate([v, v], axis=0)))
        incs.append(_dot_tn(v, kdec))
    st = st_ref[...]
    for sl, qb, o_intra, inc, decay in zip(chunks, qbs, o_intras, incs, decays):
        o_ref[sl, :] = o_intra + _dot_nt(qb, st.astype(BF16))
        st = st * decay + inc
    st_ref[...] = st


def _hgrn2_steps(m, tb=512):
    return HG_HEADS * (m // tb)


def _hgrn2(q, f_logit, v, lb, batch, side=(), tb=512):
    m, d = q.shape
    s = m // batch
    nt = s // tb
    blk = lambda b, h, t: (b * nt + t, h)
    side_specs, side_blocks = _side_plan(
        side, batch * HG_HEADS * nt, lambda b, h, t: (b * HG_HEADS + h) * nt + t)
    outs = pl.pallas_call(
        functools.partial(_hgrn2_kernel, n_chunks=tb // HG_CHUNK, side_blocks=side_blocks),
        grid=(batch, HG_HEADS, nt),
        in_specs=[pl.BlockSpec((tb, HG_KDIM), blk)] * 3
        + [pl.BlockSpec((1, HG_KDIM), lambda b, h, t: (0, h))] + side_specs,
        out_specs=[pl.BlockSpec((tb, HG_KDIM), blk)] + side_specs,
        out_shape=[jax.ShapeDtypeStruct((m, d), F32)]
        + [jax.ShapeDtypeStruct(w.shape, BF16) for w in side],
        scratch_shapes=[pltpu.VMEM((HG_KDIM, HG_KDIM), F32)],
        compiler_params=_cparams(("arbitrary", "arbitrary", "arbitrary")),
        name="hgrn2_recurrence",
    )(q, f_logit, v, lb, *side)
    return outs[0], list(outs[1:])


def _oproj_kernel(*refs, gated):
    if gated:
        o_ref, gg_ref, gout_ref, w_ref, h_ref, gate_ref, g2_ref, out_ref = refs
        gg = gg_ref[...].astype(F32)
        x = _rms(o_ref[...]) * gout_ref[...] * (gg * _sigmoid(gg))
    else:
        o_ref, w_ref, h_ref, gate_ref, g2_ref, out_ref = refs
        x = o_ref[...]
    y = _dot(x.astype(BF16), w_ref[...])
    out_ref[...] = h_ref[...] + gate_ref[0] * (_rms(y) * g2_ref[...])


def _oproj(o, w, h, gate, g2, rows_per_batch, gg=None, gout=None, tm=512):
    m, d = h.shape
    bpb = rows_per_batch // tm
    row = pl.BlockSpec((tm, d), lambda i: (i, 0))
    vec = pl.BlockSpec((1, d), lambda i: (0, 0))
    mod = pl.BlockSpec((1, 1, d), lambda i: (i // bpb, 0, 0))
    wsp = pl.BlockSpec(w.shape, lambda i: (0, 0))
    gated = gg is not None
    if gated:
        args, specs = (o, gg, gout, w, h, gate, g2), [row, row, vec, wsp, row, mod, vec]
    else:
        args, specs = (o, w, h, gate, g2), [row, wsp, row, mod, vec]
    return pl.pallas_call(
        functools.partial(_oproj_kernel, gated=gated),
        grid=(m // tm,),
        in_specs=specs,
        out_specs=row,
        out_shape=jax.ShapeDtypeStruct((m, d), F32),
        compiler_params=_cparams(("arbitrary",)),
        name="hg_out_proj" if gated else "mla_out_proj",
    )(*args)


def _ffn_kernel(*refs, side_blocks):
    n_side = len(side_blocks)
    h_ref, g_ref, sc_ref, sh_ref, wg_ref, wu_ref, wd_ref, gate_ref, g2_ref = refs[:9]
    out_ref = refs[9 + n_side]
    u_scr, acc_scr = refs[10 + 2 * n_side:]
    j = pl.program_id(1)
    _side_cast(pl.program_id(0) * pl.num_programs(1) + j,
               refs[9:9 + n_side], refs[10 + n_side:10 + 2 * n_side], side_blocks)

    @pl.when(j == 0)
    def _():
        u = _rms(h_ref[...]) * g_ref[...] * (1.0 + sc_ref[0]) + sh_ref[0]
        u_scr[...] = u.astype(BF16)
        acc_scr[...] = jnp.zeros_like(acc_scr)

    u = u_scr[...]
    gt = _dot(u, wg_ref[...])
    up = _dot(u, wu_ref[...])
    a = (gt * _sigmoid(gt) * up).astype(BF16)
    acc_scr[...] += _dot(a, wd_ref[...])

    @pl.when(j == pl.num_programs(1) - 1)
    def _():
        out_ref[...] = h_ref[...] + gate_ref[0] * (_rms(acc_scr[...]) * g2_ref[...])


def _ffn_steps(m, ff, tm=512, tf=FFN_TF):
    return (m // tm) * (ff // tf)


def _ffn(h, g, scale, shift, w_gu, w_down, gate, g2, rows_per_batch, side=(), tm=512, tf=FFN_TF):
    m, d = h.shape
    ff = w_down.shape[0]
    nf = ff // tf
    bpb = rows_per_batch // tm
    row = pl.BlockSpec((tm, d), lambda i, j: (i, 0))
    vec = pl.BlockSpec((1, d), lambda i, j: (0, 0))
    mod = pl.BlockSpec((1, 1, d), lambda i, j: (i // bpb, 0, 0))
    side_specs, side_blocks = _side_plan(side, (m // tm) * nf, lambda i, j: i * nf + j)
    outs = pl.pallas_call(
        functools.partial(_ffn_kernel, side_blocks=side_blocks),
        grid=(m // tm, nf),
        in_specs=[row, vec, mod, mod,
                  pl.BlockSpec((d, tf), lambda i, j: (0, j)),
                  pl.BlockSpec((d, tf), lambda i, j: (0, nf + j)),
                  pl.BlockSpec((tf, d), lambda i, j: (j, 0)),
                  mod, vec] + side_specs,
        out_specs=[row] + side_specs,
        out_shape=[jax.ShapeDtypeStruct((m, d), F32)]
        + [jax.ShapeDtypeStruct(w.shape, BF16) for w in side],
        scratch_shapes=[pltpu.VMEM((tm, d), BF16), pltpu.VMEM((tm, d), F32)],
        compiler_params=_cparams(("arbitrary", "arbitrary")),
        name="dense_ffn",
    )(h, g, scale, shift, w_gu, w_gu, w_down, gate, g2, *side)
    return outs[0], list(outs[1:])


def _mla_proj_kernel(h_ref, gq_ref, scq_ref, shq_ref, gk_ref, sck_ref, shk_ref,
                     wqa_ref, qg_ref, wqm_ref, wqr_ref,
                     wkva_ref, wkr_ref, wkrr_ref, kg_ref, wk_ref, wv_ref,
                     cos_ref, sin_ref, q_ref, k_ref, v_ref):
    xhat = _rms(h_ref[...])
    uq = (xhat * gq_ref[...] * (1.0 + scq_ref[0]) + shq_ref[0]).astype(BF16)
    xk = (xhat * gk_ref[...] * (1.0 + sck_ref[0]) + shk_ref[0]).astype(BF16)
    cos = cos_ref[...]
    sin = sin_ref[...]
    qn = (_rms(_dot(uq, wqa_ref[...])) * qg_ref[...]).astype(BF16)
    cn = (_rms(_dot(xk, wkva_ref[...])) * kg_ref[...]).astype(BF16)
    krope = (_dot(xk, wkr_ref[...]) * cos + _dot(xk, wkrr_ref[...]) * sin).astype(BF16)
    ones = jnp.ones((cn.shape[0], V_HEAD), BF16)
    for hp in range(MLA_HEADS // 2):
        ps = slice(hp * 2 * LANES, (hp + 1) * 2 * LANES)
        qr2 = _dot(qn, wqr_ref[:, ps]) * Q_SCALE
        kn2 = _dot(cn, wk_ref[:, ps]).astype(BF16)
        v2 = _dot(cn, wv_ref[:, ps]).astype(BF16)
        for sub in range(2):
            hd = 2 * hp + sub
            ls = slice(sub * LANES, (sub + 1) * LANES)
            qm = _dot(qn, wqm_ref[:, hd * QK_PAD:(hd + 1) * QK_PAD]) * Q_SCALE
            q_ref[:, hd * QK_PAD:hd * QK_PAD + QK_NOPE] = qm[:, :QK_NOPE].astype(BF16)
            q_ref[:, hd * QK_PAD + QK_NOPE:(hd + 1) * QK_PAD] = (
                qm[:, QK_NOPE:] * cos + qr2[:, ls] * sin).astype(BF16)
            k_ref[:, hd * QK_PAD:hd * QK_PAD + QK_NOPE] = kn2[:, ls]
            k_ref[:, hd * QK_PAD + QK_NOPE:(hd + 1) * QK_PAD] = krope
            v_ref[:, 2 * hd * V_HEAD:(2 * hd + 1) * V_HEAD] = v2[:, ls]
            v_ref[:, (2 * hd + 1) * V_HEAD:(2 * hd + 2) * V_HEAD] = ones


def _mla_proj(h, gq, scq, shq, gk, sck, shk, wts, cos, sin, rows_per_batch, tm=256):
    m, d = h.shape
    bpb = rows_per_batch // tm
    row = lambda w: pl.BlockSpec((tm, w), lambda i: (i, 0))
    vec = lambda w: pl.BlockSpec((1, w), lambda i: (0, 0))
    mod = pl.BlockSpec((1, 1, d), lambda i: (i // bpb, 0, 0))
    full = lambda a: pl.BlockSpec(a.shape, lambda i: (0, 0))
    wqa, qg, wqm, wqr, wkva, wkr, wkrr, kg, wk, wv = wts
    return pl.pallas_call(
        _mla_proj_kernel,
        grid=(m // tm,),
        in_specs=[row(d), vec(d), mod, mod, vec(d), mod, mod,
                  full(wqa), full(qg), full(wqm), full(wqr),
                  full(wkva), full(wkr), full(wkrr), full(kg), full(wk), full(wv),
                  row(LANES), row(LANES)],
        out_specs=[row(MLA_HEADS * QK_PAD), row(MLA_HEADS * QK_PAD), row(MLA_HEADS * 2 * V_HEAD)],
        out_shape=[jax.ShapeDtypeStruct((m, MLA_HEADS * QK_PAD), BF16),
                   jax.ShapeDtypeStruct((m, MLA_HEADS * QK_PAD), BF16),
                   jax.ShapeDtypeStruct((m, MLA_HEADS * 2 * V_HEAD), BF16)],
        compiler_params=_cparams(("arbitrary",)),
        name="mla_proj",
    )(h, gq, scq, shq, gk, sck, shk, wqa, qg, wqm, wqr, wkva, wkr, wkrr, kg, wk, wv, cos, sin)


def _flash_kernel(*refs, tq, tk, slab, side_blocks):
    n_side = len(side_blocks)
    q_ref, k_ref, v_ref = refs[:3]
    o_ref = refs[3 + n_side]
    s_a, s_b, p_scr, m_scr, al_scr, acc_scr = refs[4 + 2 * n_side:]
    qi = pl.program_id(2)
    step_id = (pl.program_id(0) * pl.num_programs(1) + pl.program_id(1)) * pl.num_programs(2) + qi
    _side_cast(step_id, refs[3:3 + n_side], refs[4 + n_side:4 + 2 * n_side], side_blocks)

    s_bufs = (s_a, s_b)
    m_scr[...] = jnp.full_like(m_scr, -jnp.inf)
    acc_scr[...] = jnp.zeros_like(acc_scr)
    n_col = tk // LANES
    n_diag = tq // tk

    def kv_rows(j):
        return pl.ds(pl.multiple_of(j * tk, tk), tk)

    def scores(j, dst, row0):
        dst[row0:, :] = _dot_nt(q_ref[row0:, :], k_ref[kv_rows(j), :])

    def consume(j, src, row0, diag):
        for r0 in range(row0, tq, slab):
            rows = slice(r0, r0 + slab)
            cols = [src[rows, cb * LANES:(cb + 1) * LANES] for cb in range(n_col)]
            if diag and r0 < row0 + tk:
                rowp = (r0 - row0) + lax.broadcasted_iota(jnp.int32, (slab, LANES), 0)
                lane = lax.broadcasted_iota(jnp.int32, (slab, LANES), 1)
                cols = [jnp.where(cb * LANES + lane <= rowp, cols[cb], -jnp.inf)
                        for cb in range(n_col)]
            mx = functools.reduce(jnp.maximum, cols)
            m_old = m_scr[rows, :]
            m_new = jnp.maximum(m_old, jnp.max(mx, axis=-1, keepdims=True))
            al_scr[rows, :] = jnp.exp2(m_old - m_new)
            m_scr[rows, :] = m_new
            for cb in range(n_col):
                p_scr[rows, cb * LANES:(cb + 1) * LANES] = jnp.exp2(cols[cb] - m_new).astype(BF16)
        pv = _dot(p_scr[row0:, :], v_ref[kv_rows(j), :])
        al = al_scr[row0:, :]
        acc_scr[row0:, :V_HEAD] = acc_scr[row0:, :V_HEAD] * al + pv[:, :V_HEAD]
        acc_scr[row0:, V_HEAD:] = acc_scr[row0:, V_HEAD:] * al + pv[:, V_HEAD:]

    n_full = qi * n_diag
    scores(0, s_a, 0)

    def pair(pi, carry):
        for u in range(2):
            scores(2 * pi + u + 1, s_bufs[1 - u], 0)
            consume(2 * pi + u, s_bufs[u], 0, False)
        return carry

    lax.fori_loop(0, n_full // 2, pair, 0)
    for dg in range(n_diag):
        if dg + 1 < n_diag:
            scores(n_full + dg + 1, s_bufs[(dg + 1) % 2], (dg + 1) * tk)
        consume(n_full + dg, s_bufs[dg % 2], dg * tk, True)

    o_ref[...] = (acc_scr[:, :V_HEAD] / acc_scr[:, V_HEAD:]).astype(o_ref.dtype)


def _flash_tiles(seq):
    tq = min(FLASH_TQ, seq)
    return tq, min(FLASH_TK, tq // 2)


def _flash(q, k, v, batch, side=(), slab=64):
    m = q.shape[0]
    s = m // batch
    tq, tk = _flash_tiles(s)
    nq = s // tq
    side_specs, side_blocks = _side_plan(
        side, batch * MLA_HEADS * nq, lambda b, h, i: (b * MLA_HEADS + h) * nq + i)
    outs = pl.pallas_call(
        functools.partial(_flash_kernel, tq=tq, tk=tk, slab=slab, side_blocks=side_blocks),
        grid=(batch, MLA_HEADS, nq),
        in_specs=[pl.BlockSpec((tq, QK_PAD), lambda b, h, i: (b * nq + i, h)),
                  pl.BlockSpec((s, QK_PAD), lambda b, h, i: (b, h)),
                  pl.BlockSpec((s, 2 * V_HEAD), lambda b, h, i: (b, h))] + side_specs,
        out_specs=[pl.BlockSpec((tq, V_HEAD), lambda b, h, i: (b * nq + i, h))] + side_specs,
        out_shape=[jax.ShapeDtypeStruct((m, MLA_HEADS * V_HEAD), BF16)]
        + [jax.ShapeDtypeStruct(w.shape, BF16) for w in side],
        scratch_shapes=[pltpu.VMEM((tq, tk), F32), pltpu.VMEM((tq, tk), F32),
                        pltpu.VMEM((tq, tk), BF16), pltpu.VMEM((tq, LANES), F32),
                        pltpu.VMEM((tq, LANES), F32), pltpu.VMEM((tq, 2 * V_HEAD), F32)],
        compiler_params=_cparams(("arbitrary", "arbitrary", "arbitrary")),
        name="mla_flash",
    )(q, k, v, *side)
    return outs[0], list(outs[1:])


def _split3(x):
    hi = x.astype(BF16)
    lo = (x - hi.astype(F32)).astype(BF16)
    return hi, lo


def _moe_pre_kernel(h_ref, g_ref, sc_ref, sh_ref, wr_ref, xs_ref, meta_ref, wts_ref, cnt_ref,
                    run_scr):
    tm, d = h_ref.shape

    @pl.when(pl.program_id(0) == 0)
    def _():
        run_scr[...] = jnp.zeros_like(run_scr)

    u = _rms(h_ref[...]) * g_ref[...] * (1.0 + sc_ref[0]) + sh_ref[0]
    half = d // 2
    lo = pltpu.bitcast(u[:, :half].astype(BF16).astype(F32), jnp.uint32) >> 16
    hi = pltpu.bitcast(u[:, half:].astype(BF16).astype(F32), jnp.uint32)
    xs_ref[...] = hi | lo

    u_hi, u_lo = _split3(u)
    w_hi, w_lo = _split3(wr_ref[...])
    logits = _dot(u_hi, w_hi) + (_dot(u_hi, w_lo) + _dot(u_lo, w_hi))
    lane = lax.broadcasted_iota(jnp.int32, logits.shape, 1)
    lg = jnp.where(lane < N_EXPERTS, logits, -jnp.inf)
    m1 = jnp.max(lg, axis=-1, keepdims=True)
    i1 = jnp.min(jnp.where(lg == m1, lane, LANES), axis=-1, keepdims=True)
    lg2 = jnp.where(lane == i1, -jnp.inf, lg)
    m2 = jnp.max(lg2, axis=-1, keepdims=True)
    i2 = jnp.min(jnp.where(lg2 == m2, lane, LANES), axis=-1, keepdims=True)
    e = jnp.exp(m2 - m1)
    w1 = 1.0 / (1.0 + e)
    w2 = e * w1

    oh1 = lane == i1
    oh2 = lane == i2
    cnt = oh1.astype(F32) + oh2.astype(F32)
    r = lax.broadcasted_iota(jnp.int32, (tm, tm), 0)
    c = lax.broadcasted_iota(jnp.int32, (tm, tm), 1)
    strict = (r > c).astype(BF16)
    prefix = _dot(strict, cnt.astype(BF16)) + run_scr[...]
    rank1 = jnp.sum(jnp.where(oh1, prefix, 0.0), axis=-1, keepdims=True).astype(jnp.int32)
    rank2 = jnp.sum(jnp.where(oh2, prefix, 0.0), axis=-1, keepdims=True).astype(jnp.int32)
    run_scr[...] += jnp.sum(cnt, axis=0, keepdims=True)

    meta_ref[...] = jnp.where(lane == 0, i1, jnp.where(lane == 1, i2,
                              jnp.where(lane == 2, rank1, jnp.where(lane == 3, rank2, 0))))
    wts_ref[...] = jnp.where(lane == 0, w1, jnp.where(lane == 1, w2, 0.0))
    cnt_ref[...] = run_scr[...]


def _moe_pre(h, g, scale, shift, w_router_pad, rows_per_batch, tm=512):
    m, d = h.shape
    bpb = rows_per_batch // tm
    row = lambda w: pl.BlockSpec((tm, w), lambda i: (i, 0))
    vec = lambda w: pl.BlockSpec((1, w), lambda i: (0, 0))
    mod = pl.BlockSpec((1, 1, d), lambda i: (i // bpb, 0, 0))
    return pl.pallas_call(
        _moe_pre_kernel,
        grid=(m // tm,),
        in_specs=[row(d), vec(d), mod, mod, pl.BlockSpec((d, LANES), lambda i: (0, 0))],
        out_specs=[row(d // 2), row(LANES), row(LANES), vec(LANES)],
        out_shape=[jax.ShapeDtypeStruct((m, d // 2), jnp.uint32),
                   jax.ShapeDtypeStruct((m, LANES), jnp.int32),
                   jax.ShapeDtypeStruct((m, LANES), F32),
                   jax.ShapeDtypeStruct((1, LANES), F32)],
        scratch_shapes=[pltpu.VMEM((1, LANES), F32)],
        compiler_params=_cparams(("arbitrary",)),
        name="moe_route",
    )(h, g, scale, shift, w_router_pad)


def _moe_dispatch_kernel(dest_ref, x_ref, xs_in_ref, xs_ref, sem):
    del xs_in_ref
    tm = x_ref.shape[0]

    def row_copy(r, k):
        return pltpu.make_async_copy(x_ref.at[pl.ds(r, 1), :],
                                     xs_ref.at[pl.ds(dest_ref[0, 0, 2 * r + k], 1), :], sem)

    def issue(r, carry):
        row_copy(r, 0).start()
        row_copy(r, 1).start()
        return carry

    lax.fori_loop(0, tm, issue, 0)

    def drain(r, carry):
        row_copy(r, 0).wait()
        row_copy(r, 1).wait()
        return carry

    lax.fori_loop(0, tm, drain, 0)


def _moe_dispatch(x, dest, xs_init, tm=256):
    m, w = x.shape
    nb = m // tm
    return pl.pallas_call(
        _moe_dispatch_kernel,
        grid=(nb,),
        in_specs=[pl.BlockSpec((1, 1, 2 * tm), lambda i: (i, 0, 0), memory_space=pltpu.SMEM),
                  pl.BlockSpec((tm, w), lambda i: (i, 0)),
                  pl.BlockSpec(memory_space=pl.ANY)],
        out_specs=pl.BlockSpec(memory_space=pl.ANY),
        out_shape=jax.ShapeDtypeStruct(xs_init.shape, xs_init.dtype),
        scratch_shapes=[pltpu.SemaphoreType.DMA(())],
        input_output_aliases={2: 0},
        compiler_params=_cparams(("arbitrary",)),
        name="moe_dispatch",
    )(dest.reshape(nb, 1, 2 * tm), x, xs_init)


def _moe_ffn_kernel(blk_e_ref, n_used_ref, xs_ref, wg_ref, wu_ref, wd_ref, y_ref, u_scr):
    i = pl.program_id(0)
    j = pl.program_id(1)

    @pl.when(i < n_used_ref[0])
    def _():
        @pl.when(j == 0)
        def _():
            w = xs_ref[...]
            half = w.shape[1]
            u_scr[:, :half] = pltpu.bitcast(w << 16, F32).astype(BF16)
            u_scr[:, half:] = pltpu.bitcast(w & jnp.uint32(0xFFFF0000), F32).astype(BF16)
            y_ref[...] = jnp.zeros_like(y_ref)

        u = u_scr[...]
        gt = _dot(u, wg_ref[0])
        up = _dot(u, wu_ref[0])
        a = (gt * _sigmoid(gt) * up).astype(BF16)
        y_ref[...] += _dot(a, wd_ref[0])

    @pl.when((i >= n_used_ref[0]) & (j == 0))
    def _():
        y_ref[...] = jnp.zeros_like(y_ref)


def _moe_ffn(xs, blk_e, n_used, w_gu, w_down, tmb, tf=FFN_TF):
    n_rows, half = xs.shape
    d = 2 * half
    ff = w_down.shape[1]
    nf = ff // tf
    nb = n_rows // tmb

    def blk(i, n_used_ref):
        return jnp.minimum(i, n_used_ref[0] - 1)

    def fidx(i, j, n_used_ref):
        return jnp.where(i < n_used_ref[0], j, nf - 1)

    grid_spec = pltpu.PrefetchScalarGridSpec(
        num_scalar_prefetch=2,
        grid=(nb, nf),
        in_specs=[
            pl.BlockSpec((tmb, half), lambda i, j, be, nu: (blk(i, nu), 0)),
            pl.BlockSpec((1, d, tf), lambda i, j, be, nu: (be[blk(i, nu)], 0, fidx(i, j, nu))),
            pl.BlockSpec((1, d, tf), lambda i, j, be, nu: (be[blk(i, nu)], 0, nf + fidx(i, j, nu))),
            pl.BlockSpec((1, tf, d), lambda i, j, be, nu: (be[blk(i, nu)], fidx(i, j, nu), 0)),
        ],
        out_specs=pl.BlockSpec((tmb, d), lambda i, j, be, nu: (i, 0)),
        scratch_shapes=[pltpu.VMEM((tmb, d), BF16)],
    )
    return pl.pallas_call(
        _moe_ffn_kernel,
        grid_spec=grid_spec,
        out_shape=jax.ShapeDtypeStruct((n_rows, d), F32),
        compiler_params=_cparams(("arbitrary", "arbitrary")),
        name="moe_grouped_ffn",
    )(blk_e, n_used, xs, w_gu, w_gu, w_down)


def _moe_combine_kernel(dest_ref, ys_ref, wts_ref, h_ref, gate_ref, g2_ref, out_ref,
                        buf0, buf1, sem):
    tm = h_ref.shape[0]
    bufs = (buf0, buf1)

    def row_copy(r, k):
        return pltpu.make_async_copy(ys_ref.at[pl.ds(dest_ref[0, 0, 2 * r + k], 1), :],
                                     bufs[k].at[pl.ds(r, 1), :], sem)

    def issue(r, carry):
        row_copy(r, 0).start()
        row_copy(r, 1).start()
        return carry

    lax.fori_loop(0, tm, issue, 0)

    def drain(r, carry):
        row_copy(r, 0).wait()
        row_copy(r, 1).wait()
        return carry

    lax.fori_loop(0, tm, drain, 0)

    wts = wts_ref[...]
    y = wts[:, 0:1] * buf0[...] + wts[:, 1:2] * buf1[...]
    out_ref[...] = h_ref[...] + gate_ref[0] * (_rms(y) * g2_ref[...])


def _moe_combine(ys, dest, wts, h, gate, g2, rows_per_batch, tm=256):
    m, d = h.shape
    nb = m // tm
    bpb = rows_per_batch // tm
    row = lambda w: pl.BlockSpec((tm, w), lambda i: (i, 0))
    return pl.pallas_call(
        _moe_combine_kernel,
        grid=(nb,),
        in_specs=[pl.BlockSpec((1, 1, 2 * tm), lambda i: (i, 0, 0), memory_space=pltpu.SMEM),
                  pl.BlockSpec(memory_space=pl.ANY),
                  row(LANES), row(d),
                  pl.BlockSpec((1, 1, d), lambda i: (i // bpb, 0, 0)),
                  pl.BlockSpec((1, d), lambda i: (0, 0))],
        out_specs=row(d),
        out_shape=jax.ShapeDtypeStruct((m, d), F32),
        scratch_shapes=[pltpu.VMEM((tm, d), F32), pltpu.VMEM((tm, d), F32),
                        pltpu.SemaphoreType.DMA(())],
        compiler_params=_cparams(("arbitrary",)),
        name="moe_combine",
    )(dest.reshape(nb, 1, 2 * tm), ys, wts, h, gate, g2)


def _moe(h, g, scale, shift, w_router, w_gu, w_down, gate, g2, rows_per_batch, tmb=512):
    m, d = h.shape
    w_router_pad = jnp.pad(w_router, ((0, 0), (0, LANES - N_EXPERTS)))
    x_packed, meta, wts, cnt = _moe_pre(h, g, scale, shift, w_router_pad, rows_per_batch)

    counts = cnt[0, :N_EXPERTS].astype(jnp.int32)
    padded = (counts + tmb - 1) // tmb * tmb
    pend = jnp.cumsum(padded)
    pstart = pend - padded
    dest = (pstart[meta[:, 0:2]] + meta[:, 2:4]).astype(jnp.int32).reshape(-1)
    n_rows = (2 * m + N_EXPERTS * (tmb - 1)) // tmb * tmb
    nb = n_rows // tmb
    blk_e = jnp.minimum(jnp.searchsorted(pend, jnp.arange(nb, dtype=jnp.int32) * tmb, side='right'),
                        N_EXPERTS - 1).astype(jnp.int32)
    n_used = (pend[-1:] // tmb).astype(jnp.int32)

    xs = _moe_dispatch(x_packed, dest, jnp.zeros((n_rows, d // 2), jnp.uint32))
    ys = _moe_ffn(xs, blk_e, n_used, w_gu, w_down, tmb)
    return _moe_combine(ys, dest, wts, h, gate, g2, rows_per_batch)


def _swap_halves(w):
    half = w.shape[-1] // 2
    return jnp.concatenate([w[..., half:], w[..., :half]], axis=-1)


def _mla_weights(w_q_a, q_norm_g, w_q_b, w_kv_a, kv_norm_g, w_kv_b):
    ql = w_q_b.shape[0]
    kvl = w_kv_b.shape[0]
    wq = w_q_b.reshape(ql, MLA_HEADS, QK_NOPE + QK_ROPE)
    rope = wq[:, :, QK_NOPE:]
    pad_r = LANES - QK_ROPE
    wqm = jnp.concatenate([wq, jnp.zeros((ql, MLA_HEADS, QK_PAD - QK_NOPE - QK_ROPE), F32)], axis=-1)
    wqr = jnp.concatenate([_swap_halves(rope), jnp.zeros((ql, MLA_HEADS, pad_r), F32)], axis=-1)
    wkv = w_kv_b.reshape(kvl, MLA_HEADS, QK_NOPE + V_HEAD)
    kr = w_kv_a[:, kvl:]
    dm = w_kv_a.shape[0]
    wkr = jnp.concatenate([kr, jnp.zeros((dm, pad_r), F32)], axis=-1)
    wkrr = jnp.concatenate([_swap_halves(kr), jnp.zeros((dm, pad_r), F32)], axis=-1)
    bf = lambda a: a.astype(BF16)
    return (bf(w_q_a), q_norm_g.reshape(1, -1), bf(wqm.reshape(ql, -1)), bf(wqr.reshape(ql, -1)),
            bf(w_kv_a[:, :kvl]), bf(wkr), bf(wkrr), kv_norm_g.reshape(1, -1),
            bf(wkv[:, :, :QK_NOPE].reshape(kvl, -1)), bf(wkv[:, :, QK_NOPE:].reshape(kvl, -1)))


def kernel(x, c, positions, ada_w, ada_b, norm_g, hg_w_in, hg_lb_logits, hg_out_norm_g, hg_w_out, kv_src_norm_g, kv_src_ada_w, kv_src_ada_b, mla_w_kv_a, mla_kv_norm_g, mla_w_kv_b, mla_w_q_a, mla_q_norm_g, mla_w_q_b, mla_w_o, ffn_w_gu, ffn_w_down, moe_w_router, moe_w_gu, moe_w_down):
    batch, seq, d = x.shape
    m = batch * seq
    bf = lambda a: a.astype(BF16)

    c8 = jnp.pad(c, ((0, 8 - batch), (0, 0)))
    ada = _modulation(c8, ada_w.reshape(4, d, 3 * d), ada_b.reshape(4, 3 * d))[:, :batch]
    kvm = _modulation(c8, kv_src_ada_w[None], kv_src_ada_b[None])[0, :batch]

    def mods(idx):
        a = ada[idx]
        return [a[:, i * d:(i + 1) * d].reshape(batch, 1, d) for i in range(3)]

    vec = lambda a: a.reshape(1, -1)
    lb_all = jnp.cumsum(jax.nn.softmax(hg_lb_logits.astype(F32), axis=0), axis=0)

    h = x.reshape(m, d)

    shift, scale, gate = mods(0)
    q, f_logit, iv, gg = _hg_proj(h, vec(norm_g[0, 0, 0]), scale, shift, bf(hg_w_in[0]), seq)
    o, (ffn_gu_bf, ffn_down_bf, hg_out_bf, moe_down_bf) = _with_side(
        lambda side: _hgrn2(q, f_logit, iv, vec(lb_all[0]), batch, side=side),
        [ffn_w_gu[0], ffn_w_down[0], hg_w_out[0], moe_w_down[0]], _hgrn2_steps(m))
    h = _oproj(o, hg_out_bf, h, gate, vec(norm_g[0, 0, 1]), seq,
               gg=gg, gout=vec(hg_out_norm_g[0]))

    shift, scale, gate = mods(1)
    h_in = h
    h, (moe_gu_bf, mla_o_bf) = _with_side(
        lambda side: _ffn(h_in, vec(norm_g[0, 1, 0]), scale, shift, ffn_gu_bf, ffn_down_bf,
                          gate, vec(norm_g[0, 1, 1]), seq, side=side),
        [moe_w_gu[0], mla_w_o[0]], _ffn_steps(m, ffn_w_down.shape[1]))

    shift_k, scale_k = [kvm[:, i * d:(i + 1) * d].reshape(batch, 1, d) for i in range(2)]
    shift, scale, gate = mods(2)
    cos, sin = _rope_tables(positions.reshape(m, 1).astype(F32))
    wts = _mla_weights(mla_w_q_a[0], mla_q_norm_g[0], mla_w_q_b[0],
                       mla_w_kv_a, mla_kv_norm_g, mla_w_kv_b)
    qh, kh, vh = _mla_proj(h, vec(norm_g[1, 0, 0]), scale, shift,
                           vec(kv_src_norm_g), scale_k, shift_k, wts, cos, sin, seq)
    att, _ = _flash(qh, kh, vh, batch)
    h = _oproj(att, mla_o_bf, h, gate, vec(norm_g[1, 0, 1]), seq)

    shift, scale, gate = mods(3)
    h = _moe(h, vec(norm_g[1, 1, 0]), scale, shift, moe_w_router[0],
             moe_gu_bf, moe_down_bf, gate, vec(norm_g[1, 1, 1]), seq)
    return h.reshape(batch, seq, d)
```

```python
import functools

import jax
import jax.numpy as jnp
from jax import lax
from jax.experimental import pallas as pl
from jax.experimental.pallas import tpu as pltpu

F32 = jnp.float32
BF16 = jnp.bfloat16

EPS = 1e-6
LANES = 128
F32_SUBLANES = 8
BF16_SUBLANES = 16
SIDE_BLOCK_BYTES = 3 * 1024 * 1024

HG_HEADS = 16
HG_KDIM = 128
HG_CHUNK = 64
HG_SUB = 16
HG_EXP_CLAMP = 80.0

MLA_HEADS = 16
QK_NOPE = 128
QK_ROPE = 64
V_HEAD = 128
QK_PAD = 256
MLA_SCALE = (QK_NOPE + QK_ROPE) ** -0.5
Q_SCALE = MLA_SCALE * 1.4426950408889634
ROPE_THETA = 10000.0
FLASH_TQ = 2048
FLASH_TK = 1024

N_EXPERTS = 8
FFN_TF = 512

VMEM_LIMIT = 56 * 1024 * 1024


def _cparams(sem):
    return pltpu.CompilerParams(dimension_semantics=sem, vmem_limit_bytes=VMEM_LIMIT)


def _rms(x):
    return x * lax.rsqrt(jnp.mean(x * x, axis=-1, keepdims=True) + EPS)


def _sigmoid(x):
    return 1.0 / (1.0 + jnp.exp(-x))


def _dot(a, b):
    return jnp.dot(a, b, preferred_element_type=F32)


def _dot_nt(a, b):
    return lax.dot_general(a, b, (((1,), (1,)), ((), ())), preferred_element_type=F32)


def _dot_tn(a, b):
    return lax.dot_general(a, b, (((0,), (0,)), ((), ())), preferred_element_type=F32)


def _side_rows(w, n_steps):
    e, r, c = w.shape
    for rb in range(BF16_SUBLANES, r + 1, BF16_SUBLANES):
        if r % rb == 0 and e * (r // rb) <= n_steps:
            return rb if rb * c * 4 <= SIDE_BLOCK_BYTES else None
    return None


def _side_ok(weights, n_steps):
    return all(_side_rows(w, n_steps) is not None for w in weights)


def _side_plan(weights, n_steps, step_of):
    specs, blocks = [], []
    for w in weights:
        rb = _side_rows(w, n_steps)
        per = w.shape[1] // rb
        n_blk = w.shape[0] * per
        blocks.append(n_blk)

        def idx(*grid_ids, per=per, n_blk=n_blk):
            blk = jnp.minimum(step_of(*grid_ids), n_blk - 1)
            return (blk // per, blk % per, 0)

        specs.append(pl.BlockSpec((1, rb, w.shape[2]), idx))
    return specs, tuple(blocks)


def _side_cast(w_ins, w_outs):
    for w_in, w_out in zip(w_ins, w_outs):
        w_out[...] = w_in[...].astype(w_out.dtype)


def _with_side(run, weights, n_steps):
    w3 = tuple(w.reshape((1,) + w.shape) if w.ndim == 2 else w for w in weights)
    if _side_ok(w3, n_steps):
        res, copies = run(w3)
        return res, [cp.reshape(w.shape) for cp, w in zip(copies, weights)]
    res, _ = run(())
    return res, [w.astype(BF16) for w in weights]


def _mod_kernel(c_ref, w_ref, b_ref, o_ref):
    c = c_ref[...]
    sc = (c * _sigmoid(c)).astype(BF16)
    o_ref[0] = _dot(sc, w_ref[0].astype(BF16)) + b_ref[0]


def _modulation(c8, w, b, tn=1024):
    g, d, n = w.shape
    return pl.pallas_call(
        _mod_kernel,
        grid=(g, n // tn),
        in_specs=[pl.BlockSpec((8, d), lambda i, j: (0, 0)),
                  pl.BlockSpec((1, d, tn), lambda i, j: (i, 0, j)),
                  pl.BlockSpec((1, 1, tn), lambda i, j: (i, 0, j))],
        out_specs=pl.BlockSpec((1, 8, tn), lambda i, j: (i, 0, j)),
        out_shape=jax.ShapeDtypeStruct((g, 8, n), F32),
        compiler_params=_cparams(("arbitrary", "arbitrary")),
        name="modulation",
    )(c8, w, b.reshape(g, 1, n))


def _rope_kernel(pos_ref, inv_ref, cos_ref, sin_ref):
    ang = pos_ref[...] * inv_ref[...]
    lane = lax.broadcasted_iota(jnp.int32, ang.shape, 1)
    half = QK_ROPE // 2
    cos_ref[...] = jnp.where(lane < QK_ROPE, jnp.cos(ang), 0.0)
    sin_ref[...] = jnp.where(lane < half, -jnp.sin(ang),
                             jnp.where(lane < QK_ROPE, jnp.sin(ang), 0.0))


def _rope_tables(pos, tm=512):
    m = pos.shape[0]
    half = QK_ROPE // 2
    inv = 1.0 / (ROPE_THETA ** (jnp.arange(0, QK_ROPE, 2, dtype=F32) / QK_ROPE))
    inv128 = jnp.concatenate([inv, inv, jnp.zeros((LANES - 2 * half,), F32)]).reshape(1, LANES)
    return pl.pallas_call(
        _rope_kernel,
        grid=(m // tm,),
        in_specs=[pl.BlockSpec((tm, 1), lambda i: (i, 0)),
                  pl.BlockSpec((1, LANES), lambda i: (0, 0))],
        out_specs=[pl.BlockSpec((tm, LANES), lambda i: (i, 0))] * 2,
        out_shape=[jax.ShapeDtypeStruct((m, LANES), F32)] * 2,
        compiler_params=_cparams(("arbitrary",)),
        name="rope_tables",
    )(pos, inv128)


def _hg_proj_kernel(h_ref, g_ref, sc_ref, sh_ref, w_ref, q_ref, f_ref, i_ref, gg_ref, u_scr):
    j = pl.program_id(1)

    @pl.when(j == 0)
    def _():
        u = _rms(h_ref[...]) * g_ref[...] * (1.0 + sc_ref[0]) + sh_ref[0]
        u_scr[...] = u.astype(BF16)

    r = _dot(u_scr[...], w_ref[...])
    for idx, ref in enumerate((q_ref, f_ref, i_ref, gg_ref)):
        @pl.when(j == idx)
        def _(ref=ref):
            ref[...] = r.astype(ref.dtype)


def _hg_proj(h, g, scale, shift, w_in, rows_per_batch, tm=512):
    m, d = h.shape
    bpb = rows_per_batch // tm
    row = lambda i, j: (i, 0)
    mod = lambda i, j: (i // bpb, 0, 0)
    return pl.pallas_call(
        _hg_proj_kernel,
        grid=(m // tm, 4),
        in_specs=[pl.BlockSpec((tm, d), row),
                  pl.BlockSpec((1, d), lambda i, j: (0, 0)),
                  pl.BlockSpec((1, 1, d), mod),
                  pl.BlockSpec((1, 1, d), mod),
                  pl.BlockSpec((d, d), lambda i, j: (0, j))],
        out_specs=[pl.BlockSpec((tm, d), row)] * 4,
        out_shape=[jax.ShapeDtypeStruct((m, d), BF16), jax.ShapeDtypeStruct((m, d), F32),
                   jax.ShapeDtypeStruct((m, d), BF16), jax.ShapeDtypeStruct((m, d), BF16)],
        scratch_shapes=[pltpu.VMEM((tm, d), BF16)],
        compiler_params=_cparams(("arbitrary", "arbitrary")),
        name="hg_proj",
    )(h, g, scale, shift, w_in)


def _hgrn2_kernel(*refs, n_chunks, side_blocks):
    n_side = len(side_blocks)
    q_ref, f_ref, v_ref, lb_ref = refs[:4]
    o_ref = refs[4 + n_side]
    st_ref = refs[5 + 2 * n_side]
    c, sub = HG_CHUNK, HG_SUB
    n_sub = c // sub

    @pl.when(pl.program_id(2) == 0)
    def _():
        st_ref[...] = jnp.zeros_like(st_ref)

    _side_cast(refs[4:4 + n_side], refs[5 + n_side:5 + 2 * n_side])

    lb = lb_ref[...]
    row = lax.broadcasted_iota(jnp.int32, (c, c), 0)
    col = lax.broadcasted_iota(jnp.int32, (c, c), 1)
    tri = (row >= col).astype(BF16)
    row2 = lax.broadcasted_iota(jnp.int32, (c, 2 * c), 0)
    col2 = lax.broadcasted_iota(jnp.int32, (c, 2 * c), 1)
    c_bits, sub_bits = c.bit_length() - 1, sub.bit_length() - 1
    mask2 = ((col2 >> c_bits) == ((row2 >> sub_bits) & 1)) & ((col2 & (c - 1)) <= row2)

    chunks = [slice(ci * c, (ci + 1) * c) for ci in range(n_chunks)]
    ks, bs = [], []
    for sl in chunks:
        f = lb + (1.0 - lb) * _sigmoid(f_ref[sl, :])
        lf = jnp.log(f)
        lf_hi = lf.astype(BF16)
        lf_lo = (lf - lf_hi.astype(F32)).astype(BF16)
        ks.append(1.0 - f)
        bs.append(_dot(tri, lf_hi) + _dot(tri, lf_lo))
    a_fulls, qbs, kdecs, decays = [], [], [], []
    for sl, k, b in zip(chunks, ks, bs):
        q = q_ref[sl, :].astype(F32)
        b_last = b[c - 1:c, :]
        refs = [jnp.zeros_like(b_last)] + [b[j * sub - 1:j * sub, :] for j in range(1, n_sub)]
        rrow = jnp.concatenate([jnp.broadcast_to(r, (sub, r.shape[1])) for r in refs], axis=0)
        qe = q * jnp.exp(b - rrow)
        qbs.append((qe * jnp.exp(rrow)).astype(BF16))
        ke = jnp.concatenate(
            [k * jnp.exp(jnp.minimum(r - b, HG_EXP_CLAMP)) for r in refs], axis=0)
        a_fulls.append(_dot_nt(qe.astype(BF16), ke.astype(BF16)))
        kdecs.append((k * jnp.exp(b_last - b)).astype(BF16))
        decays.append(jnp.exp(b_last))
    o_intras, incs = [], []
    for sl, a_full, kdec in zip(chunks, a_fulls, kdecs):
        v = v_ref[sl, :]
        a2 = jnp.concatenate(
            [a_full[j * sub:(j + 1) * sub, (j // 2) * 2 * c:(j // 2 + 1) * 2 * c]
             for j in range(n_sub)], axis=0)
        a2 = jnp.where(mask2, a2, 0.0).astype(BF16)
        o_intras.append(_dot(a2, jnp.concatenate([v, v], axis=0)))
        incs.append(_dot_tn(v, kdec))
    st = st_ref[...]
    for sl, qb, o_intra, inc, decay in zip(chunks, qbs, o_intras, incs, decays):
        o_ref[sl, :] = o_intra + _dot_nt(qb, st.astype(BF16))
        st = st * decay + inc
    st_ref[...] = st


def _hgrn2_steps(m, tb=512):
    return HG_HEADS * (m // tb)


def _hgrn2(q, f_logit, v, lb, batch, side=(), tb=512):
    m, d = q.shape
    s = m // batch
    nt = s // tb
    blk = lambda b, h, t: (b * nt + t, h)
    side_specs, side_blocks = _side_plan(
        side, batch * HG_HEADS * nt, lambda b, h, t: (b * HG_HEADS + h) * nt + t)
    outs = pl.pallas_call(
        functools.partial(_hgrn2_kernel, n_chunks=tb // HG_CHUNK, side_blocks=side_blocks),
        grid=(batch, HG_HEADS, nt),
        in_specs=[pl.BlockSpec((tb, HG_KDIM), blk)] * 3
        + [pl.BlockSpec((1, HG_KDIM), lambda b, h, t: (0, h))] + side_specs,
        out_specs=[pl.BlockSpec((tb, HG_KDIM), blk)] + side_specs,
        out_shape=[jax.ShapeDtypeStruct((m, d), F32)]
        + [jax.ShapeDtypeStruct(w.shape, BF16) for w in side],
        scratch_shapes=[pltpu.VMEM((HG_KDIM, HG_KDIM), F32)],
        compiler_params=_cparams(("arbitrary", "arbitrary", "arbitrary")),
        name="hgrn2_recurrence",
    )(q, f_logit, v, lb, *side)
    return outs[0], list(outs[1:])


def _oproj_kernel(*refs, gated):
    if gated:
        o_ref, gg_ref, gout_ref, w_ref, h_ref, gate_ref, g2_ref, out_ref = refs
        gg = gg_ref[...].astype(F32)
        x = _rms(o_ref[...]) * gout_ref[...] * (gg * _sigmoid(gg))
    else:
        o_ref, w_ref, h_ref, gate_ref, g2_ref, out_ref = refs
        x = o_ref[...]
    y = _dot(x.astype(BF16), w_ref[...])
    out_ref[...] = h_ref[...] + gate_ref[0] * (_rms(y) * g2_ref[...])


def _oproj(o, w, h, gate, g2, rows_per_batch, gg=None, gout=None, tm=512):
    m, d = h.shape
    bpb = rows_per_batch // tm
    row = pl.BlockSpec((tm, d), lambda i: (i, 0))
    vec = pl.BlockSpec((1, d), lambda i: (0, 0))
    mod = pl.BlockSpec((1, 1, d), lambda i: (i // bpb, 0, 0))
    wsp = pl.BlockSpec(w.shape, lambda i: (0, 0))
    gated = gg is not None
    if gated:
        args, specs = (o, gg, gout, w, h, gate, g2), [row, row, vec, wsp, row, mod, vec]
    else:
        args, specs = (o, w, h, gate, g2), [row, wsp, row, mod, vec]
    return pl.pallas_call(
        functools.partial(_oproj_kernel, gated=gated),
        grid=(m // tm,),
        in_specs=specs,
        out_specs=row,
        out_shape=jax.ShapeDtypeStruct((m, d), F32),
        compiler_params=_cparams(("arbitrary",)),
        name="hg_out_proj" if gated else "mla_out_proj",
    )(*args)


def _ffn_kernel(*refs, side_blocks):
    n_side = len(side_blocks)
    h_ref, g_ref, sc_ref, sh_ref, wg_ref, wu_ref, wd_ref, gate_ref, g2_ref = refs[:9]
    out_ref = refs[9 + n_side]
    u_scr, acc_scr = refs[10 + 2 * n_side:]
    j = pl.program_id(1)

    @pl.when(j == 0)
    def _():
        u = _rms(h_ref[...]) * g_ref[...] * (1.0 + sc_ref[0]) + sh_ref[0]
        u_scr[...] = u.astype(BF16)
        acc_scr[...] = jnp.zeros_like(acc_scr)

    _side_cast(refs[9:9 + n_side], refs[10 + n_side:10 + 2 * n_side])
    u = u_scr[...]
    gt = _dot(u, wg_ref[...])
    up = _dot(u, wu_ref[...])
    a = (gt * _sigmoid(gt) * up).astype(BF16)
    acc_scr[...] += _dot(a, wd_ref[...])

    @pl.when(j == pl.num_programs(1) - 1)
    def _():
        out_ref[...] = h_ref[...] + gate_ref[0] * (_rms(acc_scr[...]) * g2_ref[...])


def _ffn_steps(m, ff, tm=512, tf=FFN_TF):
    return (m // tm) * (ff // tf)


def _ffn(h, g, scale, shift, w_gu, w_down, gate, g2, rows_per_batch, side=(), tm=512, tf=FFN_TF):
    m, d = h.shape
    ff = w_down.shape[0]
    nf = ff // tf
    bpb = rows_per_batch // tm
    row = pl.BlockSpec((tm, d), lambda i, j: (i, 0))
    vec = pl.BlockSpec((1, d), lambda i, j: (0, 0))
    mod = pl.BlockSpec((1, 1, d), lambda i, j: (i // bpb, 0, 0))
    side_specs, side_blocks = _side_plan(side, (m // tm) * nf, lambda i, j: i * nf + j)
    outs = pl.pallas_call(
        functools.partial(_ffn_kernel, side_blocks=side_blocks),
        grid=(m // tm, nf),
        in_specs=[row, vec, mod, mod,
                  pl.BlockSpec((d, tf), lambda i, j: (0, j)),
                  pl.BlockSpec((d, tf), lambda i, j: (0, nf + j)),
                  pl.BlockSpec((tf, d), lambda i, j: (j, 0)),
                  mod, vec] + side_specs,
        out_specs=[row] + side_specs,
        out_shape=[jax.ShapeDtypeStruct((m, d), F32)]
        + [jax.ShapeDtypeStruct(w.shape, BF16) for w in side],
        scratch_shapes=[pltpu.VMEM((tm, d), BF16), pltpu.VMEM((tm, d), F32)],
        compiler_params=_cparams(("arbitrary", "arbitrary")),
        name="dense_ffn",
    )(h, g, scale, shift, w_gu, w_gu, w_down, gate, g2, *side)
    return outs[0], list(outs[1:])


def _mla_proj_kernel(h_ref, gq_ref, scq_ref, shq_ref, gk_ref, sck_ref, shk_ref,
                     wqa_ref, qg_ref, wqm_ref, wqr_ref,
                     wkva_ref, wkr_ref, wkrr_ref, kg_ref, wk_ref, wv_ref,
                     cos_ref, sin_ref, q_ref, k_ref, v_ref):
    xhat = _rms(h_ref[...])
    uq = (xhat * gq_ref[...] * (1.0 + scq_ref[0]) + shq_ref[0]).astype(BF16)
    xk = (xhat * gk_ref[...] * (1.0 + sck_ref[0]) + shk_ref[0]).astype(BF16)
    cos = cos_ref[...]
    sin = sin_ref[...]
    qn = (_rms(_dot(uq, wqa_ref[...])) * qg_ref[...]).astype(BF16)
    cn = (_rms(_dot(xk, wkva_ref[...])) * kg_ref[...]).astype(BF16)
    krope = (_dot(xk, wkr_ref[...]) * cos + _dot(xk, wkrr_ref[...]) * sin).astype(BF16)
    ones = jnp.ones((cn.shape[0], V_HEAD), BF16)
    for hp in range(MLA_HEADS // 2):
        ps = slice(hp * 2 * LANES, (hp + 1) * 2 * LANES)
        qr2 = _dot(qn, wqr_ref[:, ps]) * Q_SCALE
        kn2 = _dot(cn, wk_ref[:, ps]).astype(BF16)
        v2 = _dot(cn, wv_ref[:, ps]).astype(BF16)
        for sub in range(2):
            hd = 2 * hp + sub
            ls = slice(sub * LANES, (sub + 1) * LANES)
            qm = _dot(qn, wqm_ref[:, hd * QK_PAD:(hd + 1) * QK_PAD]) * Q_SCALE
            q_ref[:, hd * QK_PAD:hd * QK_PAD + QK_NOPE] = qm[:, :QK_NOPE].astype(BF16)
            q_ref[:, hd * QK_PAD + QK_NOPE:(hd + 1) * QK_PAD] = (
                qm[:, QK_NOPE:] * cos + qr2[:, ls] * sin).astype(BF16)
            k_ref[:, hd * QK_PAD:hd * QK_PAD + QK_NOPE] = kn2[:, ls]
            k_ref[:, hd * QK_PAD + QK_NOPE:(hd + 1) * QK_PAD] = krope
            v_ref[:, 2 * hd * V_HEAD:(2 * hd + 1) * V_HEAD] = v2[:, ls]
            v_ref[:, (2 * hd + 1) * V_HEAD:(2 * hd + 2) * V_HEAD] = ones


def _mla_proj(h, gq, scq, shq, gk, sck, shk, wts, cos, sin, rows_per_batch, tm=256):
    m, d = h.shape
    bpb = rows_per_batch // tm
    row = lambda w: pl.BlockSpec((tm, w), lambda i: (i, 0))
    vec = lambda w: pl.BlockSpec((1, w), lambda i: (0, 0))
    mod = pl.BlockSpec((1, 1, d), lambda i: (i // bpb, 0, 0))
    full = lambda a: pl.BlockSpec(a.shape, lambda i: (0, 0))
    wqa, qg, wqm, wqr, wkva, wkr, wkrr, kg, wk, wv = wts
    return pl.pallas_call(
        _mla_proj_kernel,
        grid=(m // tm,),
        in_specs=[row(d), vec(d), mod, mod, vec(d), mod, mod,
                  full(wqa), full(qg), full(wqm), full(wqr),
                  full(wkva), full(wkr), full(wkrr), full(kg), full(wk), full(wv),
                  row(LANES), row(LANES)],
        out_specs=[row(MLA_HEADS * QK_PAD), row(MLA_HEADS * QK_PAD), row(MLA_HEADS * 2 * V_HEAD)],
        out_shape=[jax.ShapeDtypeStruct((m, MLA_HEADS * QK_PAD), BF16),
                   jax.ShapeDtypeStruct((m, MLA_HEADS * QK_PAD), BF16),
                   jax.ShapeDtypeStruct((m, MLA_HEADS * 2 * V_HEAD), BF16)],
        compiler_params=_cparams(("arbitrary",)),
        name="mla_proj",
    )(h, gq, scq, shq, gk, sck, shk, wqa, qg, wqm, wqr, wkva, wkr, wkrr, kg, wk, wv, cos, sin)


def _flash_kernel(*refs, tq, tk, slab, side_blocks):
    n_side = len(side_blocks)
    q_ref, k_ref, v_ref = refs[:3]
    o_ref = refs[3 + n_side]
    s_a, s_b, p_scr, m_scr, al_scr, acc_scr = refs[4 + 2 * n_side:]
    qi = pl.program_id(2)
    _side_cast(refs[3:3 + n_side], refs[4 + n_side:4 + 2 * n_side])

    s_bufs = (s_a, s_b)
    m_scr[...] = jnp.full_like(m_scr, -jnp.inf)
    acc_scr[...] = jnp.zeros_like(acc_scr)
    n_col = tk // LANES
    n_diag = tq // tk

    def kv_rows(j):
        return pl.ds(pl.multiple_of(j * tk, tk), tk)

    def scores(j, dst, row0):
        dst[row0:, :] = _dot_nt(q_ref[row0:, :], k_ref[kv_rows(j), :])

    def consume(j, src, row0, diag):
        for r0 in range(row0, tq, slab):
            rows = slice(r0, r0 + slab)
            cols = [src[rows, cb * LANES:(cb + 1) * LANES] for cb in range(n_col)]
            if diag and r0 < row0 + tk:
                rowp = (r0 - row0) + lax.broadcasted_iota(jnp.int32, (slab, LANES), 0)
                lane = lax.broadcasted_iota(jnp.int32, (slab, LANES), 1)
                cols = [jnp.where(cb * LANES + lane <= rowp, cols[cb], -jnp.inf)
                        for cb in range(n_col)]
            mx = functools.reduce(jnp.maximum, cols)
            m_old = m_scr[rows, :]
            m_new = jnp.maximum(m_old, jnp.max(mx, axis=-1, keepdims=True))
            al_scr[rows, :] = jnp.exp2(m_old - m_new)
            m_scr[rows, :] = m_new
            for cb in range(n_col):
                p_scr[rows, cb * LANES:(cb + 1) * LANES] = jnp.exp2(cols[cb] - m_new).astype(BF16)
        pv = _dot(p_scr[row0:, :], v_ref[kv_rows(j), :])
        al = al_scr[row0:, :]
        acc_scr[row0:, :V_HEAD] = acc_scr[row0:, :V_HEAD] * al + pv[:, :V_HEAD]
        acc_scr[row0:, V_HEAD:] = acc_scr[row0:, V_HEAD:] * al + pv[:, V_HEAD:]

    n_full = qi * n_diag
    scores(0, s_a, 0)

    def pair(pi, carry):
        for u in range(2):
            scores(2 * pi + u + 1, s_bufs[1 - u], 0)
            consume(2 * pi + u, s_bufs[u], 0, False)
        return carry

    lax.fori_loop(0, n_full // 2, pair, 0)
    for dg in range(n_diag):
        if dg + 1 < n_diag:
            scores(n_full + dg + 1, s_bufs[(dg + 1) % 2], (dg + 1) * tk)
        consume(n_full + dg, s_bufs[dg % 2], dg * tk, True)

    o_ref[...] = (acc_scr[:, :V_HEAD] / acc_scr[:, V_HEAD:]).astype(o_ref.dtype)


def _flash_tiles(seq):
    tq = min(FLASH_TQ, seq)
    return tq, min(FLASH_TK, tq // 2)


def _flash(q, k, v, batch, side=(), slab=64):
    m = q.shape[0]
    s = m // batch
    tq, tk = _flash_tiles(s)
    nq = s // tq
    side_specs, side_blocks = _side_plan(
        side, batch * MLA_HEADS * nq, lambda b, h, i: (b * MLA_HEADS + h) * nq + i)
    outs = pl.pallas_call(
        functools.partial(_flash_kernel, tq=tq, tk=tk, slab=slab, side_blocks=side_blocks),
        grid=(batch, MLA_HEADS, nq),
        in_specs=[pl.BlockSpec((tq, QK_PAD), lambda b, h, i: (b * nq + i, h)),
                  pl.BlockSpec((s, QK_PAD), lambda b, h, i: (b, h)),
                  pl.BlockSpec((s, 2 * V_HEAD), lambda b, h, i: (b, h))] + side_specs,
        out_specs=[pl.BlockSpec((tq, V_HEAD), lambda b, h, i: (b * nq + i, h))] + side_specs,
        out_shape=[jax.ShapeDtypeStruct((m, MLA_HEADS * V_HEAD), BF16)]
        + [jax.ShapeDtypeStruct(w.shape, BF16) for w in side],
        scratch_shapes=[pltpu.VMEM((tq, tk), F32), pltpu.VMEM((tq, tk), F32),
                        pltpu.VMEM((tq, tk), BF16), pltpu.VMEM((tq, LANES), F32),
                        pltpu.VMEM((tq, LANES), F32), pltpu.VMEM((tq, 2 * V_HEAD), F32)],
        compiler_params=_cparams(("arbitrary", "arbitrary", "arbitrary")),
        name="mla_flash",
    )(q, k, v, *side)
    return outs[0], list(outs[1:])


def _split3(x):
    hi = x.astype(BF16)
    lo = (x - hi.astype(F32)).astype(BF16)
    return hi, lo


def _moe_pre_kernel(h_ref, g_ref, sc_ref, sh_ref, wr_ref, xs_ref, meta_ref, wts_ref, cnt_ref,
                    run_scr):
    tm, d = h_ref.shape

    @pl.when(pl.program_id(0) == 0)
    def _():
        run_scr[...] = jnp.zeros_like(run_scr)

    u = _rms(h_ref[...]) * g_ref[...] * (1.0 + sc_ref[0]) + sh_ref[0]
    half = d // 2
    lo = pltpu.bitcast(u[:, :half].astype(BF16).astype(F32), jnp.uint32) >> 16
    hi = pltpu.bitcast(u[:, half:].astype(BF16).astype(F32), jnp.uint32)
    xs_ref[...] = hi | lo

    u_hi, u_lo = _split3(u)
    w_hi, w_lo = _split3(wr_ref[...])
    logits = _dot(u_hi, w_hi) + (_dot(u_hi, w_lo) + _dot(u_lo, w_hi))
    lane = lax.broadcasted_iota(jnp.int32, logits.shape, 1)
    lg = jnp.where(lane < N_EXPERTS, logits, -jnp.inf)
    m1 = jnp.max(lg, axis=-1, keepdims=True)
    i1 = jnp.min(jnp.where(lg == m1, lane, LANES), axis=-1, keepdims=True)
    lg2 = jnp.where(lane == i1, -jnp.inf, lg)
    m2 = jnp.max(lg2, axis=-1, keepdims=True)
    i2 = jnp.min(jnp.where(lg2 == m2, lane, LANES), axis=-1, keepdims=True)
    e = jnp.exp(m2 - m1)
    w1 = 1.0 / (1.0 + e)
    w2 = e * w1

    oh1 = lane == i1
    oh2 = lane == i2
    cnt = oh1.astype(F32) + oh2.astype(F32)
    r = lax.broadcasted_iota(jnp.int32, (tm, tm), 0)
    c = lax.broadcasted_iota(jnp.int32, (tm, tm), 1)
    strict = (r > c).astype(BF16)
    prefix = _dot(strict, cnt.astype(BF16)) + run_scr[...]
    rank1 = jnp.sum(jnp.where(oh1, prefix, 0.0), axis=-1, keepdims=True).astype(jnp.int32)
    rank2 = jnp.sum(jnp.where(oh2, prefix, 0.0), axis=-1, keepdims=True).astype(jnp.int32)
    run_scr[...] += jnp.sum(cnt, axis=0, keepdims=True)

    meta_ref[...] = jnp.where(lane == 0, i1, jnp.where(lane == 1, i2,
                              jnp.where(lane == 2, rank1, jnp.where(lane == 3, rank2, 0))))
    wts_ref[...] = jnp.where(lane == 0, w1, jnp.where(lane == 1, w2, 0.0))
    cnt_ref[...] = run_scr[...]


def _moe_pre(h, g, scale, shift, w_router_pad, rows_per_batch, tm=512):
    m, d = h.shape
    bpb = rows_per_batch // tm
    row = lambda w: pl.BlockSpec((tm, w), lambda i: (i, 0))
    vec = lambda w: pl.BlockSpec((1, w), lambda i: (0, 0))
    mod = pl.BlockSpec((1, 1, d), lambda i: (i // bpb, 0, 0))
    return pl.pallas_call(
        _moe_pre_kernel,
        grid=(m // tm,),
        in_specs=[row(d), vec(d), mod, mod, pl.BlockSpec((d, LANES), lambda i: (0, 0))],
        out_specs=[row(d // 2), row(LANES), row(LANES), vec(LANES)],
        out_shape=[jax.ShapeDtypeStruct((m, d // 2), jnp.uint32),
                   jax.ShapeDtypeStruct((m, LANES), jnp.int32),
                   jax.ShapeDtypeStruct((m, LANES), F32),
                   jax.ShapeDtypeStruct((1, LANES), F32)],
        scratch_shapes=[pltpu.VMEM((1, LANES), F32)],
        compiler_params=_cparams(("arbitrary",)),
        name="moe_route",
    )(h, g, scale, shift, w_router_pad)


def _moe_rows_per_step(tmb, nf):
    return -(-tmb // (nf * F32_SUBLANES)) * F32_SUBLANES


def _moe_ffn_kernel(blk_e_ref, n_used_ref, tok_cur_ref, tok_nxt_ref, x_hbm, wg_ref, wu_ref, wd_ref,
                    y_ref, xbuf, u_scr, sem, *, nf):
    i = pl.program_id(0)
    j = pl.program_id(1)
    tmb = u_scr.shape[0]
    per_step = _moe_rows_per_step(tmb, nf)
    n_issued = per_step * nf
    n_used = n_used_ref[0]
    slot = i % 2

    def row_copy(tok_ref, r, dst_slot):
        return pltpu.make_async_copy(x_hbm.at[pl.ds(tok_ref[0, 0, r], 1), :],
                                     xbuf.at[dst_slot, pl.ds(r, 1), :], sem.at[dst_slot])

    def block_wait(dst_slot):
        pltpu.make_async_copy(x_hbm.at[pl.ds(0, n_issued), :],
                              xbuf.at[dst_slot, pl.ds(0, n_issued), :], sem.at[dst_slot]).wait()

    @pl.when((i == 0) & (j == 0))
    def _():
        def issue(r, carry):
            row_copy(tok_cur_ref, r, 0).start()
            return carry
        lax.fori_loop(0, n_issued, issue, 0)

    @pl.when(i < n_used)
    def _():
        @pl.when(j == 0)
        def _():
            block_wait(slot)
            w = xbuf[slot, :tmb, :]
            half = w.shape[1]
            u_scr[:, :half] = pltpu.bitcast(w << 16, F32).astype(BF16)
            u_scr[:, half:] = pltpu.bitcast(w & jnp.uint32(0xFFFF0000), F32).astype(BF16)
            y_ref[...] = jnp.zeros_like(y_ref)

        for k in range(per_step):
            row_copy(tok_nxt_ref, j * per_step + k, 1 - slot).start()
        u = u_scr[...]
        gt = _dot(u, wg_ref[0])
        up = _dot(u, wu_ref[0])
        a = (gt * _sigmoid(gt) * up).astype(BF16)
        y_ref[...] += _dot(a, wd_ref[0])

        @pl.when((i == n_used - 1) & (j == nf - 1))
        def _():
            block_wait(1 - slot)

    @pl.when((i >= n_used) & (j == 0))
    def _():
        y_ref[...] = jnp.zeros_like(y_ref)


def _moe_ffn(x_packed, row_tok, blk_e, n_used, w_gu, w_down, tmb, tf=FFN_TF):
    half = x_packed.shape[1]
    d = 2 * half
    n_rows = row_tok.shape[0]
    ff = w_down.shape[1]
    nf = ff // tf
    nb = n_rows // tmb
    n_issued = _moe_rows_per_step(tmb, nf) * nf
    tok = jnp.pad(row_tok.reshape(nb, 1, tmb), ((0, 0), (0, 0), (0, n_issued - tmb)))

    def blk(i, n_used_ref):
        return jnp.minimum(i, n_used_ref[0] - 1)

    def fidx(i, j, n_used_ref):
        return jnp.where(i < n_used_ref[0], j, nf - 1)

    grid_spec = pltpu.PrefetchScalarGridSpec(
        num_scalar_prefetch=2,
        grid=(nb, nf),
        in_specs=[
            pl.BlockSpec((1, 1, n_issued), lambda i, j, be, nu: (i, 0, 0), memory_space=pltpu.SMEM),
            pl.BlockSpec((1, 1, n_issued), lambda i, j, be, nu: (jnp.minimum(i + 1, nb - 1), 0, 0),
                         memory_space=pltpu.SMEM),
            pl.BlockSpec(memory_space=pl.ANY),
            pl.BlockSpec((1, d, tf), lambda i, j, be, nu: (be[blk(i, nu)], 0, fidx(i, j, nu))),
            pl.BlockSpec((1, d, tf), lambda i, j, be, nu: (be[blk(i, nu)], 0, nf + fidx(i, j, nu))),
            pl.BlockSpec((1, tf, d), lambda i, j, be, nu: (be[blk(i, nu)], fidx(i, j, nu), 0)),
        ],
        out_specs=pl.BlockSpec((tmb, d), lambda i, j, be, nu: (i, 0)),
        scratch_shapes=[pltpu.VMEM((2, n_issued, half), jnp.uint32), pltpu.VMEM((tmb, d), BF16),
                        pltpu.SemaphoreType.DMA((2,))],
    )
    return pl.pallas_call(
        functools.partial(_moe_ffn_kernel, nf=nf),
        grid_spec=grid_spec,
        out_shape=jax.ShapeDtypeStruct((n_rows, d), F32),
        compiler_params=_cparams(("arbitrary", "arbitrary")),
        name="moe_grouped_ffn",
    )(blk_e, n_used, tok, tok, x_packed, w_gu, w_gu, w_down)


def _moe_combine_kernel(dest_ref, ys_ref, wts_ref, h_ref, gate_ref, g2_ref, out_ref,
                        buf0, buf1, sem):
    tm = h_ref.shape[0]
    bufs = (buf0, buf1)

    def row_copy(r, k):
        return pltpu.make_async_copy(ys_ref.at[pl.ds(dest_ref[0, 0, 2 * r + k], 1), :],
                                     bufs[k].at[pl.ds(r, 1), :], sem)

    def issue(r, carry):
        row_copy(r, 0).start()
        row_copy(r, 1).start()
        return carry

    lax.fori_loop(0, tm, issue, 0)

    def drain(r, carry):
        row_copy(r, 0).wait()
        row_copy(r, 1).wait()
        return carry

    lax.fori_loop(0, tm, drain, 0)

    wts = wts_ref[...]
    y = wts[:, 0:1] * buf0[...] + wts[:, 1:2] * buf1[...]
    out_ref[...] = h_ref[...] + gate_ref[0] * (_rms(y) * g2_ref[...])


def _moe_combine(ys, dest, wts, h, gate, g2, rows_per_batch, tm=256):
    m, d = h.shape
    nb = m // tm
    bpb = rows_per_batch // tm
    row = lambda w: pl.BlockSpec((tm, w), lambda i: (i, 0))
    return pl.pallas_call(
        _moe_combine_kernel,
        grid=(nb,),
        in_specs=[pl.BlockSpec((1, 1, 2 * tm), lambda i: (i, 0, 0), memory_space=pltpu.SMEM),
                  pl.BlockSpec(memory_space=pl.ANY),
                  row(LANES), row(d),
                  pl.BlockSpec((1, 1, d), lambda i: (i // bpb, 0, 0)),
                  pl.BlockSpec((1, d), lambda i: (0, 0))],
        out_specs=row(d),
        out_shape=jax.ShapeDtypeStruct((m, d), F32),
        scratch_shapes=[pltpu.VMEM((tm, d), F32), pltpu.VMEM((tm, d), F32),
                        pltpu.SemaphoreType.DMA(())],
        compiler_params=_cparams(("arbitrary",)),
        name="moe_combine",
    )(dest.reshape(nb, 1, 2 * tm), ys, wts, h, gate, g2)


def _moe(h, g, scale, shift, w_router, w_gu, w_down, gate, g2, rows_per_batch, tmb=512):
    m, d = h.shape
    w_router_pad = jnp.pad(w_router, ((0, 0), (0, LANES - N_EXPERTS)))
    x_packed, meta, wts, cnt = _moe_pre(h, g, scale, shift, w_router_pad, rows_per_batch)

    counts = cnt[0, :N_EXPERTS].astype(jnp.int32)
    padded = (counts + tmb - 1) // tmb * tmb
    pend = jnp.cumsum(padded)
    pstart = pend - padded
    dest = (pstart[meta[:, 0:2]] + meta[:, 2:4]).astype(jnp.int32).reshape(-1)
    n_rows = (2 * m + N_EXPERTS * (tmb - 1)) // tmb * tmb
    nb = n_rows // tmb
    blk_start = jnp.arange(nb, dtype=jnp.int32) * tmb
    blk_e = jnp.minimum(jnp.sum(pend[None, :] <= blk_start[:, None], axis=1),
                        N_EXPERTS - 1).astype(jnp.int32)
    n_used = (pend[-1:] // tmb).astype(jnp.int32)
    row_tok = jnp.zeros((n_rows,), jnp.int32).at[dest].set(jnp.arange(2 * m, dtype=jnp.int32) // 2)

    ys = _moe_ffn(x_packed, row_tok, blk_e, n_used, w_gu, w_down, tmb)
    return _moe_combine(ys, dest, wts, h, gate, g2, rows_per_batch)


def _swap_halves(w):
    half = w.shape[-1] // 2
    return jnp.concatenate([w[..., half:], w[..., :half]], axis=-1)


def _mla_weights(w_q_a, q_norm_g, w_q_b, w_kv_a, kv_norm_g, w_kv_b):
    ql = w_q_b.shape[0]
    kvl = w_kv_b.shape[0]
    wq = w_q_b.reshape(ql, MLA_HEADS, QK_NOPE + QK_ROPE)
    rope = wq[:, :, QK_NOPE:]
    pad_r = LANES - QK_ROPE
    wqm = jnp.concatenate([wq, jnp.zeros((ql, MLA_HEADS, QK_PAD - QK_NOPE - QK_ROPE), F32)], axis=-1)
    wqr = jnp.concatenate([_swap_halves(rope), jnp.zeros((ql, MLA_HEADS, pad_r), F32)], axis=-1)
    wkv = w_kv_b.reshape(kvl, MLA_HEADS, QK_NOPE + V_HEAD)
    kr = w_kv_a[:, kvl:]
    dm = w_kv_a.shape[0]
    wkr = jnp.concatenate([kr, jnp.zeros((dm, pad_r), F32)], axis=-1)
    wkrr = jnp.concatenate([_swap_halves(kr), jnp.zeros((dm, pad_r), F32)], axis=-1)
    bf = lambda a: a.astype(BF16)
    return (bf(w_q_a), q_norm_g.reshape(1, -1), bf(wqm.reshape(ql, -1)), bf(wqr.reshape(ql, -1)),
            bf(w_kv_a[:, :kvl]), bf(wkr), bf(wkrr), kv_norm_g.reshape(1, -1),
            bf(wkv[:, :, :QK_NOPE].reshape(kvl, -1)), bf(wkv[:, :, QK_NOPE:].reshape(kvl, -1)))


def kernel(x, c, positions, ada_w, ada_b, norm_g, hg_w_in, hg_lb_logits, hg_out_norm_g, hg_w_out, kv_src_norm_g, kv_src_ada_w, kv_src_ada_b, mla_w_kv_a, mla_kv_norm_g, mla_w_kv_b, mla_w_q_a, mla_q_norm_g, mla_w_q_b, mla_w_o, ffn_w_gu, ffn_w_down, moe_w_router, moe_w_gu, moe_w_down):
    batch, seq, d = x.shape
    m = batch * seq
    bf = lambda a: a.astype(BF16)

    c8 = jnp.pad(c, ((0, 8 - batch), (0, 0)))
    ada = _modulation(c8, ada_w.reshape(4, d, 3 * d), ada_b.reshape(4, 3 * d))[:, :batch]
    kvm = _modulation(c8, kv_src_ada_w[None], kv_src_ada_b[None])[0, :batch]

    def mods(idx):
        a = ada[idx]
        return [a[:, i * d:(i + 1) * d].reshape(batch, 1, d) for i in range(3)]

    vec = lambda a: a.reshape(1, -1)
    lb_all = jnp.cumsum(jax.nn.softmax(hg_lb_logits.astype(F32), axis=0), axis=0)

    h = x.reshape(m, d)

    shift, scale, gate = mods(0)
    q, f_logit, iv, gg = _hg_proj(h, vec(norm_g[0, 0, 0]), scale, shift, bf(hg_w_in[0]), seq)
    o, (ffn_gu_bf, ffn_down_bf, hg_out_bf, moe_down_bf) = _with_side(
        lambda side: _hgrn2(q, f_logit, iv, vec(lb_all[0]), batch, side=side),
        [ffn_w_gu[0], ffn_w_down[0], hg_w_out[0], moe_w_down[0]], _hgrn2_steps(m))
    h = _oproj(o, hg_out_bf, h, gate, vec(norm_g[0, 0, 1]), seq,
               gg=gg, gout=vec(hg_out_norm_g[0]))

    shift, scale, gate = mods(1)
    h_in = h
    h, (moe_gu_bf, mla_o_bf) = _with_side(
        lambda side: _ffn(h_in, vec(norm_g[0, 1, 0]), scale, shift, ffn_gu_bf, ffn_down_bf,
                          gate, vec(norm_g[0, 1, 1]), seq, side=side),
        [moe_w_gu[0], mla_w_o[0]], _ffn_steps(m, ffn_w_down.shape[1]))

    shift_k, scale_k = [kvm[:, i * d:(i + 1) * d].reshape(batch, 1, d) for i in range(2)]
    shift, scale, gate = mods(2)
    cos, sin = _rope_tables(positions.reshape(m, 1).astype(F32))
    wts = _mla_weights(mla_w_q_a[0], mla_q_norm_g[0], mla_w_q_b[0],
                       mla_w_kv_a, mla_kv_norm_g, mla_w_kv_b)
    qh, kh, vh = _mla_proj(h, vec(norm_g[1, 0, 0]), scale, shift,
                           vec(kv_src_norm_g), scale_k, shift_k, wts, cos, sin, seq)
    att, _ = _flash(qh, kh, vh, batch)
    h = _oproj(att, mla_o_bf, h, gate, vec(norm_g[1, 0, 1]), seq)

    shift, scale, gate = mods(3)
    h = _moe(h, vec(norm_g[1, 1, 0]), scale, shift, moe_w_router[0],
             moe_gu_bf, moe_down_bf, gate, vec(norm_g[1, 1, 1]), seq)
    return h.reshape(batch, seq, d)
```

```python
import functools

import jax
import jax.numpy as jnp
from jax import lax
from jax.experimental import pallas as pl
from jax.experimental.pallas import tpu as pltpu

F32 = jnp.float32
BF16 = jnp.bfloat16

EPS = 1e-6
LANES = 128
F32_SUBLANES = 8
BF16_SUBLANES = 16
SIDE_BLOCK_BYTES = 3 * 1024 * 1024

HG_HEADS = 16
HG_KDIM = 128
HG_CHUNK = 64
HG_SUB = 16
HG_EXP_CLAMP = 80.0

MLA_HEADS = 16
QK_NOPE = 128
QK_ROPE = 64
V_HEAD = 128
QK_PAD = 256
MLA_SCALE = (QK_NOPE + QK_ROPE) ** -0.5
Q_SCALE = MLA_SCALE * 1.4426950408889634
ROPE_THETA = 10000.0
FLASH_TQ = 2048
FLASH_TK = 1024

N_EXPERTS = 8
FFN_TF = 512

VMEM_LIMIT = 56 * 1024 * 1024


def _cparams(sem):
    return pltpu.CompilerParams(dimension_semantics=sem, vmem_limit_bytes=VMEM_LIMIT)


def _rms(x):
    return x * lax.rsqrt(jnp.mean(x * x, axis=-1, keepdims=True) + EPS)


def _sigmoid(x):
    return 1.0 / (1.0 + jnp.exp(-x))


def _dot(a, b):
    return jnp.dot(a, b, preferred_element_type=F32)


def _dot_nt(a, b):
    return lax.dot_general(a, b, (((1,), (1,)), ((), ())), preferred_element_type=F32)


def _dot_tn(a, b):
    return lax.dot_general(a, b, (((0,), (0,)), ((), ())), preferred_element_type=F32)


def _side_rows(w, n_steps):
    e, r, c = w.shape
    for rb in range(BF16_SUBLANES, r + 1, BF16_SUBLANES):
        if r % rb == 0 and e * (r // rb) <= n_steps:
            return rb if rb * c * 4 <= SIDE_BLOCK_BYTES else None
    return None


def _side_ok(weights, n_steps):
    return all(_side_rows(w, n_steps) is not None for w in weights)


def _side_plan(weights, n_steps, step_of):
    specs, blocks = [], []
    for w in weights:
        rb = _side_rows(w, n_steps)
        per = w.shape[1] // rb
        n_blk = w.shape[0] * per
        blocks.append(n_blk)

        def idx(*grid_ids, per=per, n_blk=n_blk):
            blk = jnp.minimum(step_of(*grid_ids), n_blk - 1)
            return (blk // per, blk % per, 0)

        specs.append(pl.BlockSpec((1, rb, w.shape[2]), idx))
    return specs, tuple(blocks)


def _side_cast(step_id, w_ins, w_outs, blocks):
    for w_in, w_out, n_blk in zip(w_ins, w_outs, blocks):
        @pl.when(step_id < n_blk)
        def _(w_in=w_in, w_out=w_out):
            w_out[...] = w_in[...].astype(w_out.dtype)


def _grid_step_id(n_axes):
    sid = pl.program_id(0)
    for ax in range(1, n_axes):
        sid = sid * pl.num_programs(ax) + pl.program_id(ax)
    return sid


def _with_side(run, weights, n_steps):
    w3 = tuple(w.reshape((1,) + w.shape) if w.ndim == 2 else w for w in weights)
    if _side_ok(w3, n_steps):
        res, copies = run(w3)
        return res, [cp.reshape(w.shape) for cp, w in zip(copies, weights)]
    res, _ = run(())
    return res, [w.astype(BF16) for w in weights]


def _mod_kernel(c_ref, w_ref, b_ref, o_ref):
    c = c_ref[...]
    sc = (c * _sigmoid(c)).astype(BF16)
    o_ref[0] = _dot(sc, w_ref[0].astype(BF16)) + b_ref[0]


def _modulation(c8, w, b, tn=1024):
    g, d, n = w.shape
    return pl.pallas_call(
        _mod_kernel,
        grid=(g, n // tn),
        in_specs=[pl.BlockSpec((8, d), lambda i, j: (0, 0)),
                  pl.BlockSpec((1, d, tn), lambda i, j: (i, 0, j)),
                  pl.BlockSpec((1, 1, tn), lambda i, j: (i, 0, j))],
        out_specs=pl.BlockSpec((1, 8, tn), lambda i, j: (i, 0, j)),
        out_shape=jax.ShapeDtypeStruct((g, 8, n), F32),
        compiler_params=_cparams(("arbitrary", "arbitrary")),
        name="modulation",
    )(c8, w, b.reshape(g, 1, n))


def _rope_kernel(pos_ref, inv_ref, cos_ref, sin_ref):
    ang = pos_ref[...] * inv_ref[...]
    lane = lax.broadcasted_iota(jnp.int32, ang.shape, 1)
    half = QK_ROPE // 2
    cos_ref[...] = jnp.where(lane < QK_ROPE, jnp.cos(ang), 0.0)
    sin_ref[...] = jnp.where(lane < half, -jnp.sin(ang),
                             jnp.where(lane < QK_ROPE, jnp.sin(ang), 0.0))


def _rope_tables(pos, tm=512):
    m = pos.shape[0]
    half = QK_ROPE // 2
    inv = 1.0 / (ROPE_THETA ** (jnp.arange(0, QK_ROPE, 2, dtype=F32) / QK_ROPE))
    inv128 = jnp.concatenate([inv, inv, jnp.zeros((LANES - 2 * half,), F32)]).reshape(1, LANES)
    return pl.pallas_call(
        _rope_kernel,
        grid=(m // tm,),
        in_specs=[pl.BlockSpec((tm, 1), lambda i: (i, 0)),
                  pl.BlockSpec((1, LANES), lambda i: (0, 0))],
        out_specs=[pl.BlockSpec((tm, LANES), lambda i: (i, 0))] * 2,
        out_shape=[jax.ShapeDtypeStruct((m, LANES), F32)] * 2,
        compiler_params=_cparams(("arbitrary",)),
        name="rope_tables",
    )(pos, inv128)


def _hg_proj_kernel(h_ref, g_ref, sc_ref, sh_ref, w_ref, q_ref, f_ref, i_ref, gg_ref, u_scr):
    j = pl.program_id(1)

    @pl.when(j == 0)
    def _():
        u = _rms(h_ref[...]) * g_ref[...] * (1.0 + sc_ref[0]) + sh_ref[0]
        u_scr[...] = u.astype(BF16)

    r = _dot(u_scr[...], w_ref[...])
    for idx, ref in enumerate((q_ref, f_ref, i_ref, gg_ref)):
        @pl.when(j == idx)
        def _(ref=ref):
            ref[...] = r.astype(ref.dtype)


def _hg_proj(h, g, scale, shift, w_in, rows_per_batch, tm=512):
    m, d = h.shape
    bpb = rows_per_batch // tm
    row = lambda i, j: (i, 0)
    mod = lambda i, j: (i // bpb, 0, 0)
    return pl.pallas_call(
        _hg_proj_kernel,
        grid=(m // tm, 4),
        in_specs=[pl.BlockSpec((tm, d), row),
                  pl.BlockSpec((1, d), lambda i, j: (0, 0)),
                  pl.BlockSpec((1, 1, d), mod),
                  pl.BlockSpec((1, 1, d), mod),
                  pl.BlockSpec((d, d), lambda i, j: (0, j))],
        out_specs=[pl.BlockSpec((tm, d), row)] * 4,
        out_shape=[jax.ShapeDtypeStruct((m, d), BF16), jax.ShapeDtypeStruct((m, d), F32),
                   jax.ShapeDtypeStruct((m, d), BF16), jax.ShapeDtypeStruct((m, d), BF16)],
        scratch_shapes=[pltpu.VMEM((tm, d), BF16)],
        compiler_params=_cparams(("arbitrary", "arbitrary")),
        name="hg_proj",
    )(h, g, scale, shift, w_in)


def _hgrn2_kernel(*refs, n_chunks, side_blocks):
    n_side = len(side_blocks)
    q_ref, f_ref, v_ref, lb_ref = refs[:4]
    o_ref = refs[4 + n_side]
    st_ref = refs[5 + 2 * n_side]
    c, sub = HG_CHUNK, HG_SUB
    n_sub = c // sub

    @pl.when(pl.program_id(2) == 0)
    def _():
        st_ref[...] = jnp.zeros_like(st_ref)

    _side_cast(_grid_step_id(3), refs[4:4 + n_side], refs[5 + n_side:5 + 2 * n_side], side_blocks)

    lb = lb_ref[...]
    row = lax.broadcasted_iota(jnp.int32, (c, c), 0)
    col = lax.broadcasted_iota(jnp.int32, (c, c), 1)
    tri = (row >= col).astype(BF16)
    row2 = lax.broadcasted_iota(jnp.int32, (c, 2 * c), 0)
    col2 = lax.broadcasted_iota(jnp.int32, (c, 2 * c), 1)
    c_bits, sub_bits = c.bit_length() - 1, sub.bit_length() - 1
    mask2 = ((col2 >> c_bits) == ((row2 >> sub_bits) & 1)) & ((col2 & (c - 1)) <= row2)

    chunks = [slice(ci * c, (ci + 1) * c) for ci in range(n_chunks)]
    ks, bs = [], []
    for sl in chunks:
        f = lb + (1.0 - lb) * _sigmoid(f_ref[sl, :])
        lf = jnp.log(f)
        lf_hi = lf.astype(BF16)
        lf_lo = (lf - lf_hi.astype(F32)).astype(BF16)
        ks.append(1.0 - f)
        bs.append(_dot(tri, lf_hi) + _dot(tri, lf_lo))
    a_fulls, qbs, kdecs, decays = [], [], [], []
    for sl, k, b in zip(chunks, ks, bs):
        q = q_ref[sl, :].astype(F32)
        b_last = b[c - 1:c, :]
        refs = [jnp.zeros_like(b_last)] + [b[j * sub - 1:j * sub, :] for j in range(1, n_sub)]
        rrow = jnp.concatenate([jnp.broadcast_to(r, (sub, r.shape[1])) for r in refs], axis=0)
        qe = q * jnp.exp(b - rrow)
        qbs.append((qe * jnp.exp(rrow)).astype(BF16))
        ke = jnp.concatenate(
            [k * jnp.exp(jnp.minimum(r - b, HG_EXP_CLAMP)) for r in refs], axis=0)
        a_fulls.append(_dot_nt(qe.astype(BF16), ke.astype(BF16)))
        kdecs.append((k * jnp.exp(b_last - b)).astype(BF16))
        decays.append(jnp.exp(b_last))
    o_intras, incs = [], []
    for sl, a_full, kdec in zip(chunks, a_fulls, kdecs):
        v = v_ref[sl, :]
        a2 = jnp.concatenate(
            [a_full[j * sub:(j + 1) * sub, (j // 2) * 2 * c:(j // 2 + 1) * 2 * c]
             for j in range(n_sub)], axis=0)
        a2 = jnp.where(mask2, a2, 0.0).astype(BF16)
        o_intras.append(_dot(a2, jnp.concatenate([v, v], axis=0)))
        incs.append(_dot_tn(v, kdec))
    st = st_ref[...]
    for sl, qb, o_intra, inc, decay in zip(chunks, qbs, o_intras, incs, decays):
        o_ref[sl, :] = o_intra + _dot_nt(qb, st.astype(BF16))
        st = st * decay + inc
    st_ref[...] = st


def _hgrn2_steps(m, tb=512):
    return HG_HEADS * (m // tb)


def _hgrn2(q, f_logit, v, lb, batch, side=(), tb=512):
    m, d = q.shape
    s = m // batch
    nt = s // tb
    blk = lambda b, h, t: (b * nt + t, h)
    side_specs, side_blocks = _side_plan(
        side, batch * HG_HEADS * nt, lambda b, h, t: (b * HG_HEADS + h) * nt + t)
    outs = pl.pallas_call(
        functools.partial(_hgrn2_kernel, n_chunks=tb // HG_CHUNK, side_blocks=side_blocks),
        grid=(batch, HG_HEADS, nt),
        in_specs=[pl.BlockSpec((tb, HG_KDIM), blk)] * 3
        + [pl.BlockSpec((1, HG_KDIM), lambda b, h, t: (0, h))] + side_specs,
        out_specs=[pl.BlockSpec((tb, HG_KDIM), blk)] + side_specs,
        out_shape=[jax.ShapeDtypeStruct((m, d), F32)]
        + [jax.ShapeDtypeStruct(w.shape, BF16) for w in side],
        scratch_shapes=[pltpu.VMEM((HG_KDIM, HG_KDIM), F32)],
        compiler_params=_cparams(("arbitrary", "arbitrary", "arbitrary")),
        name="hgrn2_recurrence",
    )(q, f_logit, v, lb, *side)
    return outs[0], list(outs[1:])


def _oproj_kernel(*refs, gated):
    if gated:
        o_ref, gg_ref, gout_ref, w_ref, h_ref, gate_ref, g2_ref, out_ref = refs
        gg = gg_ref[...].astype(F32)
        x = _rms(o_ref[...]) * gout_ref[...] * (gg * _sigmoid(gg))
    else:
        o_ref, w_ref, h_ref, gate_ref, g2_ref, out_ref = refs
        x = o_ref[...]
    y = _dot(x.astype(BF16), w_ref[...])
    out_ref[...] = h_ref[...] + gate_ref[0] * (_rms(y) * g2_ref[...])


def _oproj(o, w, h, gate, g2, rows_per_batch, gg=None, gout=None, tm=512):
    m, d = h.shape
    bpb = rows_per_batch // tm
    row = pl.BlockSpec((tm, d), lambda i: (i, 0))
    vec = pl.BlockSpec((1, d), lambda i: (0, 0))
    mod = pl.BlockSpec((1, 1, d), lambda i: (i // bpb, 0, 0))
    wsp = pl.BlockSpec(w.shape, lambda i: (0, 0))
    gated = gg is not None
    if gated:
        args, specs = (o, gg, gout, w, h, gate, g2), [row, row, vec, wsp, row, mod, vec]
    else:
        args, specs = (o, w, h, gate, g2), [row, wsp, row, mod, vec]
    return pl.pallas_call(
        functools.partial(_oproj_kernel, gated=gated),
        grid=(m // tm,),
        in_specs=specs,
        out_specs=row,
        out_shape=jax.ShapeDtypeStruct((m, d), F32),
        compiler_params=_cparams(("arbitrary",)),
        name="hg_out_proj" if gated else "mla_out_proj",
    )(*args)


def _ffn_kernel(*refs, side_blocks):
    n_side = len(side_blocks)
    h_ref, g_ref, sc_ref, sh_ref, wg_ref, wu_ref, wd_ref, gate_ref, g2_ref = refs[:9]
    out_ref = refs[9 + n_side]
    u_scr, acc_scr = refs[10 + 2 * n_side:]
    j = pl.program_id(1)

    @pl.when(j == 0)
    def _():
        u = _rms(h_ref[...]) * g_ref[...] * (1.0 + sc_ref[0]) + sh_ref[0]
        u_scr[...] = u.astype(BF16)
        acc_scr[...] = jnp.zeros_like(acc_scr)

    _side_cast(_grid_step_id(2), refs[9:9 + n_side], refs[10 + n_side:10 + 2 * n_side], side_blocks)
    u = u_scr[...]
    gt = _dot(u, wg_ref[...])
    up = _dot(u, wu_ref[...])
    a = (gt * _sigmoid(gt) * up).astype(BF16)
    acc_scr[...] += _dot(a, wd_ref[...])

    @pl.when(j == pl.num_programs(1) - 1)
    def _():
        out_ref[...] = h_ref[...] + gate_ref[0] * (_rms(acc_scr[...]) * g2_ref[...])


def _ffn_steps(m, ff, tm=512, tf=FFN_TF):
    return (m // tm) * (ff // tf)


def _ffn(h, g, scale, shift, w_gu, w_down, gate, g2, rows_per_batch, side=(), tm=512, tf=FFN_TF):
    m, d = h.shape
    ff = w_down.shape[0]
    nf = ff // tf
    bpb = rows_per_batch // tm
    row = pl.BlockSpec((tm, d), lambda i, j: (i, 0))
    vec = pl.BlockSpec((1, d), lambda i, j: (0, 0))
    mod = pl.BlockSpec((1, 1, d), lambda i, j: (i // bpb, 0, 0))
    side_specs, side_blocks = _side_plan(side, (m // tm) * nf, lambda i, j: i * nf + j)
    outs = pl.pallas_call(
        functools.partial(_ffn_kernel, side_blocks=side_blocks),
        grid=(m // tm, nf),
        in_specs=[row, vec, mod, mod,
                  pl.BlockSpec((d, tf), lambda i, j: (0, j)),
                  pl.BlockSpec((d, tf), lambda i, j: (0, nf + j)),
                  pl.BlockSpec((tf, d), lambda i, j: (j, 0)),
                  mod, vec] + side_specs,
        out_specs=[row] + side_specs,
        out_shape=[jax.ShapeDtypeStruct((m, d), F32)]
        + [jax.ShapeDtypeStruct(w.shape, BF16) for w in side],
        scratch_shapes=[pltpu.VMEM((tm, d), BF16), pltpu.VMEM((tm, d), F32)],
        compiler_params=_cparams(("arbitrary", "arbitrary")),
        name="dense_ffn",
    )(h, g, scale, shift, w_gu, w_gu, w_down, gate, g2, *side)
    return outs[0], list(outs[1:])


def _mla_proj_kernel(h_ref, gq_ref, scq_ref, shq_ref, gk_ref, sck_ref, shk_ref,
                     wqa_ref, qg_ref, wqm_ref, wqr_ref,
                     wkva_ref, wkr_ref, wkrr_ref, kg_ref, wk_ref, wv_ref,
                     cos_ref, sin_ref, q_ref, k_ref, v_ref):
    xhat = _rms(h_ref[...])
    uq = (xhat * gq_ref[...] * (1.0 + scq_ref[0]) + shq_ref[0]).astype(BF16)
    xk = (xhat * gk_ref[...] * (1.0 + sck_ref[0]) + shk_ref[0]).astype(BF16)
    cos = cos_ref[...]
    sin = sin_ref[...]
    qn = (_rms(_dot(uq, wqa_ref[...])) * qg_ref[...]).astype(BF16)
    cn = (_rms(_dot(xk, wkva_ref[...])) * kg_ref[...]).astype(BF16)
    krope = (_dot(xk, wkr_ref[...]) * cos + _dot(xk, wkrr_ref[...]) * sin).astype(BF16)
    ones = jnp.ones((cn.shape[0], V_HEAD), BF16)
    for hp in range(MLA_HEADS // 2):
        ps = slice(hp * 2 * LANES, (hp + 1) * 2 * LANES)
        qr2 = _dot(qn, wqr_ref[:, ps]) * Q_SCALE
        kn2 = _dot(cn, wk_ref[:, ps]).astype(BF16)
        v2 = _dot(cn, wv_ref[:, ps]).astype(BF16)
        for sub in range(2):
            hd = 2 * hp + sub
            ls = slice(sub * LANES, (sub + 1) * LANES)
            qm = _dot(qn, wqm_ref[:, hd * QK_PAD:(hd + 1) * QK_PAD]) * Q_SCALE
            q_ref[:, hd * QK_PAD:hd * QK_PAD + QK_NOPE] = qm[:, :QK_NOPE].astype(BF16)
            q_ref[:, hd * QK_PAD + QK_NOPE:(hd + 1) * QK_PAD] = (
                qm[:, QK_NOPE:] * cos + qr2[:, ls] * sin).astype(BF16)
            k_ref[:, hd * QK_PAD:hd * QK_PAD + QK_NOPE] = kn2[:, ls]
            k_ref[:, hd * QK_PAD + QK_NOPE:(hd + 1) * QK_PAD] = krope
            v_ref[:, 2 * hd * V_HEAD:(2 * hd + 1) * V_HEAD] = v2[:, ls]
            v_ref[:, (2 * hd + 1) * V_HEAD:(2 * hd + 2) * V_HEAD] = ones


def _mla_proj(h, gq, scq, shq, gk, sck, shk, wts, cos, sin, rows_per_batch, tm=256):
    m, d = h.shape
    bpb = rows_per_batch // tm
    row = lambda w: pl.BlockSpec((tm, w), lambda i: (i, 0))
    vec = lambda w: pl.BlockSpec((1, w), lambda i: (0, 0))
    mod = pl.BlockSpec((1, 1, d), lambda i: (i // bpb, 0, 0))
    full = lambda a: pl.BlockSpec(a.shape, lambda i: (0, 0))
    wqa, qg, wqm, wqr, wkva, wkr, wkrr, kg, wk, wv = wts
    return pl.pallas_call(
        _mla_proj_kernel,
        grid=(m // tm,),
        in_specs=[row(d), vec(d), mod, mod, vec(d), mod, mod,
                  full(wqa), full(qg), full(wqm), full(wqr),
                  full(wkva), full(wkr), full(wkrr), full(kg), full(wk), full(wv),
                  row(LANES), row(LANES)],
        out_specs=[row(MLA_HEADS * QK_PAD), row(MLA_HEADS * QK_PAD), row(MLA_HEADS * 2 * V_HEAD)],
        out_shape=[jax.ShapeDtypeStruct((m, MLA_HEADS * QK_PAD), BF16),
                   jax.ShapeDtypeStruct((m, MLA_HEADS * QK_PAD), BF16),
                   jax.ShapeDtypeStruct((m, MLA_HEADS * 2 * V_HEAD), BF16)],
        compiler_params=_cparams(("arbitrary",)),
        name="mla_proj",
    )(h, gq, scq, shq, gk, sck, shk, wqa, qg, wqm, wqr, wkva, wkr, wkrr, kg, wk, wv, cos, sin)


def _flash_kernel(*refs, tq, tk, slab, side_blocks):
    n_side = len(side_blocks)
    q_ref, k_ref, v_ref = refs[:3]
    o_ref = refs[3 + n_side]
    s_a, s_b, p_scr, m_scr, al_scr, acc_scr = refs[4 + 2 * n_side:]
    qi = pl.program_id(2)
    _side_cast(_grid_step_id(3), refs[3:3 + n_side], refs[4 + n_side:4 + 2 * n_side], side_blocks)

    s_bufs = (s_a, s_b)
    m_scr[...] = jnp.full_like(m_scr, -jnp.inf)
    acc_scr[...] = jnp.zeros_like(acc_scr)
    n_col = tk // LANES
    n_diag = tq // tk

    def kv_rows(j):
        return pl.ds(pl.multiple_of(j * tk, tk), tk)

    def scores(j, dst, row0):
        dst[row0:, :] = _dot_nt(q_ref[row0:, :], k_ref[kv_rows(j), :])

    def consume(j, src, row0, diag):
        for r0 in range(row0, tq, slab):
            rows = slice(r0, r0 + slab)
            cols = [src[rows, cb * LANES:(cb + 1) * LANES] for cb in range(n_col)]
            if diag and r0 < row0 + tk:
                rowp = (r0 - row0) + lax.broadcasted_iota(jnp.int32, (slab, LANES), 0)
                lane = lax.broadcasted_iota(jnp.int32, (slab, LANES), 1)
                cols = [jnp.where(cb * LANES + lane <= rowp, cols[cb], -jnp.inf)
                        for cb in range(n_col)]
            mx = functools.reduce(jnp.maximum, cols)
            m_old = m_scr[rows, :]
            m_new = jnp.maximum(m_old, jnp.max(mx, axis=-1, keepdims=True))
            al_scr[rows, :] = jnp.exp2(m_old - m_new)
            m_scr[rows, :] = m_new
            for cb in range(n_col):
                p_scr[rows, cb * LANES:(cb + 1) * LANES] = jnp.exp2(cols[cb] - m_new).astype(BF16)
        pv = _dot(p_scr[row0:, :], v_ref[kv_rows(j), :])
        al = al_scr[row0:, :]
        acc_scr[row0:, :V_HEAD] = acc_scr[row0:, :V_HEAD] * al + pv[:, :V_HEAD]
        acc_scr[row0:, V_HEAD:] = acc_scr[row0:, V_HEAD:] * al + pv[:, V_HEAD:]

    n_full = qi * n_diag
    scores(0, s_a, 0)

    def pair(pi, carry):
        for u in range(2):
            scores(2 * pi + u + 1, s_bufs[1 - u], 0)
            consume(2 * pi + u, s_bufs[u], 0, False)
        return carry

    lax.fori_loop(0, n_full // 2, pair, 0)
    for dg in range(n_diag):
        if dg + 1 < n_diag:
            scores(n_full + dg + 1, s_bufs[(dg + 1) % 2], (dg + 1) * tk)
        consume(n_full + dg, s_bufs[dg % 2], dg * tk, True)

    o_ref[...] = (acc_scr[:, :V_HEAD] / acc_scr[:, V_HEAD:]).astype(o_ref.dtype)


def _flash_tiles(seq):
    tq = min(FLASH_TQ, seq)
    return tq, min(FLASH_TK, tq // 2)


def _flash(q, k, v, batch, side=(), slab=64):
    m = q.shape[0]
    s = m // batch
    tq, tk = _flash_tiles(s)
    nq = s // tq
    side_specs, side_blocks = _side_plan(
        side, batch * MLA_HEADS * nq, lambda b, h, i: (b * MLA_HEADS + h) * nq + i)
    outs = pl.pallas_call(
        functools.partial(_flash_kernel, tq=tq, tk=tk, slab=slab, side_blocks=side_blocks),
        grid=(batch, MLA_HEADS, nq),
        in_specs=[pl.BlockSpec((tq, QK_PAD), lambda b, h, i: (b * nq + i, h)),
                  pl.BlockSpec((s, QK_PAD), lambda b, h, i: (b, h)),
                  pl.BlockSpec((s, 2 * V_HEAD), lambda b, h, i: (b, h))] + side_specs,
        out_specs=[pl.BlockSpec((tq, V_HEAD), lambda b, h, i: (b * nq + i, h))] + side_specs,
        out_shape=[jax.ShapeDtypeStruct((m, MLA_HEADS * V_HEAD), BF16)]
        + [jax.ShapeDtypeStruct(w.shape, BF16) for w in side],
        scratch_shapes=[pltpu.VMEM((tq, tk), F32), pltpu.VMEM((tq, tk), F32),
                        pltpu.VMEM((tq, tk), BF16), pltpu.VMEM((tq, LANES), F32),
                        pltpu.VMEM((tq, LANES), F32), pltpu.VMEM((tq, 2 * V_HEAD), F32)],
        compiler_params=_cparams(("arbitrary", "arbitrary", "arbitrary")),
        name="mla_flash",
    )(q, k, v, *side)
    return outs[0], list(outs[1:])


def _split3(x):
    hi = x.astype(BF16)
    lo = (x - hi.astype(F32)).astype(BF16)
    return hi, lo


def _moe_pre_kernel(h_ref, g_ref, sc_ref, sh_ref, wr_ref, xs_ref, meta_ref, wts_ref, cnt_ref,
                    run_scr):
    tm, d = h_ref.shape

    @pl.when(pl.program_id(0) == 0)
    def _():
        run_scr[...] = jnp.zeros_like(run_scr)

    u = _rms(h_ref[...]) * g_ref[...] * (1.0 + sc_ref[0]) + sh_ref[0]
    half = d // 2
    lo = pltpu.bitcast(u[:, :half].astype(BF16).astype(F32), jnp.uint32) >> 16
    hi = pltpu.bitcast(u[:, half:].astype(BF16).astype(F32), jnp.uint32)
    xs_ref[...] = hi | lo

    u_hi, u_lo = _split3(u)
    w_hi, w_lo = _split3(wr_ref[...])
    logits = _dot(u_hi, w_hi) + (_dot(u_hi, w_lo) + _dot(u_lo, w_hi))
    lane = lax.broadcasted_iota(jnp.int32, logits.shape, 1)
    lg = jnp.where(lane < N_EXPERTS, logits, -jnp.inf)
    m1 = jnp.max(lg, axis=-1, keepdims=True)
    i1 = jnp.min(jnp.where(lg == m1, lane, LANES), axis=-1, keepdims=True)
    lg2 = jnp.where(lane == i1, -jnp.inf, lg)
    m2 = jnp.max(lg2, axis=-1, keepdims=True)
    i2 = jnp.min(jnp.where(lg2 == m2, lane, LANES), axis=-1, keepdims=True)
    e = jnp.exp(m2 - m1)
    w1 = 1.0 / (1.0 + e)
    w2 = e * w1

    oh1 = lane == i1
    oh2 = lane == i2
    cnt = oh1.astype(F32) + oh2.astype(F32)
    r = lax.broadcasted_iota(jnp.int32, (tm, tm), 0)
    c = lax.broadcasted_iota(jnp.int32, (tm, tm), 1)
    strict = (r > c).astype(BF16)
    prefix = _dot(strict, cnt.astype(BF16)) + run_scr[...]
    rank1 = jnp.sum(jnp.where(oh1, prefix, 0.0), axis=-1, keepdims=True).astype(jnp.int32)
    rank2 = jnp.sum(jnp.where(oh2, prefix, 0.0), axis=-1, keepdims=True).astype(jnp.int32)
    run_scr[...] += jnp.sum(cnt, axis=0, keepdims=True)

    meta_ref[...] = jnp.where(lane == 0, i1, jnp.where(lane == 1, i2,
                              jnp.where(lane == 2, rank1, jnp.where(lane == 3, rank2, 0))))
    wts_ref[...] = jnp.where(lane == 0, w1, jnp.where(lane == 1, w2, 0.0))
    cnt_ref[...] = run_scr[...]


def _moe_pre(h, g, scale, shift, w_router_pad, rows_per_batch, tm=512):
    m, d = h.shape
    bpb = rows_per_batch // tm
    row = lambda w: pl.BlockSpec((tm, w), lambda i: (i, 0))
    vec = lambda w: pl.BlockSpec((1, w), lambda i: (0, 0))
    mod = pl.BlockSpec((1, 1, d), lambda i: (i // bpb, 0, 0))
    return pl.pallas_call(
        _moe_pre_kernel,
        grid=(m // tm,),
        in_specs=[row(d), vec(d), mod, mod, pl.BlockSpec((d, LANES), lambda i: (0, 0))],
        out_specs=[row(d // 2), row(LANES), row(LANES), vec(LANES)],
        out_shape=[jax.ShapeDtypeStruct((m, d // 2), jnp.uint32),
                   jax.ShapeDtypeStruct((m, LANES), jnp.int32),
                   jax.ShapeDtypeStruct((m, LANES), F32),
                   jax.ShapeDtypeStruct((1, LANES), F32)],
        scratch_shapes=[pltpu.VMEM((1, LANES), F32)],
        compiler_params=_cparams(("arbitrary",)),
        name="moe_route",
    )(h, g, scale, shift, w_router_pad)


ROW_COPY_UNROLL = 4


def _issue_row_copies(n_rows, copies_of_row):
    def trip(t, carry):
        n = 0
        for u in range(ROW_COPY_UNROLL):
            for cp in copies_of_row(t * ROW_COPY_UNROLL + u):
                cp.start(priority=n % 2)
                n += 1
        return carry
    lax.fori_loop(0, n_rows // ROW_COPY_UNROLL, trip, 0)


def _moe_dispatch_kernel(dest_ref, x_ref, xs_in_ref, xs_ref, sem):
    del xs_in_ref
    tm = x_ref.shape[0]

    def copies_of_row(r):
        return [pltpu.make_async_copy(x_ref.at[pl.ds(r, 1), :],
                                      xs_ref.at[pl.ds(dest_ref[0, 0, 2 * r + k], 1), :], sem)
                for k in range(2)]

    _issue_row_copies(tm, copies_of_row)
    for _ in range(2):
        pltpu.make_async_copy(x_ref, x_ref, sem).wait()


def _moe_dispatch(x, dest, xs_init, tm=256):
    m, w = x.shape
    nb = m // tm
    return pl.pallas_call(
        _moe_dispatch_kernel,
        grid=(nb,),
        in_specs=[pl.BlockSpec((1, 1, 2 * tm), lambda i: (i, 0, 0), memory_space=pltpu.SMEM),
                  pl.BlockSpec((tm, w), lambda i: (i, 0)),
                  pl.BlockSpec(memory_space=pl.ANY)],
        out_specs=pl.BlockSpec(memory_space=pl.ANY),
        out_shape=jax.ShapeDtypeStruct(xs_init.shape, xs_init.dtype),
        scratch_shapes=[pltpu.SemaphoreType.DMA(())],
        input_output_aliases={2: 0},
        compiler_params=_cparams(("arbitrary",)),
        name="moe_dispatch",
    )(dest.reshape(nb, 1, 2 * tm), x, xs_init)


def _moe_ffn_kernel(blk_e_ref, n_used_ref, xs_ref, wg_ref, wu_ref, wd_ref, y_ref, u_scr):
    i = pl.program_id(0)
    j = pl.program_id(1)

    @pl.when(i < n_used_ref[0])
    def _():
        @pl.when(j == 0)
        def _():
            w = xs_ref[...]
            half = w.shape[1]
            u_scr[:, :half] = pltpu.bitcast(w << 16, F32).astype(BF16)
            u_scr[:, half:] = pltpu.bitcast(w & jnp.uint32(0xFFFF0000), F32).astype(BF16)
            y_ref[...] = jnp.zeros_like(y_ref)

        u = u_scr[...]
        gt = _dot(u, wg_ref[0])
        up = _dot(u, wu_ref[0])
        a = (gt * _sigmoid(gt) * up).astype(BF16)
        y_ref[...] += _dot(a, wd_ref[0])

    @pl.when((i >= n_used_ref[0]) & (j == 0))
    def _():
        y_ref[...] = jnp.zeros_like(y_ref)


def _moe_ffn(xs, blk_e, n_used, w_gu, w_down, tmb, tf=FFN_TF):
    n_rows, half = xs.shape
    d = 2 * half
    ff = w_down.shape[1]
    nf = ff // tf
    nb = n_rows // tmb

    def blk(i, n_used_ref):
        return jnp.minimum(i, n_used_ref[0] - 1)

    def fidx(i, j, n_used_ref):
        return jnp.where(i < n_used_ref[0], j, nf - 1)

    grid_spec = pltpu.PrefetchScalarGridSpec(
        num_scalar_prefetch=2,
        grid=(nb, nf),
        in_specs=[
            pl.BlockSpec((tmb, half), lambda i, j, be, nu: (blk(i, nu), 0)),
            pl.BlockSpec((1, d, tf), lambda i, j, be, nu: (be[blk(i, nu)], 0, fidx(i, j, nu))),
            pl.BlockSpec((1, d, tf), lambda i, j, be, nu: (be[blk(i, nu)], 0, nf + fidx(i, j, nu))),
            pl.BlockSpec((1, tf, d), lambda i, j, be, nu: (be[blk(i, nu)], fidx(i, j, nu), 0)),
        ],
        out_specs=pl.BlockSpec((tmb, d), lambda i, j, be, nu: (i, 0)),
        scratch_shapes=[pltpu.VMEM((tmb, d), BF16)],
    )
    return pl.pallas_call(
        _moe_ffn_kernel,
        grid_spec=grid_spec,
        out_shape=jax.ShapeDtypeStruct((n_rows, d), F32),
        compiler_params=_cparams(("arbitrary", "arbitrary")),
        name="moe_grouped_ffn",
    )(blk_e, n_used, xs, w_gu, w_gu, w_down)


def _moe_combine_kernel(dest_ref, ys_ref, wts_ref, h_ref, gate_ref, g2_ref, out_ref,
                        buf0, buf1, sem):
    tm = h_ref.shape[0]
    bufs = (buf0, buf1)

    def copies_of_row(r):
        return [pltpu.make_async_copy(ys_ref.at[pl.ds(dest_ref[0, 0, 2 * r + k], 1), :],
                                      bufs[k].at[pl.ds(r, 1), :], sem)
                for k in range(2)]

    _issue_row_copies(tm, copies_of_row)
    for k in range(2):
        pltpu.make_async_copy(bufs[k], bufs[k], sem).wait()

    wts = wts_ref[...]
    y = wts[:, 0:1] * buf0[...] + wts[:, 1:2] * buf1[...]
    out_ref[...] = h_ref[...] + gate_ref[0] * (_rms(y) * g2_ref[...])


def _moe_combine(ys, dest, wts, h, gate, g2, rows_per_batch, tm=256):
    m, d = h.shape
    nb = m // tm
    bpb = rows_per_batch // tm
    row = lambda w: pl.BlockSpec((tm, w), lambda i: (i, 0))
    return pl.pallas_call(
        _moe_combine_kernel,
        grid=(nb,),
        in_specs=[pl.BlockSpec((1, 1, 2 * tm), lambda i: (i, 0, 0), memory_space=pltpu.SMEM),
                  pl.BlockSpec(memory_space=pl.ANY),
                  row(LANES), row(d),
                  pl.BlockSpec((1, 1, d), lambda i: (i // bpb, 0, 0)),
                  pl.BlockSpec((1, d), lambda i: (0, 0))],
        out_specs=row(d),
        out_shape=jax.ShapeDtypeStruct((m, d), F32),
        scratch_shapes=[pltpu.VMEM((tm, d), F32), pltpu.VMEM((tm, d), F32),
                        pltpu.SemaphoreType.DMA(())],
        compiler_params=_cparams(("arbitrary",)),
        name="moe_combine",
    )(dest.reshape(nb, 1, 2 * tm), ys, wts, h, gate, g2)


def _moe(h, g, scale, shift, w_router, w_gu, w_down, gate, g2, rows_per_batch, tmb=512):
    m, d = h.shape
    w_router_pad = jnp.pad(w_router, ((0, 0), (0, LANES - N_EXPERTS)))
    x_packed, meta, wts, cnt = _moe_pre(h, g, scale, shift, w_router_pad, rows_per_batch)

    counts = cnt[0, :N_EXPERTS].astype(jnp.int32)
    padded = (counts + tmb - 1) // tmb * tmb
    pend = jnp.cumsum(padded)
    pstart = pend - padded
    dest = (pstart[meta[:, 0:2]] + meta[:, 2:4]).astype(jnp.int32).reshape(-1)
    n_rows = (2 * m + N_EXPERTS * (tmb - 1)) // tmb * tmb
    nb = n_rows // tmb
    blk_start = jnp.arange(nb, dtype=jnp.int32) * tmb
    blk_e = jnp.minimum(jnp.sum(pend[None, :] <= blk_start[:, None], axis=1),
                        N_EXPERTS - 1).astype(jnp.int32)
    n_used = (pend[-1:] // tmb).astype(jnp.int32)

    xs = _moe_dispatch(x_packed, dest, jnp.zeros((n_rows, d // 2), jnp.uint32))
    ys = _moe_ffn(xs, blk_e, n_used, w_gu, w_down, tmb)
    return _moe_combine(ys, dest, wts, h, gate, g2, rows_per_batch)


def _swap_halves(w):
    half = w.shape[-1] // 2
    return jnp.concatenate([w[..., half:], w[..., :half]], axis=-1)


def _mla_weights(w_q_a, q_norm_g, w_q_b, w_kv_a, kv_norm_g, w_kv_b):
    ql = w_q_b.shape[0]
    kvl = w_kv_b.shape[0]
    wq = w_q_b.reshape(ql, MLA_HEADS, QK_NOPE + QK_ROPE)
    rope = wq[:, :, QK_NOPE:]
    pad_r = LANES - QK_ROPE
    wqm = jnp.concatenate([wq, jnp.zeros((ql, MLA_HEADS, QK_PAD - QK_NOPE - QK_ROPE), F32)], axis=-1)
    wqr = jnp.concatenate([_swap_halves(rope), jnp.zeros((ql, MLA_HEADS, pad_r), F32)], axis=-1)
    wkv = w_kv_b.reshape(kvl, MLA_HEADS, QK_NOPE + V_HEAD)
    kr = w_kv_a[:, kvl:]
    dm = w_kv_a.shape[0]
    wkr = jnp.concatenate([kr, jnp.zeros((dm, pad_r), F32)], axis=-1)
    wkrr = jnp.concatenate([_swap_halves(kr), jnp.zeros((dm, pad_r), F32)], axis=-1)
    bf = lambda a: a.astype(BF16)
    return (bf(w_q_a), q_norm_g.reshape(1, -1), bf(wqm.reshape(ql, -1)), bf(wqr.reshape(ql, -1)),
            bf(w_kv_a[:, :kvl]), bf(wkr), bf(wkrr), kv_norm_g.reshape(1, -1),
            bf(wkv[:, :, :QK_NOPE].reshape(kvl, -1)), bf(wkv[:, :, QK_NOPE:].reshape(kvl, -1)))


def kernel(x, c, positions, ada_w, ada_b, norm_g, hg_w_in, hg_lb_logits, hg_out_norm_g, hg_w_out, kv_src_norm_g, kv_src_ada_w, kv_src_ada_b, mla_w_kv_a, mla_kv_norm_g, mla_w_kv_b, mla_w_q_a, mla_q_norm_g, mla_w_q_b, mla_w_o, ffn_w_gu, ffn_w_down, moe_w_router, moe_w_gu, moe_w_down):
    batch, seq, d = x.shape
    m = batch * seq
    bf = lambda a: a.astype(BF16)

    c8 = jnp.pad(c, ((0, 8 - batch), (0, 0)))
    ada = _modulation(c8, ada_w.reshape(4, d, 3 * d), ada_b.reshape(4, 3 * d))[:, :batch]
    kvm = _modulation(c8, kv_src_ada_w[None], kv_src_ada_b[None])[0, :batch]

    def mods(idx):
        a = ada[idx]
        return [a[:, i * d:(i + 1) * d].reshape(batch, 1, d) for i in range(3)]

    vec = lambda a: a.reshape(1, -1)
    lb_all = jnp.cumsum(jax.nn.softmax(hg_lb_logits.astype(F32), axis=0), axis=0)

    h = x.reshape(m, d)

    shift, scale, gate = mods(0)
    q, f_logit, iv, gg = _hg_proj(h, vec(norm_g[0, 0, 0]), scale, shift, bf(hg_w_in[0]), seq)
    o, (ffn_gu_bf, ffn_down_bf, hg_out_bf, moe_down_bf) = _with_side(
        lambda side: _hgrn2(q, f_logit, iv, vec(lb_all[0]), batch, side=side),
        [ffn_w_gu[0], ffn_w_down[0], hg_w_out[0], moe_w_down[0]], _hgrn2_steps(m))
    h = _oproj(o, hg_out_bf, h, gate, vec(norm_g[0, 0, 1]), seq,
               gg=gg, gout=vec(hg_out_norm_g[0]))

    shift, scale, gate = mods(1)
    h_in = h
    h, (moe_gu_bf, mla_o_bf) = _with_side(
        lambda side: _ffn(h_in, vec(norm_g[0, 1, 0]), scale, shift, ffn_gu_bf, ffn_down_bf,
                          gate, vec(norm_g[0, 1, 1]), seq, side=side),
        [moe_w_gu[0], mla_w_o[0]], _ffn_steps(m, ffn_w_down.shape[1]))

    shift_k, scale_k = [kvm[:, i * d:(i + 1) * d].reshape(batch, 1, d) for i in range(2)]
    shift, scale, gate = mods(2)
    cos, sin = _rope_tables(positions.reshape(m, 1).astype(F32))
    wts = _mla_weights(mla_w_q_a[0], mla_q_norm_g[0], mla_w_q_b[0],
                       mla_w_kv_a, mla_kv_norm_g, mla_w_kv_b)
    qh, kh, vh = _mla_proj(h, vec(norm_g[1, 0, 0]), scale, shift,
                           vec(kv_src_norm_g), scale_k, shift_k, wts, cos, sin, seq)
    att, _ = _flash(qh, kh, vh, batch)
    h = _oproj(att, mla_o_bf, h, gate, vec(norm_g[1, 0, 1]), seq)

    shift, scale, gate = mods(3)
    h = _moe(h, vec(norm_g[1, 1, 0]), scale, shift, moe_w_router[0],
             moe_gu_bf, moe_down_bf, gate, vec(norm_g[1, 1, 1]), seq)
    return h.reshape(batch, seq, d)
```

```python
import functools

import jax
import jax.numpy as jnp
from jax import lax
from jax.experimental import pallas as pl
from jax.experimental.pallas import tpu as pltpu

F32 = jnp.float32
BF16 = jnp.bfloat16

EPS = 1e-6
LANES = 128
F32_SUBLANES = 8
BF16_SUBLANES = 16
SIDE_BLOCK_BYTES = 3 * 1024 * 1024

HG_HEADS = 16
HG_KDIM = 128
HG_CHUNK = 64
HG_SUB = 16
HG_EXP2_CLAMP = 115.0
HG_HEADS_PER_STEP = 2

MLA_HEADS = 16
QK_NOPE = 128
QK_ROPE = 64
V_HEAD = 128
QK_PAD = 256
MLA_SCALE = (QK_NOPE + QK_ROPE) ** -0.5
Q_SCALE = MLA_SCALE * 1.4426950408889634
ROPE_THETA = 10000.0
FLASH_TQ = 2048
FLASH_TK = 1024

N_EXPERTS = 8
MOE_BLOCK_ROWS = 512
FFN_TF = 512

VMEM_LIMIT = 56 * 1024 * 1024


def _cparams(sem):
    return pltpu.CompilerParams(dimension_semantics=sem, vmem_limit_bytes=VMEM_LIMIT)


def _rms(x):
    return x * lax.rsqrt(jnp.mean(x * x, axis=-1, keepdims=True) + EPS)


def _sigmoid(x):
    return 1.0 / (1.0 + jnp.exp(-x))


def _dot(a, b):
    return jnp.dot(a, b, preferred_element_type=F32)


def _dot_nt(a, b):
    return lax.dot_general(a, b, (((1,), (1,)), ((), ())), preferred_element_type=F32)


def _dot_tn(a, b):
    return lax.dot_general(a, b, (((0,), (0,)), ((), ())), preferred_element_type=F32)


def _side_rows(w, n_steps):
    e, r, c = w.shape
    for rb in range(BF16_SUBLANES, r + 1, BF16_SUBLANES):
        if r % rb == 0 and e * (r // rb) <= n_steps:
            return rb if rb * c * 4 <= SIDE_BLOCK_BYTES else None
    return None


def _side_ok(weights, n_steps):
    return all(_side_rows(w, n_steps) is not None for w in weights)


def _side_plan(weights, n_steps, step_of):
    specs, blocks = [], []
    for w in weights:
        rb = _side_rows(w, n_steps)
        per = w.shape[1] // rb
        n_blk = w.shape[0] * per
        blocks.append(n_blk)

        def idx(*grid_ids, per=per, n_blk=n_blk):
            blk = jnp.minimum(step_of(*grid_ids), n_blk - 1)
            return (blk // per, blk % per, 0)

        specs.append(pl.BlockSpec((1, rb, w.shape[2]), idx))
    return specs, tuple(blocks)


def _side_cast(step_id, w_ins, w_outs, blocks):
    for w_in, w_out, n_blk in zip(w_ins, w_outs, blocks):
        @pl.when(step_id < n_blk)
        def _(w_in=w_in, w_out=w_out):
            w_out[...] = w_in[...].astype(w_out.dtype)


def _grid_step_id(n_axes):
    sid = pl.program_id(0)
    for ax in range(1, n_axes):
        sid = sid * pl.num_programs(ax) + pl.program_id(ax)
    return sid


def _with_side(run, weights, n_steps):
    w3 = tuple(w.reshape((1,) + w.shape) if w.ndim == 2 else w for w in weights)
    if _side_ok(w3, n_steps):
        res, copies = run(w3)
        return res, [cp.reshape(w.shape) for cp, w in zip(copies, weights)]
    res, _ = run(())
    return res, [w.astype(BF16) for w in weights]


def _mod_kernel(c_ref, w_ref, b_ref, o_ref):
    c = c_ref[...]
    sc = (c * _sigmoid(c)).astype(BF16)
    o_ref[0] = _dot(sc, w_ref[0].astype(BF16)) + b_ref[0]


def _modulation(c8, w, b, tn=1024):
    g, d, n = w.shape
    return pl.pallas_call(
        _mod_kernel,
        grid=(g, n // tn),
        in_specs=[pl.BlockSpec((8, d), lambda i, j: (0, 0)),
                  pl.BlockSpec((1, d, tn), lambda i, j: (i, 0, j)),
                  pl.BlockSpec((1, 1, tn), lambda i, j: (i, 0, j))],
        out_specs=pl.BlockSpec((1, 8, tn), lambda i, j: (i, 0, j)),
        out_shape=jax.ShapeDtypeStruct((g, 8, n), F32),
        compiler_params=_cparams(("arbitrary", "arbitrary")),
        name="modulation",
    )(c8, w, b.reshape(g, 1, n))


def _rope_kernel(pos_ref, inv_ref, cos_ref, sin_ref):
    ang = pos_ref[...] * inv_ref[...]
    lane = lax.broadcasted_iota(jnp.int32, ang.shape, 1)
    half = QK_ROPE // 2
    cos_ref[...] = jnp.where(lane < QK_ROPE, jnp.cos(ang), 0.0)
    sin_ref[...] = jnp.where(lane < half, -jnp.sin(ang),
                             jnp.where(lane < QK_ROPE, jnp.sin(ang), 0.0))


def _rope_tables(pos, tm=512):
    m = pos.shape[0]
    half = QK_ROPE // 2
    inv = 1.0 / (ROPE_THETA ** (jnp.arange(0, QK_ROPE, 2, dtype=F32) / QK_ROPE))
    inv128 = jnp.concatenate([inv, inv, jnp.zeros((LANES - 2 * half,), F32)]).reshape(1, LANES)
    return pl.pallas_call(
        _rope_kernel,
        grid=(m // tm,),
        in_specs=[pl.BlockSpec((tm, 1), lambda i: (i, 0)),
                  pl.BlockSpec((1, LANES), lambda i: (0, 0))],
        out_specs=[pl.BlockSpec((tm, LANES), lambda i: (i, 0))] * 2,
        out_shape=[jax.ShapeDtypeStruct((m, LANES), F32)] * 2,
        compiler_params=_cparams(("arbitrary",)),
        name="rope_tables",
    )(pos, inv128)


def _hg_proj_kernel(h_ref, g_ref, sc_ref, sh_ref, w_ref, q_ref, f_ref, i_ref, gg_ref, u_scr):
    j = pl.program_id(1)

    @pl.when(j == 0)
    def _():
        u = _rms(h_ref[...]) * g_ref[...] * (1.0 + sc_ref[0]) + sh_ref[0]
        u_scr[...] = u.astype(BF16)

    r = _dot(u_scr[...], w_ref[...])
    for idx, ref in enumerate((q_ref, f_ref, i_ref, gg_ref)):
        @pl.when(j == idx)
        def _(ref=ref):
            ref[...] = r.astype(ref.dtype)


def _hg_proj(h, g, scale, shift, w_in, rows_per_batch, tm=512):
    m, d = h.shape
    bpb = rows_per_batch // tm
    row = lambda i, j: (i, 0)
    mod = lambda i, j: (i // bpb, 0, 0)
    return pl.pallas_call(
        _hg_proj_kernel,
        grid=(m // tm, 4),
        in_specs=[pl.BlockSpec((tm, d), row),
                  pl.BlockSpec((1, d), lambda i, j: (0, 0)),
                  pl.BlockSpec((1, 1, d), mod),
                  pl.BlockSpec((1, 1, d), mod),
                  pl.BlockSpec((d, d), lambda i, j: (0, j))],
        out_specs=[pl.BlockSpec((tm, d), row)] * 4,
        out_shape=[jax.ShapeDtypeStruct((m, d), BF16), jax.ShapeDtypeStruct((m, d), F32),
                   jax.ShapeDtypeStruct((m, d), BF16), jax.ShapeDtypeStruct((m, d), BF16)],
        scratch_shapes=[pltpu.VMEM((tm, d), BF16)],
        compiler_params=_cparams(("arbitrary", "arbitrary")),
        name="hg_proj",
    )(h, g, scale, shift, w_in)


def _hgrn2_kernel(*refs, n_chunks, side_blocks):
    n_side = len(side_blocks)
    q_ref, f_ref, v_ref, lb_ref = refs[:4]
    o_ref = refs[4 + n_side]
    st_ref = refs[5 + 2 * n_side]
    c, sub = HG_CHUNK, HG_SUB
    n_sub = c // sub

    @pl.when(pl.program_id(2) == 0)
    def _():
        st_ref[...] = jnp.zeros_like(st_ref)

    _side_cast(_grid_step_id(3), refs[4:4 + n_side], refs[5 + n_side:5 + 2 * n_side], side_blocks)

    row = lax.broadcasted_iota(jnp.int32, (c, c), 0)
    col = lax.broadcasted_iota(jnp.int32, (c, c), 1)
    tri = (row >= col).astype(BF16)
    row2 = lax.broadcasted_iota(jnp.int32, (c, 2 * c), 0)
    col2 = lax.broadcasted_iota(jnp.int32, (c, 2 * c), 1)
    c_bits, sub_bits = c.bit_length() - 1, sub.bit_length() - 1
    mask2 = ((col2 >> c_bits) == ((row2 >> sub_bits) & 1)) & ((col2 & (c - 1)) <= row2)

    n_heads = q_ref.shape[1] // HG_KDIM
    units = [(slice(ci * c, (ci + 1) * c), slice(hh * HG_KDIM, (hh + 1) * HG_KDIM))
             for hh in range(n_heads) for ci in range(n_chunks)]
    ks, bs = [], []
    for sl, hs in units:
        lb = lb_ref[:, hs]
        f = lb + (1.0 - lb) * _sigmoid(f_ref[sl, hs])
        lf = jnp.log2(f)
        lf_hi = lf.astype(BF16)
        lf_lo = (lf - lf_hi.astype(F32)).astype(BF16)
        ks.append(1.0 - f)
        bs.append(_dot(tri, lf_hi) + _dot(tri, lf_lo))
    a_fulls, qbs, kdecs, decays = [], [], [], []
    for (sl, hs), k, b in zip(units, ks, bs):
        q = q_ref[sl, hs].astype(F32)
        b_last = b[c - 1:c, :]
        starts = [jnp.zeros_like(b_last)] + [b[j * sub - 1:j * sub, :] for j in range(1, n_sub)]
        rrow = jnp.concatenate([jnp.broadcast_to(r, (sub, r.shape[1])) for r in starts], axis=0)
        qe = q * jnp.exp2(b - rrow)
        qbs.append((qe * jnp.exp2(rrow)).astype(BF16))
        ke = jnp.concatenate(
            [k * jnp.exp2(jnp.minimum(r - b, HG_EXP2_CLAMP)) for r in starts], axis=0)
        a_fulls.append(_dot_nt(qe.astype(BF16), ke.astype(BF16)))
        kdecs.append((k * jnp.exp2(b_last - b)).astype(BF16))
        decays.append(jnp.exp2(b_last))
    o_intras, incs = [], []
    for (sl, hs), a_full, kdec in zip(units, a_fulls, kdecs):
        v = v_ref[sl, hs]
        a2 = jnp.concatenate(
            [a_full[j * sub:(j + 1) * sub, (j // 2) * 2 * c:(j // 2 + 1) * 2 * c]
             for j in range(n_sub)], axis=0)
        a2 = jnp.where(mask2, a2, 0.0).astype(BF16)
        o_intras.append(_dot(a2, jnp.concatenate([v, v], axis=0)))
        incs.append(_dot_tn(v, kdec))
    for hh in range(n_heads):
        st = st_ref[hh]
        for ui in range(hh * n_chunks, (hh + 1) * n_chunks):
            sl, hs = units[ui]
            o_ref[sl, hs] = o_intras[ui] + _dot_nt(qbs[ui], st.astype(BF16))
            st = st * decays[ui] + incs[ui]
        st_ref[hh] = st


def _hgrn2_steps(m, tb=512):
    return (HG_HEADS // HG_HEADS_PER_STEP) * (m // tb)


def _hgrn2(q, f_logit, v, lb, batch, side=(), tb=512):
    m, d = q.shape
    s = m // batch
    nt = s // tb
    hg = HG_HEADS // HG_HEADS_PER_STEP
    wd = HG_HEADS_PER_STEP * HG_KDIM
    blk = lambda b, h, t: (b * nt + t, h)
    side_specs, side_blocks = _side_plan(
        side, batch * hg * nt, lambda b, h, t: (b * hg + h) * nt + t)
    outs = pl.pallas_call(
        functools.partial(_hgrn2_kernel, n_chunks=tb // HG_CHUNK, side_blocks=side_blocks),
        grid=(batch, hg, nt),
        in_specs=[pl.BlockSpec((tb, wd), blk)] * 3
        + [pl.BlockSpec((1, wd), lambda b, h, t: (0, h))] + side_specs,
        out_specs=[pl.BlockSpec((tb, wd), blk)] + side_specs,
        out_shape=[jax.ShapeDtypeStruct((m, d), F32)]
        + [jax.ShapeDtypeStruct(w.shape, BF16) for w in side],
        scratch_shapes=[pltpu.VMEM((HG_HEADS_PER_STEP, HG_KDIM, HG_KDIM), F32)],
        compiler_params=_cparams(("arbitrary", "arbitrary", "arbitrary")),
        name="hgrn2_recurrence",
    )(q, f_logit, v, lb, *side)
    return outs[0], list(outs[1:])


def _oproj_kernel(*refs, gated):
    if gated:
        (o_ref, gg_ref, gout_ref, w_ref, h_ref, gate_ref, g2_ref, gn_ref, scn_ref, shn_ref,
         out_ref, un_ref) = refs
        gg = gg_ref[...].astype(F32)
        x = _rms(o_ref[...]) * gout_ref[...] * (gg * _sigmoid(gg))
    else:
        o_ref, w_ref, h_ref, gate_ref, g2_ref, out_ref = refs
        x = o_ref[...]
    y = _dot(x.astype(BF16), w_ref[...])
    h_new = h_ref[...] + gate_ref[0] * (_rms(y) * g2_ref[...])
    out_ref[...] = h_new
    if gated:
        un_ref[...] = (_rms(h_new) * gn_ref[...] * (1.0 + scn_ref[0]) + shn_ref[0]).astype(BF16)


def _oproj(o, w, h, gate, g2, rows_per_batch, gg=None, gout=None, nxt=None, tm=512):
    m, d = h.shape
    bpb = rows_per_batch // tm
    row = pl.BlockSpec((tm, d), lambda i: (i, 0))
    vec = pl.BlockSpec((1, d), lambda i: (0, 0))
    mod = pl.BlockSpec((1, 1, d), lambda i: (i // bpb, 0, 0))
    wsp = pl.BlockSpec(w.shape, lambda i: (0, 0), pipeline_mode=pl.Buffered(1))
    gated = gg is not None
    if gated:
        args = (o, gg, gout, w, h, gate, g2) + tuple(nxt)
        specs = [row, row, vec, wsp, row, mod, vec, vec, mod, mod]
        out_specs = [row, row]
        out_shape = [jax.ShapeDtypeStruct((m, d), F32), jax.ShapeDtypeStruct((m, d), BF16)]
    else:
        args, specs = (o, w, h, gate, g2), [row, wsp, row, mod, vec]
        out_specs, out_shape = row, jax.ShapeDtypeStruct((m, d), F32)
    return pl.pallas_call(
        functools.partial(_oproj_kernel, gated=gated),
        grid=(m // tm,),
        in_specs=specs,
        out_specs=out_specs,
        out_shape=out_shape,
        compiler_params=_cparams(("arbitrary",)),
        name="hg_out_proj" if gated else "mla_out_proj",
    )(*args)


def _ffn_kernel(*refs, side_blocks):
    n_side = len(side_blocks)
    u_ref, wg_ref, wu_ref, wd_ref = refs[:4]
    y_ref = refs[4 + n_side]

    @pl.when(pl.program_id(1) == 0)
    def _():
        y_ref[...] = jnp.zeros_like(y_ref)

    _side_cast(_grid_step_id(2), refs[4:4 + n_side], refs[5 + n_side:5 + 2 * n_side], side_blocks)
    u = u_ref[...]
    gt = _dot(u, wg_ref[...])
    up = _dot(u, wu_ref[...])
    a = (gt * _sigmoid(gt) * up).astype(BF16)
    y_ref[...] += _dot(a, wd_ref[...])


def _ffn_steps(m, ff, tm=512, tf=FFN_TF):
    return (m // tm) * (ff // tf)


def _ffn(u, w_gu, w_down, side=(), tm=512, tf=FFN_TF):
    m, d = u.shape
    ff = w_down.shape[0]
    nf = ff // tf
    row = pl.BlockSpec((tm, d), lambda i, j: (i, 0))
    side_specs, side_blocks = _side_plan(side, (m // tm) * nf, lambda i, j: i * nf + j)
    outs = pl.pallas_call(
        functools.partial(_ffn_kernel, side_blocks=side_blocks),
        grid=(m // tm, nf),
        in_specs=[row,
                  pl.BlockSpec((d, tf), lambda i, j: (0, j)),
                  pl.BlockSpec((d, tf), lambda i, j: (0, nf + j)),
                  pl.BlockSpec((tf, d), lambda i, j: (j, 0))] + side_specs,
        out_specs=[row] + side_specs,
        out_shape=[jax.ShapeDtypeStruct((m, d), F32)]
        + [jax.ShapeDtypeStruct(w.shape, BF16) for w in side],
        compiler_params=_cparams(("arbitrary", "arbitrary")),
        name="dense_ffn",
    )(u, w_gu, w_gu, w_down, *side)
    return outs[0], list(outs[1:])


def _mla_proj_kernel(h_ref, y_ref, gate_ref, g2_ref, gq_ref, scq_ref, shq_ref, gk_ref, sck_ref, shk_ref,
                     wqa_ref, qg_ref, wqm_ref, wqr_ref,
                     wkva_ref, wkr_ref, wkrr_ref, kg_ref, wk_ref, wv_ref,
                     cos_ref, sin_ref, hn_ref, q_ref, k_ref, v_ref):
    h_new = h_ref[...] + gate_ref[0] * (_rms(y_ref[...]) * g2_ref[...])
    hn_ref[...] = h_new
    xhat = _rms(h_new)
    uq = (xhat * gq_ref[...] * (1.0 + scq_ref[0]) + shq_ref[0]).astype(BF16)
    xk = (xhat * gk_ref[...] * (1.0 + sck_ref[0]) + shk_ref[0]).astype(BF16)
    cos = cos_ref[...]
    sin = sin_ref[...]
    qn = (_rms(_dot(uq, wqa_ref[...])) * qg_ref[...]).astype(BF16)
    cn = (_rms(_dot(xk, wkva_ref[...])) * kg_ref[...]).astype(BF16)
    krope = (_dot(xk, wkr_ref[...]) * cos + _dot(xk, wkrr_ref[...]) * sin).astype(BF16)
    ones = jnp.ones((cn.shape[0], V_HEAD), BF16)
    for hp in range(MLA_HEADS // 2):
        ps = slice(hp * 2 * LANES, (hp + 1) * 2 * LANES)
        qr2 = _dot(qn, wqr_ref[:, ps]) * Q_SCALE
        kn2 = _dot(cn, wk_ref[:, ps]).astype(BF16)
        v2 = _dot(cn, wv_ref[:, ps]).astype(BF16)
        for sub in range(2):
            hd = 2 * hp + sub
            ls = slice(sub * LANES, (sub + 1) * LANES)
            qm = _dot(qn, wqm_ref[:, hd * QK_PAD:(hd + 1) * QK_PAD]) * Q_SCALE
            q_ref[:, hd * QK_PAD:hd * QK_PAD + QK_NOPE] = qm[:, :QK_NOPE].astype(BF16)
            q_ref[:, hd * QK_PAD + QK_NOPE:(hd + 1) * QK_PAD] = (
                qm[:, QK_NOPE:] * cos + qr2[:, ls] * sin).astype(BF16)
            k_ref[:, hd * QK_PAD:hd * QK_PAD + QK_NOPE] = kn2[:, ls]
            k_ref[:, hd * QK_PAD + QK_NOPE:(hd + 1) * QK_PAD] = krope
            v_ref[:, 2 * hd * V_HEAD:(2 * hd + 1) * V_HEAD] = v2[:, ls]
            v_ref[:, (2 * hd + 1) * V_HEAD:(2 * hd + 2) * V_HEAD] = ones


def _mla_proj(h, y, gate, g2, gq, scq, shq, gk, sck, shk, wts, cos, sin, rows_per_batch, tm=256):
    m, d = h.shape
    bpb = rows_per_batch // tm
    row = lambda w: pl.BlockSpec((tm, w), lambda i: (i, 0))
    vec = lambda w: pl.BlockSpec((1, w), lambda i: (0, 0))
    mod = pl.BlockSpec((1, 1, d), lambda i: (i // bpb, 0, 0))
    full = lambda a: pl.BlockSpec(a.shape, lambda i: (0, 0), pipeline_mode=pl.Buffered(1))
    wqa, qg, wqm, wqr, wkva, wkr, wkrr, kg, wk, wv = wts
    return pl.pallas_call(
        _mla_proj_kernel,
        grid=(m // tm,),
        in_specs=[row(d), row(d), mod, vec(d), vec(d), mod, mod, vec(d), mod, mod,
                  full(wqa), full(qg), full(wqm), full(wqr),
                  full(wkva), full(wkr), full(wkrr), full(kg), full(wk), full(wv),
                  row(LANES), row(LANES)],
        out_specs=[row(d), row(MLA_HEADS * QK_PAD), row(MLA_HEADS * QK_PAD),
                   row(MLA_HEADS * 2 * V_HEAD)],
        out_shape=[jax.ShapeDtypeStruct((m, d), F32),
                   jax.ShapeDtypeStruct((m, MLA_HEADS * QK_PAD), BF16),
                   jax.ShapeDtypeStruct((m, MLA_HEADS * QK_PAD), BF16),
                   jax.ShapeDtypeStruct((m, MLA_HEADS * 2 * V_HEAD), BF16)],
        compiler_params=_cparams(("arbitrary",)),
        name="mla_proj",
    )(h, y, gate, g2, gq, scq, shq, gk, sck, shk, wqa, qg, wqm, wqr, wkva, wkr, wkrr, kg, wk, wv,
      cos, sin)


def _flash_kernel(*refs, tq, tk, slab, side_blocks):
    n_side = len(side_blocks)
    q_ref, k_ref, v_ref = refs[:3]
    o_ref = refs[3 + n_side]
    s_a, s_b, p_scr, m_scr, al_scr, acc_scr = refs[4 + 2 * n_side:]
    qi = pl.program_id(2)
    _side_cast(_grid_step_id(3), refs[3:3 + n_side], refs[4 + n_side:4 + 2 * n_side], side_blocks)

    s_bufs = (s_a, s_b)
    m_scr[...] = jnp.full_like(m_scr, -jnp.inf)
    acc_scr[...] = jnp.zeros_like(acc_scr)
    n_col = tk // LANES
    n_diag = tq // tk

    def kv_rows(j):
        return pl.ds(pl.multiple_of(j * tk, tk), tk)

    def scores(j, dst, row0):
        dst[row0:, :] = _dot_nt(q_ref[row0:, :], k_ref[kv_rows(j), :])

    def consume(j, src, row0, diag):
        for r0 in range(row0, tq, slab):
            rows = slice(r0, r0 + slab)
            cols = [src[rows, cb * LANES:(cb + 1) * LANES] for cb in range(n_col)]
            if diag and r0 < row0 + tk:
                rowp = (r0 - row0) + lax.broadcasted_iota(jnp.int32, (slab, LANES), 0)
                lane = lax.broadcasted_iota(jnp.int32, (slab, LANES), 1)
                cols = [jnp.where(cb * LANES + lane <= rowp, cols[cb], -jnp.inf)
                        for cb in range(n_col)]
            mx = functools.reduce(jnp.maximum, cols)
            m_old = m_scr[rows, :]
            m_new = jnp.maximum(m_old, jnp.max(mx, axis=-1, keepdims=True))
            al_scr[rows, :] = jnp.exp2(m_old - m_new)
            m_scr[rows, :] = m_new
            for cb in range(n_col):
                p_scr[rows, cb * LANES:(cb + 1) * LANES] = jnp.exp2(cols[cb] - m_new).astype(BF16)
        pv = _dot(p_scr[row0:, :], v_ref[kv_rows(j), :])
        al = al_scr[row0:, :]
        acc_scr[row0:, :V_HEAD] = acc_scr[row0:, :V_HEAD] * al + pv[:, :V_HEAD]
        acc_scr[row0:, V_HEAD:] = acc_scr[row0:, V_HEAD:] * al + pv[:, V_HEAD:]

    n_full = qi * n_diag
    scores(0, s_a, 0)

    def pair(pi, carry):
        for u in range(2):
            scores(2 * pi + u + 1, s_bufs[1 - u], 0)
            consume(2 * pi + u, s_bufs[u], 0, False)
        return carry

    lax.fori_loop(0, n_full // 2, pair, 0)
    for dg in range(n_diag):
        if dg + 1 < n_diag:
            scores(n_full + dg + 1, s_bufs[(dg + 1) % 2], (dg + 1) * tk)
        consume(n_full + dg, s_bufs[dg % 2], dg * tk, True)

    o_ref[...] = (acc_scr[:, :V_HEAD] / acc_scr[:, V_HEAD:]).astype(o_ref.dtype)


def _flash_tiles(seq):
    tq = min(FLASH_TQ, seq)
    return tq, min(FLASH_TK, tq // 2)


def _flash(q, k, v, batch, side=(), slab=64):
    m = q.shape[0]
    s = m // batch
    tq, tk = _flash_tiles(s)
    nq = s // tq
    side_specs, side_blocks = _side_plan(
        side, batch * MLA_HEADS * nq, lambda b, h, i: (b * MLA_HEADS + h) * nq + i)
    outs = pl.pallas_call(
        functools.partial(_flash_kernel, tq=tq, tk=tk, slab=slab, side_blocks=side_blocks),
        grid=(batch, MLA_HEADS, nq),
        in_specs=[pl.BlockSpec((tq, QK_PAD), lambda b, h, i: (b * nq + i, h)),
                  pl.BlockSpec((s, QK_PAD), lambda b, h, i: (b, h)),
                  pl.BlockSpec((s, 2 * V_HEAD), lambda b, h, i: (b, h))] + side_specs,
        out_specs=[pl.BlockSpec((tq, V_HEAD), lambda b, h, i: (b * nq + i, h))] + side_specs,
        out_shape=[jax.ShapeDtypeStruct((m, MLA_HEADS * V_HEAD), BF16)]
        + [jax.ShapeDtypeStruct(w.shape, BF16) for w in side],
        scratch_shapes=[pltpu.VMEM((tq, tk), F32), pltpu.VMEM((tq, tk), F32),
                        pltpu.VMEM((tq, tk), BF16), pltpu.VMEM((tq, LANES), F32),
                        pltpu.VMEM((tq, LANES), F32), pltpu.VMEM((tq, 2 * V_HEAD), F32)],
        compiler_params=_cparams(("arbitrary", "arbitrary", "arbitrary")),
        name="mla_flash",
    )(q, k, v, *side)
    return outs[0], list(outs[1:])


def _split3(x):
    hi = x.astype(BF16)
    lo = (x - hi.astype(F32)).astype(BF16)
    return hi, lo


def _moe_pre_kernel(h_ref, g_ref, sc_ref, sh_ref, wr_ref, xs_ref, meta_ref, wts_ref, cnt_ref,
                    run_scr):
    tm, d = h_ref.shape

    @pl.when(pl.program_id(0) == 0)
    def _():
        run_scr[...] = jnp.zeros_like(run_scr)

    u = _rms(h_ref[...]) * g_ref[...] * (1.0 + sc_ref[0]) + sh_ref[0]
    xs_ref[...] = u

    u_hi, u_lo = _split3(u)
    w_hi, w_lo = _split3(wr_ref[...])
    logits = _dot(u_hi, w_hi) + (_dot(u_hi, w_lo) + _dot(u_lo, w_hi))
    lane = lax.broadcasted_iota(jnp.int32, logits.shape, 1)
    lg = jnp.where(lane < N_EXPERTS, logits, -jnp.inf)
    m1 = jnp.max(lg, axis=-1, keepdims=True)
    i1 = jnp.min(jnp.where(lg == m1, lane, LANES), axis=-1, keepdims=True)
    lg2 = jnp.where(lane == i1, -jnp.inf, lg)
    m2 = jnp.max(lg2, axis=-1, keepdims=True)
    i2 = jnp.min(jnp.where(lg2 == m2, lane, LANES), axis=-1, keepdims=True)
    e = jnp.exp(m2 - m1)
    w1 = 1.0 / (1.0 + e)
    w2 = e * w1

    oh1 = lane == i1
    oh2 = lane == i2
    cnt = oh1.astype(F32) + oh2.astype(F32)
    r = lax.broadcasted_iota(jnp.int32, (tm, tm), 0)
    c = lax.broadcasted_iota(jnp.int32, (tm, tm), 1)
    strict = (r > c).astype(BF16)
    prefix = _dot(strict, cnt.astype(BF16)) + run_scr[...]
    rank1 = jnp.sum(jnp.where(oh1, prefix, 0.0), axis=-1, keepdims=True).astype(jnp.int32)
    rank2 = jnp.sum(jnp.where(oh2, prefix, 0.0), axis=-1, keepdims=True).astype(jnp.int32)
    run_scr[...] += jnp.sum(cnt, axis=0, keepdims=True)

    meta_ref[...] = jnp.where(lane == 0, i1, jnp.where(lane == 1, i2,
                              jnp.where(lane == 2, rank1, jnp.where(lane == 3, rank2, 0))))
    wts_ref[...] = jnp.where(lane == 0, w1, jnp.where(lane == 1, w2, 0.0))
    cnt_ref[...] = run_scr[...]


def _moe_pre(h, g, scale, shift, w_router_pad, rows_per_batch, tm=512):
    m, d = h.shape
    bpb = rows_per_batch // tm
    row = lambda w: pl.BlockSpec((tm, w), lambda i: (i, 0))
    vec = lambda w: pl.BlockSpec((1, w), lambda i: (0, 0))
    mod = pl.BlockSpec((1, 1, d), lambda i: (i // bpb, 0, 0))
    return pl.pallas_call(
        _moe_pre_kernel,
        grid=(m // tm,),
        in_specs=[row(d), vec(d), mod, mod, pl.BlockSpec((d, LANES), lambda i: (0, 0))],
        out_specs=[row(d), row(LANES), row(LANES), vec(LANES)],
        out_shape=[jax.ShapeDtypeStruct((m, d), F32),
                   jax.ShapeDtypeStruct((m, LANES), jnp.int32),
                   jax.ShapeDtypeStruct((m, LANES), F32),
                   jax.ShapeDtypeStruct((1, LANES), F32)],
        scratch_shapes=[pltpu.VMEM((1, LANES), F32)],
        compiler_params=_cparams(("arbitrary",)),
        name="moe_route",
    )(h, g, scale, shift, w_router_pad)


ROW_COPY_UNROLL = 4


def _issue_row_copies(n_rows, copies_of_row):
    def trip(t, carry):
        n = 0
        for u in range(ROW_COPY_UNROLL):
            for cp in copies_of_row(t * ROW_COPY_UNROLL + u):
                cp.start(priority=n % 2)
                n += 1
        return carry
    lax.fori_loop(0, n_rows // ROW_COPY_UNROLL, trip, 0)


def _zero_blocks_kernel(ids_ref, o_ref):
    del ids_ref
    o_ref[...] = jnp.zeros_like(o_ref)


def _zero_blocks(shape, tmb, blk_ids):
    return pl.pallas_call(
        _zero_blocks_kernel,
        grid_spec=pltpu.PrefetchScalarGridSpec(
            num_scalar_prefetch=1,
            grid=(blk_ids.shape[0],),
            in_specs=[],
            out_specs=pl.BlockSpec((tmb, shape[1]), lambda e, ids: (ids[e], 0)),
        ),
        out_shape=jax.ShapeDtypeStruct(shape, F32),
        compiler_params=_cparams(("arbitrary",)),
        name="moe_zero_tails",
    )(blk_ids)


def _moe_dispatch_kernel(dest_ref, x_ref, xs_in_ref, xs_ref, sem):
    del xs_in_ref
    tm = x_ref.shape[0]

    def copies_of_row(r):
        return [pltpu.make_async_copy(x_ref.at[pl.ds(r, 1), :],
                                      xs_ref.at[pl.ds(dest_ref[0, 0, 2 * r + k], 1), :], sem)
                for k in range(2)]

    _issue_row_copies(tm, copies_of_row)
    for _ in range(2):
        pltpu.make_async_copy(x_ref, x_ref, sem).wait()


def _moe_dispatch(x, dest, xs_init, tm=256):
    m, w = x.shape
    nb = m // tm
    return pl.pallas_call(
        _moe_dispatch_kernel,
        grid=(nb,),
        in_specs=[pl.BlockSpec((1, 1, 2 * tm), lambda i: (i, 0, 0), memory_space=pltpu.SMEM),
                  pl.BlockSpec((tm, w), lambda i: (i, 0)),
                  pl.BlockSpec(memory_space=pl.ANY)],
        out_specs=pl.BlockSpec(memory_space=pl.ANY),
        out_shape=jax.ShapeDtypeStruct(xs_init.shape, xs_init.dtype),
        scratch_shapes=[pltpu.SemaphoreType.DMA(())],
        input_output_aliases={2: 0},
        compiler_params=_cparams(("arbitrary",)),
        name="moe_dispatch",
    )(dest.reshape(nb, 1, 2 * tm), x, xs_init)


def _moe_ffn_kernel(blk_e_ref, n_used_ref, xs_ref, wg_ref, wu_ref, wd_ref, y_ref, u_scr):
    i = pl.program_id(0)
    j = pl.program_id(1)

    @pl.when(i < n_used_ref[0])
    def _():
        @pl.when(j == 0)
        def _():
            u_scr[...] = xs_ref[...].astype(BF16)
            y_ref[...] = jnp.zeros_like(y_ref)

        u = u_scr[...]
        gt = _dot(u, wg_ref[0])
        up = _dot(u, wu_ref[0])
        a = (gt * _sigmoid(gt) * up).astype(BF16)
        y_ref[...] += _dot(a, wd_ref[0])

    @pl.when((i >= n_used_ref[0]) & (j == 0))
    def _():
        y_ref[...] = jnp.zeros_like(y_ref)


def _moe_ffn(xs, blk_e, n_used, w_gu, w_down, tmb, tf=FFN_TF):
    n_rows, d = xs.shape
    ff = w_down.shape[1]
    nf = ff // tf
    nb = n_rows // tmb

    def blk(i, n_used_ref):
        return jnp.minimum(i, n_used_ref[0] - 1)

    def fidx(i, j, n_used_ref):
        return jnp.where(i < n_used_ref[0], j, nf - 1)

    grid_spec = pltpu.PrefetchScalarGridSpec(
        num_scalar_prefetch=2,
        grid=(nb, nf),
        in_specs=[
            pl.BlockSpec((tmb, d), lambda i, j, be, nu: (blk(i, nu), 0)),
            pl.BlockSpec((1, d, tf), lambda i, j, be, nu: (be[blk(i, nu)], 0, fidx(i, j, nu))),
            pl.BlockSpec((1, d, tf), lambda i, j, be, nu: (be[blk(i, nu)], 0, nf + fidx(i, j, nu))),
            pl.BlockSpec((1, tf, d), lambda i, j, be, nu: (be[blk(i, nu)], fidx(i, j, nu), 0)),
        ],
        out_specs=pl.BlockSpec((tmb, d), lambda i, j, be, nu: (i, 0)),
        scratch_shapes=[pltpu.VMEM((tmb, d), BF16)],
    )
    return pl.pallas_call(
        _moe_ffn_kernel,
        grid_spec=grid_spec,
        out_shape=jax.ShapeDtypeStruct((n_rows, d), F32),
        compiler_params=_cparams(("arbitrary", "arbitrary")),
        name="moe_grouped_ffn",
    )(blk_e, n_used, xs, w_gu, w_gu, w_down)


def _moe_combine_kernel(dest_ref, ys_ref, wts_ref, h_ref, gate_ref, g2_ref, out_ref,
                        buf0, buf1, sem):
    tm = h_ref.shape[0]
    bufs = (buf0, buf1)

    def copies_of_row(r):
        return [pltpu.make_async_copy(ys_ref.at[pl.ds(dest_ref[0, 0, 2 * r + k], 1), :],
                                      bufs[k].at[pl.ds(r, 1), :], sem)
                for k in range(2)]

    _issue_row_copies(tm, copies_of_row)
    for k in range(2):
        pltpu.make_async_copy(bufs[k], bufs[k], sem).wait()

    wts = wts_ref[...]
    y = wts[:, 0:1] * buf0[...] + wts[:, 1:2] * buf1[...]
    out_ref[...] = h_ref[...] + gate_ref[0] * (_rms(y) * g2_ref[...])


def _moe_combine(ys, dest, wts, h, gate, g2, rows_per_batch, tm=256):
    m, d = h.shape
    nb = m // tm
    bpb = rows_per_batch // tm
    row = lambda w: pl.BlockSpec((tm, w), lambda i: (i, 0))
    return pl.pallas_call(
        _moe_combine_kernel,
        grid=(nb,),
        in_specs=[pl.BlockSpec((1, 1, 2 * tm), lambda i: (i, 0, 0), memory_space=pltpu.SMEM),
                  pl.BlockSpec(memory_space=pl.ANY),
                  row(LANES), row(d),
                  pl.BlockSpec((1, 1, d), lambda i: (i // bpb, 0, 0)),
                  pl.BlockSpec((1, d), lambda i: (0, 0))],
        out_specs=row(d),
        out_shape=jax.ShapeDtypeStruct((m, d), F32),
        scratch_shapes=[pltpu.VMEM((tm, d), F32), pltpu.VMEM((tm, d), F32),
                        pltpu.SemaphoreType.DMA(())],
        compiler_params=_cparams(("arbitrary",)),
        name="moe_combine",
    )(dest.reshape(nb, 1, 2 * tm), ys, wts, h, gate, g2)


def _moe(h, g, scale, shift, w_router, w_gu, w_down, gate, g2, rows_per_batch, tmb=MOE_BLOCK_ROWS):
    m, d = h.shape
    n_rows = (2 * m + N_EXPERTS * (tmb - 1)) // tmb * tmb
    nb = n_rows // tmb
    w_router_pad = jnp.pad(w_router, ((0, 0), (0, LANES - N_EXPERTS)))
    u, meta, wts, cnt = _moe_pre(h, g, scale, shift, w_router_pad, rows_per_batch)

    counts = cnt[0, :N_EXPERTS].astype(jnp.int32)
    padded = (counts + tmb - 1) // tmb * tmb
    pend = jnp.cumsum(padded)
    pstart = pend - padded
    dest = (pstart[meta[:, 0:2]] + meta[:, 2:4]).astype(jnp.int32).reshape(-1)
    blk_start = jnp.arange(nb, dtype=jnp.int32) * tmb
    blk_e = jnp.minimum(jnp.sum(pend[None, :] <= blk_start[:, None], axis=1),
                        N_EXPERTS - 1).astype(jnp.int32)
    n_used = (pend[-1:] // tmb).astype(jnp.int32)

    tail_blk = jnp.clip(pend // tmb - 1, 0, nb - 1).astype(jnp.int32)
    xs = _moe_dispatch(u, dest, _zero_blocks((n_rows, d), tmb, tail_blk))
    ys = _moe_ffn(xs, blk_e, n_used, w_gu, w_down, tmb)
    return _moe_combine(ys, dest, wts, h, gate, g2, rows_per_batch)


def _swap_halves(w):
    half = w.shape[-1] // 2
    return jnp.concatenate([w[..., half:], w[..., :half]], axis=-1)


def _mla_weights(w_q_a, q_norm_g, w_q_b, w_kv_a, kv_norm_g, w_kv_b):
    ql = w_q_b.shape[0]
    kvl = w_kv_b.shape[0]
    wq = w_q_b.reshape(ql, MLA_HEADS, QK_NOPE + QK_ROPE)
    rope = wq[:, :, QK_NOPE:]
    pad_r = LANES - QK_ROPE
    wqm = jnp.concatenate([wq, jnp.zeros((ql, MLA_HEADS, QK_PAD - QK_NOPE - QK_ROPE), F32)], axis=-1)
    wqr = jnp.concatenate([_swap_halves(rope), jnp.zeros((ql, MLA_HEADS, pad_r), F32)], axis=-1)
    wkv = w_kv_b.reshape(kvl, MLA_HEADS, QK_NOPE + V_HEAD)
    kr = w_kv_a[:, kvl:]
    dm = w_kv_a.shape[0]
    wkr = jnp.concatenate([kr, jnp.zeros((dm, pad_r), F32)], axis=-1)
    wkrr = jnp.concatenate([_swap_halves(kr), jnp.zeros((dm, pad_r), F32)], axis=-1)
    bf = lambda a: a.astype(BF16)
    return (bf(w_q_a), q_norm_g.reshape(1, -1), bf(wqm.reshape(ql, -1)), bf(wqr.reshape(ql, -1)),
            bf(w_kv_a[:, :kvl]), bf(wkr), bf(wkrr), kv_norm_g.reshape(1, -1),
            bf(wkv[:, :, :QK_NOPE].reshape(kvl, -1)), bf(wkv[:, :, QK_NOPE:].reshape(kvl, -1)))


def kernel(x, c, positions, ada_w, ada_b, norm_g, hg_w_in, hg_lb_logits, hg_out_norm_g, hg_w_out, kv_src_norm_g, kv_src_ada_w, kv_src_ada_b, mla_w_kv_a, mla_kv_norm_g, mla_w_kv_b, mla_w_q_a, mla_q_norm_g, mla_w_q_b, mla_w_o, ffn_w_gu, ffn_w_down, moe_w_router, moe_w_gu, moe_w_down):
    batch, seq, d = x.shape
    m = batch * seq
    bf = lambda a: a.astype(BF16)

    c8 = jnp.pad(c, ((0, 8 - batch), (0, 0)))
    ada = _modulation(c8, ada_w.reshape(4, d, 3 * d), ada_b.reshape(4, 3 * d))[:, :batch]
    kvm = _modulation(c8, kv_src_ada_w[None], kv_src_ada_b[None])[0, :batch]

    def mods(idx):
        a = ada[idx]
        return [a[:, i * d:(i + 1) * d].reshape(batch, 1, d) for i in range(3)]

    vec = lambda a: a.reshape(1, -1)
    lb_all = jnp.cumsum(jax.nn.softmax(hg_lb_logits.astype(F32), axis=0), axis=0)

    h = x.reshape(m, d)

    shift, scale, gate = mods(0)
    q, f_logit, iv, gg = _hg_proj(h, vec(norm_g[0, 0, 0]), scale, shift, bf(hg_w_in[0]), seq)
    o, (ffn_gu_bf, ffn_down_bf, hg_out_bf, moe_down_bf) = _with_side(
        lambda side: _hgrn2(q, f_logit, iv, vec(lb_all[0]), batch, side=side),
        [ffn_w_gu[0], ffn_w_down[0], hg_w_out[0], moe_w_down[0]], _hgrn2_steps(m))
    shift_f, scale_f, gate_f = mods(1)
    h, u_ffn = _oproj(o, hg_out_bf, h, gate, vec(norm_g[0, 0, 1]), seq,
                      gg=gg, gout=vec(hg_out_norm_g[0]),
                      nxt=(vec(norm_g[0, 1, 0]), scale_f, shift_f))

    y_ffn, (moe_gu_bf, mla_o_bf) = _with_side(
        lambda side: _ffn(u_ffn, ffn_gu_bf, ffn_down_bf, side=side),
        [moe_w_gu[0], mla_w_o[0]], _ffn_steps(m, ffn_w_down.shape[1]))

    shift_k, scale_k = [kvm[:, i * d:(i + 1) * d].reshape(batch, 1, d) for i in range(2)]
    shift, scale, gate = mods(2)
    cos, sin = _rope_tables(positions.reshape(m, 1).astype(F32))
    wts = _mla_weights(mla_w_q_a[0], mla_q_norm_g[0], mla_w_q_b[0],
                       mla_w_kv_a, mla_kv_norm_g, mla_w_kv_b)
    h, qh, kh, vh = _mla_proj(h, y_ffn, gate_f, vec(norm_g[0, 1, 1]),
                              vec(norm_g[1, 0, 0]), scale, shift,
                              vec(kv_src_norm_g), scale_k, shift_k, wts, cos, sin, seq)
    att, _ = _flash(qh, kh, vh, batch)
    h = _oproj(att, mla_o_bf, h, gate, vec(norm_g[1, 0, 1]), seq)

    shift, scale, gate = mods(3)
    h = _moe(h, vec(norm_g[1, 1, 0]), scale, shift, moe_w_router[0],
             moe_gu_bf, moe_down_bf, gate, vec(norm_g[1, 1, 1]), seq)
    return h.reshape(batch, seq, d)
```

```python
import functools

import jax
import jax.numpy as jnp
from jax import lax
from jax.experimental import pallas as pl
from jax.experimental.pallas import tpu as pltpu

F32 = jnp.float32
BF16 = jnp.bfloat16

EPS = 1e-6
LANES = 128
F32_SUBLANES = 8
BF16_SUBLANES = 16
SIDE_BLOCK_BYTES = 3 * 1024 * 1024

HG_HEADS = 16
HG_KDIM = 128
HG_CHUNK = 64
HG_SUB = 16
HG_EXP2_CLAMP = 115.0
HG_HEADS_PER_STEP = 2

MLA_HEADS = 16
QK_NOPE = 128
QK_ROPE = 64
V_HEAD = 128
QK_PAD = 256
MLA_SCALE = (QK_NOPE + QK_ROPE) ** -0.5
Q_SCALE = MLA_SCALE * 1.4426950408889634
ROPE_THETA = 10000.0
FLASH_TQ = 2048
FLASH_TK = 1024

N_EXPERTS = 8
MOE_BLOCK_ROWS = 512
FFN_TF = 512

VMEM_LIMIT = 56 * 1024 * 1024


def _cparams(sem):
    return pltpu.CompilerParams(dimension_semantics=sem, vmem_limit_bytes=VMEM_LIMIT)


def _rms(x):
    return x * lax.rsqrt(jnp.mean(x * x, axis=-1, keepdims=True) + EPS)


def _sigmoid(x):
    return 1.0 / (1.0 + jnp.exp(-x))


def _dot(a, b):
    return jnp.dot(a, b, preferred_element_type=F32)


def _dot_nt(a, b):
    return lax.dot_general(a, b, (((1,), (1,)), ((), ())), preferred_element_type=F32)


def _dot_tn(a, b):
    return lax.dot_general(a, b, (((0,), (0,)), ((), ())), preferred_element_type=F32)


def _side_rows(w, n_steps):
    e, r, c = w.shape
    for rb in range(BF16_SUBLANES, r + 1, BF16_SUBLANES):
        if r % rb == 0 and e * (r // rb) <= n_steps:
            return rb if rb * c * 4 <= SIDE_BLOCK_BYTES else None
    return None


def _side_ok(weights, n_steps):
    return all(_side_rows(w, n_steps) is not None for w in weights)


def _side_plan(weights, n_steps, step_of):
    specs, blocks = [], []
    for w in weights:
        rb = _side_rows(w, n_steps)
        per = w.shape[1] // rb
        n_blk = w.shape[0] * per
        blocks.append(n_blk)

        def idx(*grid_ids, per=per, n_blk=n_blk):
            blk = jnp.minimum(step_of(*grid_ids), n_blk - 1)
            return (blk // per, blk % per, 0)

        specs.append(pl.BlockSpec((1, rb, w.shape[2]), idx))
    return specs, tuple(blocks)


def _side_cast(step_id, w_ins, w_outs, blocks):
    for w_in, w_out, n_blk in zip(w_ins, w_outs, blocks):
        @pl.when(step_id < n_blk)
        def _(w_in=w_in, w_out=w_out):
            w_out[...] = w_in[...].astype(w_out.dtype)


def _grid_step_id(n_axes):
    sid = pl.program_id(0)
    for ax in range(1, n_axes):
        sid = sid * pl.num_programs(ax) + pl.program_id(ax)
    return sid


def _with_side(run, weights, n_steps):
    w3 = tuple(w.reshape((1,) + w.shape) if w.ndim == 2 else w for w in weights)
    if _side_ok(w3, n_steps):
        res, copies = run(w3)
        return res, [cp.reshape(w.shape) for cp, w in zip(copies, weights)]
    res, _ = run(())
    return res, [w.astype(BF16) for w in weights]


def _mod_kernel(c_ref, w_ref, b_ref, o_ref):
    c = c_ref[...]
    sc = (c * _sigmoid(c)).astype(BF16)
    o_ref[0] = _dot(sc, w_ref[0].astype(BF16)) + b_ref[0]


def _modulation(c8, w, b, tn=1024):
    g, d, n = w.shape
    return pl.pallas_call(
        _mod_kernel,
        grid=(g, n // tn),
        in_specs=[pl.BlockSpec((8, d), lambda i, j: (0, 0)),
                  pl.BlockSpec((1, d, tn), lambda i, j: (i, 0, j)),
                  pl.BlockSpec((1, 1, tn), lambda i, j: (i, 0, j))],
        out_specs=pl.BlockSpec((1, 8, tn), lambda i, j: (i, 0, j)),
        out_shape=jax.ShapeDtypeStruct((g, 8, n), F32),
        compiler_params=_cparams(("arbitrary", "arbitrary")),
        name="modulation",
    )(c8, w, b.reshape(g, 1, n))


def _rope_kernel(pos_ref, inv_ref, cos_ref, sin_ref):
    ang = pos_ref[...] * inv_ref[...]
    lane = lax.broadcasted_iota(jnp.int32, ang.shape, 1)
    half = QK_ROPE // 2
    cos_ref[...] = jnp.where(lane < QK_ROPE, jnp.cos(ang), 0.0)
    sin_ref[...] = jnp.where(lane < half, -jnp.sin(ang),
                             jnp.where(lane < QK_ROPE, jnp.sin(ang), 0.0))


def _rope_tables(pos, tm=512):
    m = pos.shape[0]
    half = QK_ROPE // 2
    inv = 1.0 / (ROPE_THETA ** (jnp.arange(0, QK_ROPE, 2, dtype=F32) / QK_ROPE))
    inv128 = jnp.concatenate([inv, inv, jnp.zeros((LANES - 2 * half,), F32)]).reshape(1, LANES)
    return pl.pallas_call(
        _rope_kernel,
        grid=(m // tm,),
        in_specs=[pl.BlockSpec((tm, 1), lambda i: (i, 0)),
                  pl.BlockSpec((1, LANES), lambda i: (0, 0))],
        out_specs=[pl.BlockSpec((tm, LANES), lambda i: (i, 0))] * 2,
        out_shape=[jax.ShapeDtypeStruct((m, LANES), F32)] * 2,
        compiler_params=_cparams(("arbitrary",)),
        name="rope_tables",
    )(pos, inv128)


def _hg_proj_kernel(h_ref, g_ref, sc_ref, sh_ref, w_ref, q_ref, f_ref, i_ref, gg_ref, u_scr):
    j = pl.program_id(1)

    @pl.when(j == 0)
    def _():
        u = _rms(h_ref[...]) * g_ref[...] * (1.0 + sc_ref[0]) + sh_ref[0]
        u_scr[...] = u.astype(BF16)

    r = _dot(u_scr[...], w_ref[...])
    for idx, ref in enumerate((q_ref, f_ref, i_ref, gg_ref)):
        @pl.when(j == idx)
        def _(ref=ref):
            ref[...] = r.astype(ref.dtype)


def _hg_proj(h, g, scale, shift, w_in, rows_per_batch, tm=512):
    m, d = h.shape
    bpb = rows_per_batch // tm
    row = lambda i, j: (i, 0)
    mod = lambda i, j: (i // bpb, 0, 0)
    return pl.pallas_call(
        _hg_proj_kernel,
        grid=(m // tm, 4),
        in_specs=[pl.BlockSpec((tm, d), row),
                  pl.BlockSpec((1, d), lambda i, j: (0, 0)),
                  pl.BlockSpec((1, 1, d), mod),
                  pl.BlockSpec((1, 1, d), mod),
                  pl.BlockSpec((d, d), lambda i, j: (0, j))],
        out_specs=[pl.BlockSpec((tm, d), row)] * 4,
        out_shape=[jax.ShapeDtypeStruct((m, d), BF16), jax.ShapeDtypeStruct((m, d), F32),
                   jax.ShapeDtypeStruct((m, d), BF16), jax.ShapeDtypeStruct((m, d), BF16)],
        scratch_shapes=[pltpu.VMEM((tm, d), BF16)],
        compiler_params=_cparams(("arbitrary", "arbitrary")),
        name="hg_proj",
    )(h, g, scale, shift, w_in)


def _hgrn2_kernel(*refs, n_chunks, side_blocks):
    n_side = len(side_blocks)
    q_ref, f_ref, v_ref, lb_ref = refs[:4]
    o_ref = refs[4 + n_side]
    st_ref = refs[5 + 2 * n_side]
    c, sub = HG_CHUNK, HG_SUB
    n_sub = c // sub

    @pl.when(pl.program_id(2) == 0)
    def _():
        st_ref[...] = jnp.zeros_like(st_ref)

    _side_cast(_grid_step_id(3), refs[4:4 + n_side], refs[5 + n_side:5 + 2 * n_side], side_blocks)

    row = lax.broadcasted_iota(jnp.int32, (c, c), 0)
    col = lax.broadcasted_iota(jnp.int32, (c, c), 1)
    tri = (row >= col).astype(BF16)
    row2 = lax.broadcasted_iota(jnp.int32, (c, 2 * c), 0)
    col2 = lax.broadcasted_iota(jnp.int32, (c, 2 * c), 1)
    c_bits, sub_bits = c.bit_length() - 1, sub.bit_length() - 1
    mask2 = ((col2 >> c_bits) == ((row2 >> sub_bits) & 1)) & ((col2 & (c - 1)) <= row2)

    n_heads = q_ref.shape[1] // HG_KDIM
    units = [(slice(ci * c, (ci + 1) * c), slice(hh * HG_KDIM, (hh + 1) * HG_KDIM))
             for hh in range(n_heads) for ci in range(n_chunks)]
    ks, bs = [], []
    for sl, hs in units:
        lb = lb_ref[:, hs]
        f = lb + (1.0 - lb) * _sigmoid(f_ref[sl, hs])
        lf = jnp.log2(f)
        lf_hi = lf.astype(BF16)
        lf_lo = (lf - lf_hi.astype(F32)).astype(BF16)
        ks.append(1.0 - f)
        bs.append(_dot(tri, lf_hi) + _dot(tri, lf_lo))
    a_fulls, qbs, kdecs, decays = [], [], [], []
    for (sl, hs), k, b in zip(units, ks, bs):
        q = q_ref[sl, hs].astype(F32)
        b_last = b[c - 1:c, :]
        starts = [jnp.zeros_like(b_last)] + [b[j * sub - 1:j * sub, :] for j in range(1, n_sub)]
        rrow = jnp.concatenate([jnp.broadcast_to(r, (sub, r.shape[1])) for r in starts], axis=0)
        qe = q * jnp.exp2(b - rrow)
        qbs.append((qe * jnp.exp2(rrow)).astype(BF16))
        ke = jnp.concatenate(
            [k * jnp.exp2(jnp.minimum(r - b, HG_EXP2_CLAMP)) for r in starts], axis=0)
        a_fulls.append(_dot_nt(qe.astype(BF16), ke.astype(BF16)))
        kdecs.append((k * jnp.exp2(b_last - b)).astype(BF16))
        decays.append(jnp.exp2(b_last))
    o_intras, incs = [], []
    for (sl, hs), a_full, kdec in zip(units, a_fulls, kdecs):
        v = v_ref[sl, hs]
        a2 = jnp.concatenate(
            [a_full[j * sub:(j + 1) * sub, (j // 2) * 2 * c:(j // 2 + 1) * 2 * c]
             for j in range(n_sub)], axis=0)
        a2 = jnp.where(mask2, a2, 0.0).astype(BF16)
        o_intras.append(_dot(a2, jnp.concatenate([v, v], axis=0)))
        incs.append(_dot_tn(v, kdec))
    for hh in range(n_heads):
        st = st_ref[hh]
        for ui in range(hh * n_chunks, (hh + 1) * n_chunks):
            sl, hs = units[ui]
            o_ref[sl, hs] = o_intras[ui] + _dot_nt(qbs[ui], st.astype(BF16))
            st = st * decays[ui] + incs[ui]
        st_ref[hh] = st


def _hgrn2_steps(m, tb=512):
    return (HG_HEADS // HG_HEADS_PER_STEP) * (m // tb)


def _hgrn2(q, f_logit, v, lb, batch, side=(), tb=512):
    m, d = q.shape
    s = m // batch
    nt = s // tb
    hg = HG_HEADS // HG_HEADS_PER_STEP
    wd = HG_HEADS_PER_STEP * HG_KDIM
    blk = lambda b, h, t: (b * nt + t, h)
    side_specs, side_blocks = _side_plan(
        side, batch * hg * nt, lambda b, h, t: (b * hg + h) * nt + t)
    outs = pl.pallas_call(
        functools.partial(_hgrn2_kernel, n_chunks=tb // HG_CHUNK, side_blocks=side_blocks),
        grid=(batch, hg, nt),
        in_specs=[pl.BlockSpec((tb, wd), blk)] * 3
        + [pl.BlockSpec((1, wd), lambda b, h, t: (0, h))] + side_specs,
        out_specs=[pl.BlockSpec((tb, wd), blk)] + side_specs,
        out_shape=[jax.ShapeDtypeStruct((m, d), F32)]
        + [jax.ShapeDtypeStruct(w.shape, BF16) for w in side],
        scratch_shapes=[pltpu.VMEM((HG_HEADS_PER_STEP, HG_KDIM, HG_KDIM), F32)],
        compiler_params=_cparams(("arbitrary", "arbitrary", "arbitrary")),
        name="hgrn2_recurrence",
    )(q, f_logit, v, lb, *side)
    return outs[0], list(outs[1:])


def _oproj_kernel(*refs, gated):
    if gated:
        (o_ref, gg_ref, gout_ref, w_ref, h_ref, gate_ref, g2_ref, gn_ref, scn_ref, shn_ref,
         out_ref, un_ref) = refs
        gg = gg_ref[...].astype(F32)
        x = _rms(o_ref[...]) * gout_ref[...] * (gg * _sigmoid(gg))
    else:
        o_ref, w_ref, h_ref, gate_ref, g2_ref, out_ref = refs
        x = o_ref[...]
    y = _dot(x.astype(BF16), w_ref[...])
    h_new = h_ref[...] + gate_ref[0] * (_rms(y) * g2_ref[...])
    out_ref[...] = h_new
    if gated:
        un_ref[...] = (_rms(h_new) * gn_ref[...] * (1.0 + scn_ref[0]) + shn_ref[0]).astype(BF16)


def _oproj(o, w, h, gate, g2, rows_per_batch, gg=None, gout=None, nxt=None, tm=512):
    m, d = h.shape
    bpb = rows_per_batch // tm
    row = pl.BlockSpec((tm, d), lambda i: (i, 0))
    vec = pl.BlockSpec((1, d), lambda i: (0, 0))
    mod = pl.BlockSpec((1, 1, d), lambda i: (i // bpb, 0, 0))
    wsp = pl.BlockSpec(w.shape, lambda i: (0, 0), pipeline_mode=pl.Buffered(1))
    gated = gg is not None
    if gated:
        args = (o, gg, gout, w, h, gate, g2) + tuple(nxt)
        specs = [row, row, vec, wsp, row, mod, vec, vec, mod, mod]
        out_specs = [row, row]
        out_shape = [jax.ShapeDtypeStruct((m, d), F32), jax.ShapeDtypeStruct((m, d), BF16)]
    else:
        args, specs = (o, w, h, gate, g2), [row, wsp, row, mod, vec]
        out_specs, out_shape = row, jax.ShapeDtypeStruct((m, d), F32)
    return pl.pallas_call(
        functools.partial(_oproj_kernel, gated=gated),
        grid=(m // tm,),
        in_specs=specs,
        out_specs=out_specs,
        out_shape=out_shape,
        compiler_params=_cparams(("arbitrary",)),
        name="hg_out_proj" if gated else "mla_out_proj",
    )(*args)


def _ffn_kernel(*refs, side_blocks):
    n_side = len(side_blocks)
    u_ref, wg_ref, wu_ref, wd_ref = refs[:4]
    y_ref = refs[4 + n_side]

    @pl.when(pl.program_id(1) == 0)
    def _():
        y_ref[...] = jnp.zeros_like(y_ref)

    _side_cast(_grid_step_id(2), refs[4:4 + n_side], refs[5 + n_side:5 + 2 * n_side], side_blocks)
    u = u_ref[...]
    gt = _dot(u, wg_ref[...])
    up = _dot(u, wu_ref[...])
    a = (gt * _sigmoid(gt) * up).astype(BF16)
    y_ref[...] += _dot(a, wd_ref[...])


def _ffn_steps(m, ff, tm=512, tf=FFN_TF):
    return (m // tm) * (ff // tf)


def _ffn(u, w_gu, w_down, side=(), tm=512, tf=FFN_TF):
    m, d = u.shape
    ff = w_down.shape[0]
    nf = ff // tf
    row = pl.BlockSpec((tm, d), lambda i, j: (i, 0))
    side_specs, side_blocks = _side_plan(side, (m // tm) * nf, lambda i, j: i * nf + j)
    outs = pl.pallas_call(
        functools.partial(_ffn_kernel, side_blocks=side_blocks),
        grid=(m // tm, nf),
        in_specs=[row,
                  pl.BlockSpec((d, tf), lambda i, j: (0, j)),
                  pl.BlockSpec((d, tf), lambda i, j: (0, nf + j)),
                  pl.BlockSpec((tf, d), lambda i, j: (j, 0))] + side_specs,
        out_specs=[row] + side_specs,
        out_shape=[jax.ShapeDtypeStruct((m, d), F32)]
        + [jax.ShapeDtypeStruct(w.shape, BF16) for w in side],
        compiler_params=_cparams(("arbitrary", "arbitrary")),
        name="dense_ffn",
    )(u, w_gu, w_gu, w_down, *side)
    return outs[0], list(outs[1:])


def _mla_proj_kernel(h_ref, y_ref, gate_ref, g2_ref, gq_ref, scq_ref, shq_ref, gk_ref, sck_ref, shk_ref,
                     wqa_ref, qg_ref, wqm_ref, wqr_ref,
                     wkva_ref, wkr_ref, wkrr_ref, kg_ref, wk_ref, wv_ref,
                     cos_ref, sin_ref, hn_ref, q_ref, k_ref, v_ref):
    h_new = h_ref[...] + gate_ref[0] * (_rms(y_ref[...]) * g2_ref[...])
    hn_ref[...] = h_new
    xhat = _rms(h_new)
    uq = (xhat * gq_ref[...] * (1.0 + scq_ref[0]) + shq_ref[0]).astype(BF16)
    xk = (xhat * gk_ref[...] * (1.0 + sck_ref[0]) + shk_ref[0]).astype(BF16)
    cos = cos_ref[...]
    sin = sin_ref[...]
    qn = (_rms(_dot(uq, wqa_ref[...])) * qg_ref[...]).astype(BF16)
    cn = (_rms(_dot(xk, wkva_ref[...])) * kg_ref[...]).astype(BF16)
    krope = (_dot(xk, wkr_ref[...]) * cos + _dot(xk, wkrr_ref[...]) * sin).astype(BF16)
    ones = jnp.ones((cn.shape[0], V_HEAD), BF16)
    for hp in range(MLA_HEADS // 2):
        ps = slice(hp * 2 * LANES, (hp + 1) * 2 * LANES)
        qr2 = _dot(qn, wqr_ref[:, ps]) * Q_SCALE
        kn2 = _dot(cn, wk_ref[:, ps]).astype(BF16)
        v2 = _dot(cn, wv_ref[:, ps]).astype(BF16)
        for sub in range(2):
            hd = 2 * hp + sub
            ls = slice(sub * LANES, (sub + 1) * LANES)
            qm = _dot(qn, wqm_ref[:, hd * QK_PAD:(hd + 1) * QK_PAD]) * Q_SCALE
            q_ref[:, hd * QK_PAD:hd * QK_PAD + QK_NOPE] = qm[:, :QK_NOPE].astype(BF16)
            q_ref[:, hd * QK_PAD + QK_NOPE:(hd + 1) * QK_PAD] = (
                qm[:, QK_NOPE:] * cos + qr2[:, ls] * sin).astype(BF16)
            k_ref[:, hd * QK_PAD:hd * QK_PAD + QK_NOPE] = kn2[:, ls]
            k_ref[:, hd * QK_PAD + QK_NOPE:(hd + 1) * QK_PAD] = krope
            v_ref[:, 2 * hd * V_HEAD:(2 * hd + 1) * V_HEAD] = v2[:, ls]
            v_ref[:, (2 * hd + 1) * V_HEAD:(2 * hd + 2) * V_HEAD] = ones


def _mla_proj(h, y, gate, g2, gq, scq, shq, gk, sck, shk, wts, cos, sin, rows_per_batch, tm=256):
    m, d = h.shape
    bpb = rows_per_batch // tm
    row = lambda w: pl.BlockSpec((tm, w), lambda i: (i, 0))
    vec = lambda w: pl.BlockSpec((1, w), lambda i: (0, 0))
    mod = pl.BlockSpec((1, 1, d), lambda i: (i // bpb, 0, 0))
    full = lambda a: pl.BlockSpec(a.shape, lambda i: (0, 0), pipeline_mode=pl.Buffered(1))
    wqa, qg, wqm, wqr, wkva, wkr, wkrr, kg, wk, wv = wts
    return pl.pallas_call(
        _mla_proj_kernel,
        grid=(m // tm,),
        in_specs=[row(d), row(d), mod, vec(d), vec(d), mod, mod, vec(d), mod, mod,
                  full(wqa), full(qg), full(wqm), full(wqr),
                  full(wkva), full(wkr), full(wkrr), full(kg), full(wk), full(wv),
                  row(LANES), row(LANES)],
        out_specs=[row(d), row(MLA_HEADS * QK_PAD), row(MLA_HEADS * QK_PAD),
                   row(MLA_HEADS * 2 * V_HEAD)],
        out_shape=[jax.ShapeDtypeStruct((m, d), F32),
                   jax.ShapeDtypeStruct((m, MLA_HEADS * QK_PAD), BF16),
                   jax.ShapeDtypeStruct((m, MLA_HEADS * QK_PAD), BF16),
                   jax.ShapeDtypeStruct((m, MLA_HEADS * 2 * V_HEAD), BF16)],
        compiler_params=_cparams(("arbitrary",)),
        name="mla_proj",
    )(h, y, gate, g2, gq, scq, shq, gk, sck, shk, wqa, qg, wqm, wqr, wkva, wkr, wkrr, kg, wk, wv,
      cos, sin)


def _flash_kernel(*refs, tq, tk, slab, side_blocks):
    n_side = len(side_blocks)
    q_ref, k_ref, v_ref = refs[:3]
    o_ref = refs[3 + n_side]
    s_a, s_b, p_scr, m_scr, al_scr, acc_scr = refs[4 + 2 * n_side:]
    qi = pl.program_id(2)
    _side_cast(_grid_step_id(3), refs[3:3 + n_side], refs[4 + n_side:4 + 2 * n_side], side_blocks)

    s_bufs = (s_a, s_b)
    m_scr[...] = jnp.full_like(m_scr, -jnp.inf)
    acc_scr[...] = jnp.zeros_like(acc_scr)
    n_col = tk // LANES
    n_diag = tq // tk

    def kv_rows(j):
        return pl.ds(pl.multiple_of(j * tk, tk), tk)

    def scores(j, dst, row0):
        dst[row0:, :] = _dot_nt(q_ref[row0:, :], k_ref[kv_rows(j), :])

    def consume(j, src, row0, diag):
        for r0 in range(row0, tq, slab):
            rows = slice(r0, r0 + slab)
            cols = [src[rows, cb * LANES:(cb + 1) * LANES] for cb in range(n_col)]
            if diag and r0 < row0 + tk:
                rowp = (r0 - row0) + lax.broadcasted_iota(jnp.int32, (slab, LANES), 0)
                lane = lax.broadcasted_iota(jnp.int32, (slab, LANES), 1)
                cols = [jnp.where(cb * LANES + lane <= rowp, cols[cb], -jnp.inf)
                        for cb in range(n_col)]
            mx = functools.reduce(jnp.maximum, cols)
            m_old = m_scr[rows, :]
            m_new = jnp.maximum(m_old, jnp.max(mx, axis=-1, keepdims=True))
            al_scr[rows, :] = jnp.exp2(m_old - m_new)
            m_scr[rows, :] = m_new
            for cb in range(n_col):
                p_scr[rows, cb * LANES:(cb + 1) * LANES] = jnp.exp2(cols[cb] - m_new).astype(BF16)
        pv = _dot(p_scr[row0:, :], v_ref[kv_rows(j), :])
        al = al_scr[row0:, :]
        acc_scr[row0:, :V_HEAD] = acc_scr[row0:, :V_HEAD] * al + pv[:, :V_HEAD]
        acc_scr[row0:, V_HEAD:] = acc_scr[row0:, V_HEAD:] * al + pv[:, V_HEAD:]

    n_full = qi * n_diag
    scores(0, s_a, 0)

    def pair(pi, carry):
        for u in range(2):
            scores(2 * pi + u + 1, s_bufs[1 - u], 0)
            consume(2 * pi + u, s_bufs[u], 0, False)
        return carry

    lax.fori_loop(0, n_full // 2, pair, 0)
    for dg in range(n_diag):
        if dg + 1 < n_diag:
            scores(n_full + dg + 1, s_bufs[(dg + 1) % 2], (dg + 1) * tk)
        consume(n_full + dg, s_bufs[dg % 2], dg * tk, True)

    o_ref[...] = (acc_scr[:, :V_HEAD] / acc_scr[:, V_HEAD:]).astype(o_ref.dtype)


def _flash_tiles(seq):
    tq = min(FLASH_TQ, seq)
    return tq, min(FLASH_TK, tq // 2)


def _flash(q, k, v, batch, side=(), slab=64):
    m = q.shape[0]
    s = m // batch
    tq, tk = _flash_tiles(s)
    nq = s // tq
    side_specs, side_blocks = _side_plan(
        side, batch * MLA_HEADS * nq, lambda b, h, i: (b * MLA_HEADS + h) * nq + i)
    outs = pl.pallas_call(
        functools.partial(_flash_kernel, tq=tq, tk=tk, slab=slab, side_blocks=side_blocks),
        grid=(batch, MLA_HEADS, nq),
        in_specs=[pl.BlockSpec((tq, QK_PAD), lambda b, h, i: (b * nq + i, h)),
                  pl.BlockSpec((s, QK_PAD), lambda b, h, i: (b, h)),
                  pl.BlockSpec((s, 2 * V_HEAD), lambda b, h, i: (b, h))] + side_specs,
        out_specs=[pl.BlockSpec((tq, V_HEAD), lambda b, h, i: (b * nq + i, h))] + side_specs,
        out_shape=[jax.ShapeDtypeStruct((m, MLA_HEADS * V_HEAD), BF16)]
        + [jax.ShapeDtypeStruct(w.shape, BF16) for w in side],
        scratch_shapes=[pltpu.VMEM((tq, tk), F32), pltpu.VMEM((tq, tk), F32),
                        pltpu.VMEM((tq, tk), BF16), pltpu.VMEM((tq, LANES), F32),
                        pltpu.VMEM((tq, LANES), F32), pltpu.VMEM((tq, 2 * V_HEAD), F32)],
        compiler_params=_cparams(("arbitrary", "arbitrary", "arbitrary")),
        name="mla_flash",
    )(q, k, v, *side)
    return outs[0], list(outs[1:])


def _split3(x):
    hi = x.astype(BF16)
    lo = (x - hi.astype(F32)).astype(BF16)
    return hi, lo


def _moe_pre_kernel(h_ref, g_ref, sc_ref, sh_ref, wr_ref, xs_ref, meta_ref, wts_ref, cnt_ref,
                    run_scr):
    tm, d = h_ref.shape

    @pl.when(pl.program_id(0) == 0)
    def _():
        run_scr[...] = jnp.zeros_like(run_scr)

    u = _rms(h_ref[...]) * g_ref[...] * (1.0 + sc_ref[0]) + sh_ref[0]
    xs_ref[...] = u

    u_hi, u_lo = _split3(u)
    w_hi, w_lo = _split3(wr_ref[...])
    logits = _dot(u_hi, w_hi) + (_dot(u_hi, w_lo) + _dot(u_lo, w_hi))
    lane = lax.broadcasted_iota(jnp.int32, logits.shape, 1)
    lg = jnp.where(lane < N_EXPERTS, logits, -jnp.inf)
    m1 = jnp.max(lg, axis=-1, keepdims=True)
    i1 = jnp.min(jnp.where(lg == m1, lane, LANES), axis=-1, keepdims=True)
    lg2 = jnp.where(lane == i1, -jnp.inf, lg)
    m2 = jnp.max(lg2, axis=-1, keepdims=True)
    i2 = jnp.min(jnp.where(lg2 == m2, lane, LANES), axis=-1, keepdims=True)
    e = jnp.exp(m2 - m1)
    w1 = 1.0 / (1.0 + e)
    w2 = e * w1

    oh1 = lane == i1
    oh2 = lane == i2
    cnt = oh1.astype(F32) + oh2.astype(F32)
    r = lax.broadcasted_iota(jnp.int32, (tm, tm), 0)
    c = lax.broadcasted_iota(jnp.int32, (tm, tm), 1)
    strict = (r > c).astype(BF16)
    prefix = _dot(strict, cnt.astype(BF16)) + run_scr[...]
    rank1 = jnp.sum(jnp.where(oh1, prefix, 0.0), axis=-1, keepdims=True).astype(jnp.int32)
    rank2 = jnp.sum(jnp.where(oh2, prefix, 0.0), axis=-1, keepdims=True).astype(jnp.int32)
    run_scr[...] += jnp.sum(cnt, axis=0, keepdims=True)

    meta_ref[...] = jnp.where(lane == 0, i1, jnp.where(lane == 1, i2,
                              jnp.where(lane == 2, rank1, jnp.where(lane == 3, rank2, 0))))
    wts_ref[...] = jnp.where(lane == 0, w1, jnp.where(lane == 1, w2, 0.0))
    cnt_ref[...] = run_scr[...]


def _moe_pre(h, g, scale, shift, w_router_pad, rows_per_batch, tm=512):
    m, d = h.shape
    bpb = rows_per_batch // tm
    row = lambda w: pl.BlockSpec((tm, w), lambda i: (i, 0))
    vec = lambda w: pl.BlockSpec((1, w), lambda i: (0, 0))
    mod = pl.BlockSpec((1, 1, d), lambda i: (i // bpb, 0, 0))
    return pl.pallas_call(
        _moe_pre_kernel,
        grid=(m // tm,),
        in_specs=[row(d), vec(d), mod, mod, pl.BlockSpec((d, LANES), lambda i: (0, 0))],
        out_specs=[row(d), row(LANES), row(LANES), vec(LANES)],
        out_shape=[jax.ShapeDtypeStruct((m, d), F32),
                   jax.ShapeDtypeStruct((m, LANES), jnp.int32),
                   jax.ShapeDtypeStruct((m, LANES), F32),
                   jax.ShapeDtypeStruct((1, LANES), F32)],
        scratch_shapes=[pltpu.VMEM((1, LANES), F32)],
        compiler_params=_cparams(("arbitrary",)),
        name="moe_route",
    )(h, g, scale, shift, w_router_pad)


ROW_COPY_UNROLL = 4


def _issue_row_copies(n_rows, copies_of_row):
    def trip(t, carry):
        n = 0
        for u in range(ROW_COPY_UNROLL):
            for cp in copies_of_row(t * ROW_COPY_UNROLL + u):
                cp.start(priority=n % 2)
                n += 1
        return carry
    lax.fori_loop(0, n_rows // ROW_COPY_UNROLL, trip, 0)


def _moe_dispatch_kernel(fill_ref, dest_ref, x_ref, xs_ref, zbuf, sem, zsem, *, tmb, n_tail):
    tm = x_ref.shape[0]

    @pl.when(pl.program_id(0) == 0)
    def _():
        zbuf[...] = jnp.zeros_like(zbuf)

        def pad_row(e, r):
            return pltpu.make_async_copy(zbuf.at[pl.ds(0, 1), :],
                                         xs_ref.at[pl.ds(fill_ref[2 * e] + r, 1), :], zsem)

        def tail_block(b):
            start = pl.multiple_of(fill_ref[2 * N_EXPERTS] + b * tmb, tmb)
            return pltpu.make_async_copy(zbuf, xs_ref.at[pl.ds(start, tmb), :], zsem)

        for e in range(N_EXPERTS):
            def start_row(r, carry, e=e):
                pad_row(e, r).start()
                return carry
            lax.fori_loop(0, fill_ref[2 * e + 1], start_row, 0)
        n_blk = fill_ref[2 * N_EXPERTS + 1] // tmb
        for b in range(n_tail):
            @pl.when(b < n_blk)
            def _(b=b):
                tail_block(b).start()
        for e in range(N_EXPERTS):
            def wait_row(r, carry, e=e):
                pad_row(e, r).wait()
                return carry
            lax.fori_loop(0, fill_ref[2 * e + 1], wait_row, 0)
        for b in range(n_tail):
            @pl.when(b < n_blk)
            def _(b=b):
                tail_block(b).wait()

    def copies_of_row(r):
        return [pltpu.make_async_copy(x_ref.at[pl.ds(r, 1), :],
                                      xs_ref.at[pl.ds(dest_ref[0, 0, 2 * r + k], 1), :], sem)
                for k in range(2)]

    _issue_row_copies(tm, copies_of_row)
    for _ in range(2):
        pltpu.make_async_copy(x_ref, x_ref, sem).wait()


def _moe_dispatch(x, dest, fill, n_rows, tmb, tm=256):
    m, w = x.shape
    nb = m // tm
    n_tail = (n_rows - 2 * m) // tmb
    return pl.pallas_call(
        functools.partial(_moe_dispatch_kernel, tmb=tmb, n_tail=n_tail),
        grid_spec=pltpu.PrefetchScalarGridSpec(
            num_scalar_prefetch=1,
            grid=(nb,),
            in_specs=[pl.BlockSpec((1, 1, 2 * tm), lambda i, fl: (i, 0, 0), memory_space=pltpu.SMEM),
                      pl.BlockSpec((tm, w), lambda i, fl: (i, 0))],
            out_specs=pl.BlockSpec(memory_space=pl.ANY),
            scratch_shapes=[pltpu.VMEM((tmb, w), F32), pltpu.SemaphoreType.DMA(()),
                            pltpu.SemaphoreType.DMA(())],
        ),
        out_shape=jax.ShapeDtypeStruct((n_rows, w), F32),
        compiler_params=_cparams(("arbitrary",)),
        name="moe_dispatch",
    )(fill, dest.reshape(nb, 1, 2 * tm), x)


def _moe_ffn_kernel(blk_e_ref, n_used_ref, xs_ref, wg_ref, wu_ref, wd_ref, y_ref, u_scr):
    i = pl.program_id(0)
    j = pl.program_id(1)

    @pl.when(i < n_used_ref[0])
    def _():
        @pl.when(j == 0)
        def _():
            u_scr[...] = xs_ref[...].astype(BF16)
            y_ref[...] = jnp.zeros_like(y_ref)

        u = u_scr[...]
        gt = _dot(u, wg_ref[0])
        up = _dot(u, wu_ref[0])
        a = (gt * _sigmoid(gt) * up).astype(BF16)
        y_ref[...] += _dot(a, wd_ref[0])

    @pl.when((i >= n_used_ref[0]) & (j == 0))
    def _():
        y_ref[...] = jnp.zeros_like(y_ref)


def _moe_ffn(xs, blk_e, n_used, w_gu, w_down, tmb, tf=FFN_TF):
    n_rows, d = xs.shape
    ff = w_down.shape[1]
    nf = ff // tf
    nb = n_rows // tmb

    def blk(i, n_used_ref):
        return jnp.minimum(i, n_used_ref[0] - 1)

    def fidx(i, j, n_used_ref):
        return jnp.where(i < n_used_ref[0], j, nf - 1)

    grid_spec = pltpu.PrefetchScalarGridSpec(
        num_scalar_prefetch=2,
        grid=(nb, nf),
        in_specs=[
            pl.BlockSpec((tmb, d), lambda i, j, be, nu: (blk(i, nu), 0)),
            pl.BlockSpec((1, d, tf), lambda i, j, be, nu: (be[blk(i, nu)], 0, fidx(i, j, nu))),
            pl.BlockSpec((1, d, tf), lambda i, j, be, nu: (be[blk(i, nu)], 0, nf + fidx(i, j, nu))),
            pl.BlockSpec((1, tf, d), lambda i, j, be, nu: (be[blk(i, nu)], fidx(i, j, nu), 0)),
        ],
        out_specs=pl.BlockSpec((tmb, d), lambda i, j, be, nu: (i, 0)),
        scratch_shapes=[pltpu.VMEM((tmb, d), BF16)],
    )
    return pl.pallas_call(
        _moe_ffn_kernel,
        grid_spec=grid_spec,
        out_shape=jax.ShapeDtypeStruct((n_rows, d), F32),
        compiler_params=_cparams(("arbitrary", "arbitrary")),
        name="moe_grouped_ffn",
    )(blk_e, n_used, xs, w_gu, w_gu, w_down)


def _moe_combine_kernel(dest_ref, ys_ref, wts_ref, h_ref, gate_ref, g2_ref, out_ref,
                        buf0, buf1, sem):
    tm = h_ref.shape[0]
    bufs = (buf0, buf1)

    def copies_of_row(r):
        return [pltpu.make_async_copy(ys_ref.at[pl.ds(dest_ref[0, 0, 2 * r + k], 1), :],
                                      bufs[k].at[pl.ds(r, 1), :], sem)
                for k in range(2)]

    _issue_row_copies(tm, copies_of_row)
    for k in range(2):
        pltpu.make_async_copy(bufs[k], bufs[k], sem).wait()

    wts = wts_ref[...]
    y = wts[:, 0:1] * buf0[...] + wts[:, 1:2] * buf1[...]
    out_ref[...] = h_ref[...] + gate_ref[0] * (_rms(y) * g2_ref[...])


def _moe_combine(ys, dest, wts, h, gate, g2, rows_per_batch, tm=256):
    m, d = h.shape
    nb = m // tm
    bpb = rows_per_batch // tm
    row = lambda w: pl.BlockSpec((tm, w), lambda i: (i, 0))
    return pl.pallas_call(
        _moe_combine_kernel,
        grid=(nb,),
        in_specs=[pl.BlockSpec((1, 1, 2 * tm), lambda i: (i, 0, 0), memory_space=pltpu.SMEM),
                  pl.BlockSpec(memory_space=pl.ANY),
                  row(LANES), row(d),
                  pl.BlockSpec((1, 1, d), lambda i: (i // bpb, 0, 0)),
                  pl.BlockSpec((1, d), lambda i: (0, 0))],
        out_specs=row(d),
        out_shape=jax.ShapeDtypeStruct((m, d), F32),
        scratch_shapes=[pltpu.VMEM((tm, d), F32), pltpu.VMEM((tm, d), F32),
                        pltpu.SemaphoreType.DMA(())],
        compiler_params=_cparams(("arbitrary",)),
        name="moe_combine",
    )(dest.reshape(nb, 1, 2 * tm), ys, wts, h, gate, g2)


def _moe(h, g, scale, shift, w_router, w_gu, w_down, gate, g2, rows_per_batch, tmb=MOE_BLOCK_ROWS):
    m, d = h.shape
    n_rows = (2 * m + N_EXPERTS * (tmb - 1)) // tmb * tmb
    nb = n_rows // tmb
    w_router_pad = jnp.pad(w_router, ((0, 0), (0, LANES - N_EXPERTS)))
    u, meta, wts, cnt = _moe_pre(h, g, scale, shift, w_router_pad, rows_per_batch)

    counts = cnt[0, :N_EXPERTS].astype(jnp.int32)
    padded = (counts + tmb - 1) // tmb * tmb
    pend = jnp.cumsum(padded)
    pstart = pend - padded
    dest = (pstart[meta[:, 0:2]] + meta[:, 2:4]).astype(jnp.int32).reshape(-1)
    blk_start = jnp.arange(nb, dtype=jnp.int32) * tmb
    blk_e = jnp.minimum(jnp.sum(pend[None, :] <= blk_start[:, None], axis=1),
                        N_EXPERTS - 1).astype(jnp.int32)
    n_used = (pend[-1:] // tmb).astype(jnp.int32)

    fill = jnp.stack([jnp.concatenate([pstart + counts, pend[-1:]]),
                      jnp.concatenate([padded - counts, n_rows - pend[-1:]])],
                     axis=1).reshape(-1).astype(jnp.int32)
    xs = _moe_dispatch(u, dest, fill, n_rows, tmb)
    ys = _moe_ffn(xs, blk_e, n_used, w_gu, w_down, tmb)
    return _moe_combine(ys, dest, wts, h, gate, g2, rows_per_batch)


def _swap_halves(w):
    half = w.shape[-1] // 2
    return jnp.concatenate([w[..., half:], w[..., :half]], axis=-1)


def _mla_weights(w_q_a, q_norm_g, w_q_b, w_kv_a, kv_norm_g, w_kv_b):
    ql = w_q_b.shape[0]
    kvl = w_kv_b.shape[0]
    wq = w_q_b.reshape(ql, MLA_HEADS, QK_NOPE + QK_ROPE)
    rope = wq[:, :, QK_NOPE:]
    pad_r = LANES - QK_ROPE
    wqm = jnp.concatenate([wq, jnp.zeros((ql, MLA_HEADS, QK_PAD - QK_NOPE - QK_ROPE), F32)], axis=-1)
    wqr = jnp.concatenate([_swap_halves(rope), jnp.zeros((ql, MLA_HEADS, pad_r), F32)], axis=-1)
    wkv = w_kv_b.reshape(kvl, MLA_HEADS, QK_NOPE + V_HEAD)
    kr = w_kv_a[:, kvl:]
    dm = w_kv_a.shape[0]
    wkr = jnp.concatenate([kr, jnp.zeros((dm, pad_r), F32)], axis=-1)
    wkrr = jnp.concatenate([_swap_halves(kr), jnp.zeros((dm, pad_r), F32)], axis=-1)
    bf = lambda a: a.astype(BF16)
    return (bf(w_q_a), q_norm_g.reshape(1, -1), bf(wqm.reshape(ql, -1)), bf(wqr.reshape(ql, -1)),
            bf(w_kv_a[:, :kvl]), bf(wkr), bf(wkrr), kv_norm_g.reshape(1, -1),
            bf(wkv[:, :, :QK_NOPE].reshape(kvl, -1)), bf(wkv[:, :, QK_NOPE:].reshape(kvl, -1)))


def kernel(x, c, positions, ada_w, ada_b, norm_g, hg_w_in, hg_lb_logits, hg_out_norm_g, hg_w_out, kv_src_norm_g, kv_src_ada_w, kv_src_ada_b, mla_w_kv_a, mla_kv_norm_g, mla_w_kv_b, mla_w_q_a, mla_q_norm_g, mla_w_q_b, mla_w_o, ffn_w_gu, ffn_w_down, moe_w_router, moe_w_gu, moe_w_down):
    batch, seq, d = x.shape
    m = batch * seq
    bf = lambda a: a.astype(BF16)

    c8 = jnp.pad(c, ((0, 8 - batch), (0, 0)))
    ada = _modulation(c8, ada_w.reshape(4, d, 3 * d), ada_b.reshape(4, 3 * d))[:, :batch]
    kvm = _modulation(c8, kv_src_ada_w[None], kv_src_ada_b[None])[0, :batch]

    def mods(idx):
        a = ada[idx]
        return [a[:, i * d:(i + 1) * d].reshape(batch, 1, d) for i in range(3)]

    vec = lambda a: a.reshape(1, -1)
    lb_all = jnp.cumsum(jax.nn.softmax(hg_lb_logits.astype(F32), axis=0), axis=0)

    h = x.reshape(m, d)

    shift, scale, gate = mods(0)
    q, f_logit, iv, gg = _hg_proj(h, vec(norm_g[0, 0, 0]), scale, shift, bf(hg_w_in[0]), seq)
    o, (ffn_gu_bf, ffn_down_bf, hg_out_bf, moe_down_bf) = _with_side(
        lambda side: _hgrn2(q, f_logit, iv, vec(lb_all[0]), batch, side=side),
        [ffn_w_gu[0], ffn_w_down[0], hg_w_out[0], moe_w_down[0]], _hgrn2_steps(m))
    shift_f, scale_f, gate_f = mods(1)
    h, u_ffn = _oproj(o, hg_out_bf, h, gate, vec(norm_g[0, 0, 1]), seq,
                      gg=gg, gout=vec(hg_out_norm_g[0]),
                      nxt=(vec(norm_g[0, 1, 0]), scale_f, shift_f))

    y_ffn, (moe_gu_bf, mla_o_bf) = _with_side(
        lambda side: _ffn(u_ffn, ffn_gu_bf, ffn_down_bf, side=side),
        [moe_w_gu[0], mla_w_o[0]], _ffn_steps(m, ffn_w_down.shape[1]))

    shift_k, scale_k = [kvm[:, i * d:(i + 1) * d].reshape(batch, 1, d) for i in range(2)]
    shift, scale, gate = mods(2)
    cos, sin = _rope_tables(positions.reshape(m, 1).astype(F32))
    wts = _mla_weights(mla_w_q_a[0], mla_q_norm_g[0], mla_w_q_b[0],
                       mla_w_kv_a, mla_kv_norm_g, mla_w_kv_b)
    h, qh, kh, vh = _mla_proj(h, y_ffn, gate_f, vec(norm_g[0, 1, 1]),
                              vec(norm_g[1, 0, 0]), scale, shift,
                              vec(kv_src_norm_g), scale_k, shift_k, wts, cos, sin, seq)
    att, _ = _flash(qh, kh, vh, batch)
    h = _oproj(att, mla_o_bf, h, gate, vec(norm_g[1, 0, 1]), seq)

    shift, scale, gate = mods(3)
    h = _moe(h, vec(norm_g[1, 1, 0]), scale, shift, moe_w_router[0],
             moe_gu_bf, moe_down_bf, gate, vec(norm_g[1, 1, 1]), seq)
    return h.reshape(batch, seq, d)
```

```python
import functools

import jax
import jax.numpy as jnp
from jax import lax
from jax.experimental import pallas as pl
from jax.experimental.pallas import tpu as pltpu

F32 = jnp.float32
BF16 = jnp.bfloat16

EPS = 1e-6
LANES = 128
F32_SUBLANES = 8
BF16_SUBLANES = 16
SIDE_BLOCK_BYTES = 3 * 1024 * 1024

HG_HEADS = 16
HG_KDIM = 128
HG_CHUNK = 64
HG_SUB = 16
HG_EXP2_CLAMP = 115.0
HG_HEADS_PER_STEP = 4

MLA_HEADS = 16
QK_NOPE = 128
QK_ROPE = 64
V_HEAD = 128
QK_PAD = 256
MLA_SCALE = (QK_NOPE + QK_ROPE) ** -0.5
Q_SCALE = MLA_SCALE * 1.4426950408889634
ROPE_THETA = 10000.0
FLASH_TQ = 2048
FLASH_TK = 1024

N_EXPERTS = 8
MOE_BLOCK_ROWS = 512
FFN_TF = 512

VMEM_LIMIT = 56 * 1024 * 1024


def _cparams(sem):
    return pltpu.CompilerParams(dimension_semantics=sem, vmem_limit_bytes=VMEM_LIMIT)


def _rms(x):
    return x * lax.rsqrt(jnp.mean(x * x, axis=-1, keepdims=True) + EPS)


def _sigmoid(x):
    return 1.0 / (1.0 + jnp.exp(-x))


def _dot(a, b):
    return jnp.dot(a, b, preferred_element_type=F32)


def _dot_nt(a, b):
    return lax.dot_general(a, b, (((1,), (1,)), ((), ())), preferred_element_type=F32)


def _dot_tn(a, b):
    return lax.dot_general(a, b, (((0,), (0,)), ((), ())), preferred_element_type=F32)


def _side_rows(w, n_steps):
    e, r, c = w.shape
    for rb in range(BF16_SUBLANES, r + 1, BF16_SUBLANES):
        if r % rb == 0 and e * (r // rb) <= n_steps:
            return rb if rb * c * 4 <= SIDE_BLOCK_BYTES else None
    return None


def _side_ok(weights, n_steps):
    return all(_side_rows(w, n_steps) is not None for w in weights)


def _side_plan(weights, n_steps, step_of):
    specs, blocks = [], []
    for w in weights:
        rb = _side_rows(w, n_steps)
        per = w.shape[1] // rb
        n_blk = w.shape[0] * per
        blocks.append(n_blk)

        def idx(*grid_ids, per=per, n_blk=n_blk):
            blk = jnp.minimum(step_of(*grid_ids), n_blk - 1)
            return (blk // per, blk % per, 0)

        specs.append(pl.BlockSpec((1, rb, w.shape[2]), idx))
    return specs, tuple(blocks)


def _side_cast(step_id, w_ins, w_outs, blocks):
    for w_in, w_out, n_blk in zip(w_ins, w_outs, blocks):
        @pl.when(step_id < n_blk)
        def _(w_in=w_in, w_out=w_out):
            w_out[...] = w_in[...].astype(w_out.dtype)


def _grid_step_id(n_axes):
    sid = pl.program_id(0)
    for ax in range(1, n_axes):
        sid = sid * pl.num_programs(ax) + pl.program_id(ax)
    return sid


def _with_side(run, weights, n_steps):
    w3 = tuple(w.reshape((1,) + w.shape) if w.ndim == 2 else w for w in weights)
    if _side_ok(w3, n_steps):
        res, copies = run(w3)
        return res, [cp.reshape(w.shape) for cp, w in zip(copies, weights)]
    res, _ = run(())
    return res, [w.astype(BF16) for w in weights]


def _mod_kernel(c_ref, w_ref, b_ref, o_ref):
    c = c_ref[...]
    sc = (c * _sigmoid(c)).astype(BF16)
    o_ref[0] = _dot(sc, w_ref[0].astype(BF16)) + b_ref[0]


def _modulation(c8, w, b, tn=1024):
    g, d, n = w.shape
    return pl.pallas_call(
        _mod_kernel,
        grid=(g, n // tn),
        in_specs=[pl.BlockSpec((8, d), lambda i, j: (0, 0)),
                  pl.BlockSpec((1, d, tn), lambda i, j: (i, 0, j)),
                  pl.BlockSpec((1, 1, tn), lambda i, j: (i, 0, j))],
        out_specs=pl.BlockSpec((1, 8, tn), lambda i, j: (i, 0, j)),
        out_shape=jax.ShapeDtypeStruct((g, 8, n), F32),
        compiler_params=_cparams(("arbitrary", "arbitrary")),
        name="modulation",
    )(c8, w, b.reshape(g, 1, n))


def _rope_kernel(pos_ref, inv_ref, cos_ref, sin_ref):
    ang = pos_ref[...] * inv_ref[...]
    lane = lax.broadcasted_iota(jnp.int32, ang.shape, 1)
    half = QK_ROPE // 2
    cos_ref[...] = jnp.where(lane < QK_ROPE, jnp.cos(ang), 0.0)
    sin_ref[...] = jnp.where(lane < half, -jnp.sin(ang),
                             jnp.where(lane < QK_ROPE, jnp.sin(ang), 0.0))


def _rope_tables(pos, tm=512):
    m = pos.shape[0]
    half = QK_ROPE // 2
    inv = 1.0 / (ROPE_THETA ** (jnp.arange(0, QK_ROPE, 2, dtype=F32) / QK_ROPE))
    inv128 = jnp.concatenate([inv, inv, jnp.zeros((LANES - 2 * half,), F32)]).reshape(1, LANES)
    return pl.pallas_call(
        _rope_kernel,
        grid=(m // tm,),
        in_specs=[pl.BlockSpec((tm, 1), lambda i: (i, 0)),
                  pl.BlockSpec((1, LANES), lambda i: (0, 0))],
        out_specs=[pl.BlockSpec((tm, LANES), lambda i: (i, 0))] * 2,
        out_shape=[jax.ShapeDtypeStruct((m, LANES), F32)] * 2,
        compiler_params=_cparams(("arbitrary",)),
        name="rope_tables",
    )(pos, inv128)


def _hg_proj_kernel(h_ref, g_ref, sc_ref, sh_ref, w_ref, q_ref, f_ref, i_ref, gg_ref, u_scr):
    j = pl.program_id(1)

    @pl.when(j == 0)
    def _():
        u = _rms(h_ref[...]) * g_ref[...] * (1.0 + sc_ref[0]) + sh_ref[0]
        u_scr[...] = u.astype(BF16)

    r = _dot(u_scr[...], w_ref[...])
    for idx, ref in enumerate((q_ref, f_ref, i_ref, gg_ref)):
        @pl.when(j == idx)
        def _(ref=ref):
            ref[...] = r.astype(ref.dtype)


def _hg_proj(h, g, scale, shift, w_in, rows_per_batch, tm=512):
    m, d = h.shape
    bpb = rows_per_batch // tm
    row = lambda i, j: (i, 0)
    mod = lambda i, j: (i // bpb, 0, 0)
    return pl.pallas_call(
        _hg_proj_kernel,
        grid=(m // tm, 4),
        in_specs=[pl.BlockSpec((tm, d), row),
                  pl.BlockSpec((1, d), lambda i, j: (0, 0)),
                  pl.BlockSpec((1, 1, d), mod),
                  pl.BlockSpec((1, 1, d), mod),
                  pl.BlockSpec((d, d), lambda i, j: (0, j))],
        out_specs=[pl.BlockSpec((tm, d), row)] * 4,
        out_shape=[jax.ShapeDtypeStruct((m, d), BF16), jax.ShapeDtypeStruct((m, d), F32),
                   jax.ShapeDtypeStruct((m, d), BF16), jax.ShapeDtypeStruct((m, d), BF16)],
        scratch_shapes=[pltpu.VMEM((tm, d), BF16)],
        compiler_params=_cparams(("arbitrary", "arbitrary")),
        name="hg_proj",
    )(h, g, scale, shift, w_in)


def _hgrn2_kernel(*refs, n_chunks, side_blocks):
    n_side = len(side_blocks)
    q_ref, f_ref, v_ref, lb_ref = refs[:4]
    o_ref = refs[4 + n_side]
    st_ref = refs[5 + 2 * n_side]
    c, sub = HG_CHUNK, HG_SUB
    n_sub = c // sub

    @pl.when(pl.program_id(2) == 0)
    def _():
        st_ref[...] = jnp.zeros_like(st_ref)

    _side_cast(_grid_step_id(3), refs[4:4 + n_side], refs[5 + n_side:5 + 2 * n_side], side_blocks)

    row = lax.broadcasted_iota(jnp.int32, (c, c), 0)
    col = lax.broadcasted_iota(jnp.int32, (c, c), 1)
    tri = (row >= col).astype(BF16)
    row2 = lax.broadcasted_iota(jnp.int32, (c, 2 * c), 0)
    col2 = lax.broadcasted_iota(jnp.int32, (c, 2 * c), 1)
    c_bits, sub_bits = c.bit_length() - 1, sub.bit_length() - 1
    mask2 = ((col2 >> c_bits) == ((row2 >> sub_bits) & 1)) & ((col2 & (c - 1)) <= row2)

    n_heads = q_ref.shape[1] // HG_KDIM
    units = [(slice(ci * c, (ci + 1) * c), slice(hh * HG_KDIM, (hh + 1) * HG_KDIM))
             for hh in range(n_heads) for ci in range(n_chunks)]
    ks, bs = [], []
    for sl, hs in units:
        lb = lb_ref[:, hs]
        f = lb + (1.0 - lb) * _sigmoid(f_ref[sl, hs])
        lf = jnp.log2(f)
        lf_hi = lf.astype(BF16)
        lf_lo = (lf - lf_hi.astype(F32)).astype(BF16)
        ks.append(1.0 - f)
        bs.append(_dot(tri, lf_hi) + _dot(tri, lf_lo))
    a_fulls, qbs, kdecs, decays = [], [], [], []
    for (sl, hs), k, b in zip(units, ks, bs):
        q = q_ref[sl, hs].astype(F32)
        b_last = b[c - 1:c, :]
        starts = [jnp.zeros_like(b_last)] + [b[j * sub - 1:j * sub, :] for j in range(1, n_sub)]
        rrow = jnp.concatenate([jnp.broadcast_to(r, (sub, r.shape[1])) for r in starts], axis=0)
        qe = q * jnp.exp2(b - rrow)
        qbs.append((qe * jnp.exp2(rrow)).astype(BF16))
        ke = jnp.concatenate(
            [k * jnp.exp2(jnp.minimum(r - b, HG_EXP2_CLAMP)) for r in starts], axis=0)
        a_fulls.append(_dot_nt(qe.astype(BF16), ke.astype(BF16)))
        kdecs.append((k * jnp.exp2(b_last - b)).astype(BF16))
        decays.append(jnp.exp2(b_last))
    o_intras, incs = [], []
    for (sl, hs), a_full, kdec in zip(units, a_fulls, kdecs):
        v = v_ref[sl, hs]
        a2 = jnp.concatenate(
            [a_full[j * sub:(j + 1) * sub, (j // 2) * 2 * c:(j // 2 + 1) * 2 * c]
             for j in range(n_sub)], axis=0)
        a2 = jnp.where(mask2, a2, 0.0).astype(BF16)
        o_intras.append(_dot(a2, jnp.concatenate([v, v], axis=0)))
        incs.append(_dot_tn(v, kdec))
    for hh in range(n_heads):
        st = st_ref[hh]
        for ui in range(hh * n_chunks, (hh + 1) * n_chunks):
            sl, hs = units[ui]
            o_ref[sl, hs] = o_intras[ui] + _dot_nt(qbs[ui], st.astype(BF16))
            st = st * decays[ui] + incs[ui]
        st_ref[hh] = st


def _hgrn2_steps(m, tb=512):
    return (HG_HEADS // HG_HEADS_PER_STEP) * (m // tb)


def _hgrn2(q, f_logit, v, lb, batch, side=(), tb=512):
    m, d = q.shape
    s = m // batch
    nt = s // tb
    hg = HG_HEADS // HG_HEADS_PER_STEP
    wd = HG_HEADS_PER_STEP * HG_KDIM
    blk = lambda b, h, t: (b * nt + t, h)
    side_specs, side_blocks = _side_plan(
        side, batch * hg * nt, lambda b, h, t: (b * hg + h) * nt + t)
    outs = pl.pallas_call(
        functools.partial(_hgrn2_kernel, n_chunks=tb // HG_CHUNK, side_blocks=side_blocks),
        grid=(batch, hg, nt),
        in_specs=[pl.BlockSpec((tb, wd), blk)] * 3
        + [pl.BlockSpec((1, wd), lambda b, h, t: (0, h))] + side_specs,
        out_specs=[pl.BlockSpec((tb, wd), blk)] + side_specs,
        out_shape=[jax.ShapeDtypeStruct((m, d), F32)]
        + [jax.ShapeDtypeStruct(w.shape, BF16) for w in side],
        scratch_shapes=[pltpu.VMEM((HG_HEADS_PER_STEP, HG_KDIM, HG_KDIM), F32)],
        compiler_params=_cparams(("arbitrary", "arbitrary", "arbitrary")),
        name="hgrn2_recurrence",
    )(q, f_logit, v, lb, *side)
    return outs[0], list(outs[1:])


def _oproj_kernel(*refs, gated):
    if gated:
        (o_ref, gg_ref, gout_ref, w_ref, h_ref, gate_ref, g2_ref, gn_ref, scn_ref, shn_ref,
         out_ref, un_ref) = refs
        gg = gg_ref[...].astype(F32)
        x = _rms(o_ref[...]) * gout_ref[...] * (gg * _sigmoid(gg))
    else:
        o_ref, w_ref, h_ref, gate_ref, g2_ref, out_ref = refs
        x = o_ref[...]
    y = _dot(x.astype(BF16), w_ref[...])
    h_new = h_ref[...] + gate_ref[0] * (_rms(y) * g2_ref[...])
    out_ref[...] = h_new
    if gated:
        un_ref[...] = (_rms(h_new) * gn_ref[...] * (1.0 + scn_ref[0]) + shn_ref[0]).astype(BF16)


def _oproj(o, w, h, gate, g2, rows_per_batch, gg=None, gout=None, nxt=None, tm=512):
    m, d = h.shape
    bpb = rows_per_batch // tm
    row = pl.BlockSpec((tm, d), lambda i: (i, 0))
    vec = pl.BlockSpec((1, d), lambda i: (0, 0))
    mod = pl.BlockSpec((1, 1, d), lambda i: (i // bpb, 0, 0))
    wsp = pl.BlockSpec(w.shape, lambda i: (0, 0), pipeline_mode=pl.Buffered(1))
    gated = gg is not None
    if gated:
        args = (o, gg, gout, w, h, gate, g2) + tuple(nxt)
        specs = [row, row, vec, wsp, row, mod, vec, vec, mod, mod]
        out_specs = [row, row]
        out_shape = [jax.ShapeDtypeStruct((m, d), F32), jax.ShapeDtypeStruct((m, d), BF16)]
    else:
        args, specs = (o, w, h, gate, g2), [row, wsp, row, mod, vec]
        out_specs, out_shape = row, jax.ShapeDtypeStruct((m, d), F32)
    return pl.pallas_call(
        functools.partial(_oproj_kernel, gated=gated),
        grid=(m // tm,),
        in_specs=specs,
        out_specs=out_specs,
        out_shape=out_shape,
        compiler_params=_cparams(("arbitrary",)),
        name="hg_out_proj" if gated else "mla_out_proj",
    )(*args)


def _ffn_kernel(*refs, side_blocks):
    n_side = len(side_blocks)
    u_ref, wg_ref, wu_ref, wd_ref = refs[:4]
    y_ref = refs[4 + n_side]

    @pl.when(pl.program_id(1) == 0)
    def _():
        y_ref[...] = jnp.zeros_like(y_ref)

    _side_cast(_grid_step_id(2), refs[4:4 + n_side], refs[5 + n_side:5 + 2 * n_side], side_blocks)
    u = u_ref[...]
    gt = _dot(u, wg_ref[...])
    up = _dot(u, wu_ref[...])
    a = (gt * _sigmoid(gt) * up).astype(BF16)
    y_ref[...] += _dot(a, wd_ref[...])


def _ffn_steps(m, ff, tm=512, tf=FFN_TF):
    return (m // tm) * (ff // tf)


def _ffn(u, w_gu, w_down, side=(), tm=512, tf=FFN_TF):
    m, d = u.shape
    ff = w_down.shape[0]
    nf = ff // tf
    row = pl.BlockSpec((tm, d), lambda i, j: (i, 0))
    side_specs, side_blocks = _side_plan(side, (m // tm) * nf, lambda i, j: i * nf + j)
    outs = pl.pallas_call(
        functools.partial(_ffn_kernel, side_blocks=side_blocks),
        grid=(m // tm, nf),
        in_specs=[row,
                  pl.BlockSpec((d, tf), lambda i, j: (0, j)),
                  pl.BlockSpec((d, tf), lambda i, j: (0, nf + j)),
                  pl.BlockSpec((tf, d), lambda i, j: (j, 0))] + side_specs,
        out_specs=[row] + side_specs,
        out_shape=[jax.ShapeDtypeStruct((m, d), F32)]
        + [jax.ShapeDtypeStruct(w.shape, BF16) for w in side],
        compiler_params=_cparams(("arbitrary", "arbitrary")),
        name="dense_ffn",
    )(u, w_gu, w_gu, w_down, *side)
    return outs[0], list(outs[1:])


def _mla_proj_kernel(h_ref, y_ref, gate_ref, g2_ref, gq_ref, scq_ref, shq_ref, gk_ref, sck_ref, shk_ref,
                     wqa_ref, qg_ref, wqm_ref, wqr_ref,
                     wkva_ref, wkr_ref, wkrr_ref, kg_ref, wk_ref, wv_ref,
                     cos_ref, sin_ref, hn_ref, q_ref, k_ref, v_ref):
    h_new = h_ref[...] + gate_ref[0] * (_rms(y_ref[...]) * g2_ref[...])
    hn_ref[...] = h_new
    xhat = _rms(h_new)
    uq = (xhat * gq_ref[...] * (1.0 + scq_ref[0]) + shq_ref[0]).astype(BF16)
    xk = (xhat * gk_ref[...] * (1.0 + sck_ref[0]) + shk_ref[0]).astype(BF16)
    cos = cos_ref[...]
    sin = sin_ref[...]
    qn = (_rms(_dot(uq, wqa_ref[...])) * qg_ref[...]).astype(BF16)
    cn = (_rms(_dot(xk, wkva_ref[...])) * kg_ref[...]).astype(BF16)
    krope = (_dot(xk, wkr_ref[...]) * cos + _dot(xk, wkrr_ref[...]) * sin).astype(BF16)
    ones = jnp.ones((cn.shape[0], V_HEAD), BF16)
    for hp in range(MLA_HEADS // 2):
        ps = slice(hp * 2 * LANES, (hp + 1) * 2 * LANES)
        qr2 = _dot(qn, wqr_ref[:, ps]) * Q_SCALE
        kn2 = _dot(cn, wk_ref[:, ps]).astype(BF16)
        v2 = _dot(cn, wv_ref[:, ps]).astype(BF16)
        for sub in range(2):
            hd = 2 * hp + sub
            ls = slice(sub * LANES, (sub + 1) * LANES)
            qm = _dot(qn, wqm_ref[:, hd * QK_PAD:(hd + 1) * QK_PAD]) * Q_SCALE
            q_ref[:, hd * QK_PAD:hd * QK_PAD + QK_NOPE] = qm[:, :QK_NOPE].astype(BF16)
            q_ref[:, hd * QK_PAD + QK_NOPE:(hd + 1) * QK_PAD] = (
                qm[:, QK_NOPE:] * cos + qr2[:, ls] * sin).astype(BF16)
            k_ref[:, hd * QK_PAD:hd * QK_PAD + QK_NOPE] = kn2[:, ls]
            k_ref[:, hd * QK_PAD + QK_NOPE:(hd + 1) * QK_PAD] = krope
            v_ref[:, 2 * hd * V_HEAD:(2 * hd + 1) * V_HEAD] = v2[:, ls]
            v_ref[:, (2 * hd + 1) * V_HEAD:(2 * hd + 2) * V_HEAD] = ones


def _mla_proj(h, y, gate, g2, gq, scq, shq, gk, sck, shk, wts, cos, sin, rows_per_batch, tm=256):
    m, d = h.shape
    bpb = rows_per_batch // tm
    row = lambda w: pl.BlockSpec((tm, w), lambda i: (i, 0))
    vec = lambda w: pl.BlockSpec((1, w), lambda i: (0, 0))
    mod = pl.BlockSpec((1, 1, d), lambda i: (i // bpb, 0, 0))
    full = lambda a: pl.BlockSpec(a.shape, lambda i: (0, 0), pipeline_mode=pl.Buffered(1))
    wqa, qg, wqm, wqr, wkva, wkr, wkrr, kg, wk, wv = wts
    return pl.pallas_call(
        _mla_proj_kernel,
        grid=(m // tm,),
        in_specs=[row(d), row(d), mod, vec(d), vec(d), mod, mod, vec(d), mod, mod,
                  full(wqa), full(qg), full(wqm), full(wqr),
                  full(wkva), full(wkr), full(wkrr), full(kg), full(wk), full(wv),
                  row(LANES), row(LANES)],
        out_specs=[row(d), row(MLA_HEADS * QK_PAD), row(MLA_HEADS * QK_PAD),
                   row(MLA_HEADS * 2 * V_HEAD)],
        out_shape=[jax.ShapeDtypeStruct((m, d), F32),
                   jax.ShapeDtypeStruct((m, MLA_HEADS * QK_PAD), BF16),
                   jax.ShapeDtypeStruct((m, MLA_HEADS * QK_PAD), BF16),
                   jax.ShapeDtypeStruct((m, MLA_HEADS * 2 * V_HEAD), BF16)],
        compiler_params=_cparams(("arbitrary",)),
        name="mla_proj",
    )(h, y, gate, g2, gq, scq, shq, gk, sck, shk, wqa, qg, wqm, wqr, wkva, wkr, wkrr, kg, wk, wv,
      cos, sin)


def _flash_kernel(*refs, tq, tk, slab, side_blocks):
    n_side = len(side_blocks)
    q_ref, k_ref, v_ref = refs[:3]
    o_ref = refs[3 + n_side]
    s_a, s_b, p_scr, m_scr, al_scr, acc_scr = refs[4 + 2 * n_side:]
    qi = pl.program_id(2)
    _side_cast(_grid_step_id(3), refs[3:3 + n_side], refs[4 + n_side:4 + 2 * n_side], side_blocks)

    s_bufs = (s_a, s_b)
    m_scr[...] = jnp.full_like(m_scr, -jnp.inf)
    acc_scr[...] = jnp.zeros_like(acc_scr)
    n_col = tk // LANES
    n_diag = tq // tk

    def kv_rows(j):
        return pl.ds(pl.multiple_of(j * tk, tk), tk)

    half = tk // 2

    def kv_half(j):
        return pl.ds(pl.multiple_of(j * tk, half), half)

    def scores(j, dst, row0, diag=False):
        if diag:
            dst[row0:row0 + half, :half] = _dot_nt(q_ref[row0:row0 + half, :], k_ref[kv_half(j), :])
            row0 += half
        dst[row0:, :] = _dot_nt(q_ref[row0:, :], k_ref[kv_rows(j), :])

    def rescale(rows, pv):
        al = al_scr[rows, :]
        acc_scr[rows, :V_HEAD] = acc_scr[rows, :V_HEAD] * al + pv[:, :V_HEAD]
        acc_scr[rows, V_HEAD:] = acc_scr[rows, V_HEAD:] * al + pv[:, V_HEAD:]

    def consume(j, src, row0, diag):
        for r0 in range(row0, tq, slab):
            rows = slice(r0, r0 + slab)
            rel = r0 - row0
            top = diag and rel < half
            n_use = n_col // 2 if top else n_col
            cols = [src[rows, cb * LANES:(cb + 1) * LANES] for cb in range(n_use)]
            if diag and rel < tk:
                rowp = rel + lax.broadcasted_iota(jnp.int32, (slab, LANES), 0)
                lane = lax.broadcasted_iota(jnp.int32, (slab, LANES), 1)
                first = 0 if top else n_col // 2
                cols = [cols[cb] if cb < first else
                        jnp.where(cb * LANES + lane <= rowp, cols[cb], -jnp.inf)
                        for cb in range(n_use)]
            mx = functools.reduce(jnp.maximum, cols)
            m_old = m_scr[rows, :]
            m_new = jnp.maximum(m_old, jnp.max(mx, axis=-1, keepdims=True))
            al_scr[rows, :] = jnp.exp2(m_old - m_new)
            m_scr[rows, :] = m_new
            for cb in range(n_use):
                p_scr[rows, cb * LANES:(cb + 1) * LANES] = jnp.exp2(cols[cb] - m_new).astype(BF16)
        if diag:
            top_rows = slice(row0, row0 + half)
            rescale(top_rows, _dot(p_scr[top_rows, :half], v_ref[kv_half(j), :]))
            row0 += half
        rescale(slice(row0, tq), _dot(p_scr[row0:, :], v_ref[kv_rows(j), :]))

    n_full = qi * n_diag
    scores(0, s_a, 0)

    def pair(pi, carry):
        for u in range(2):
            scores(2 * pi + u + 1, s_bufs[1 - u], 0)
            consume(2 * pi + u, s_bufs[u], 0, False)
        return carry

    lax.fori_loop(0, n_full // 2, pair, 0)
    for dg in range(n_diag):
        if dg + 1 < n_diag:
            scores(n_full + dg + 1, s_bufs[(dg + 1) % 2], (dg + 1) * tk, diag=True)
        consume(n_full + dg, s_bufs[dg % 2], dg * tk, True)

    o_ref[...] = (acc_scr[:, :V_HEAD] / acc_scr[:, V_HEAD:]).astype(o_ref.dtype)


def _flash_tiles(seq):
    tq = min(FLASH_TQ, seq)
    return tq, min(FLASH_TK, tq // 2)


def _flash(q, k, v, batch, side=(), slab=64):
    m = q.shape[0]
    s = m // batch
    tq, tk = _flash_tiles(s)
    nq = s // tq
    side_specs, side_blocks = _side_plan(
        side, batch * MLA_HEADS * nq, lambda b, h, i: (b * MLA_HEADS + h) * nq + i)
    outs = pl.pallas_call(
        functools.partial(_flash_kernel, tq=tq, tk=tk, slab=slab, side_blocks=side_blocks),
        grid=(batch, MLA_HEADS, nq),
        in_specs=[pl.BlockSpec((tq, QK_PAD), lambda b, h, i: (b * nq + i, h)),
                  pl.BlockSpec((s, QK_PAD), lambda b, h, i: (b, h)),
                  pl.BlockSpec((s, 2 * V_HEAD), lambda b, h, i: (b, h))] + side_specs,
        out_specs=[pl.BlockSpec((tq, V_HEAD), lambda b, h, i: (b * nq + i, h))] + side_specs,
        out_shape=[jax.ShapeDtypeStruct((m, MLA_HEADS * V_HEAD), BF16)]
        + [jax.ShapeDtypeStruct(w.shape, BF16) for w in side],
        scratch_shapes=[pltpu.VMEM((tq, tk), F32), pltpu.VMEM((tq, tk), F32),
                        pltpu.VMEM((tq, tk), BF16), pltpu.VMEM((tq, LANES), F32),
                        pltpu.VMEM((tq, LANES), F32), pltpu.VMEM((tq, 2 * V_HEAD), F32)],
        compiler_params=_cparams(("arbitrary", "arbitrary", "arbitrary")),
        name="mla_flash",
    )(q, k, v, *side)
    return outs[0], list(outs[1:])


def _split3(x):
    hi = x.astype(BF16)
    lo = (x - hi.astype(F32)).astype(BF16)
    return hi, lo


def _moe_pre_kernel(h_ref, g_ref, sc_ref, sh_ref, wr_ref, xs_ref, meta_ref, wts_ref, cnt_ref,
                    run_scr):
    tm, d = h_ref.shape

    @pl.when(pl.program_id(0) == 0)
    def _():
        run_scr[...] = jnp.zeros_like(run_scr)

    u = _rms(h_ref[...]) * g_ref[...] * (1.0 + sc_ref[0]) + sh_ref[0]
    xs_ref[...] = u

    u_hi, u_lo = _split3(u)
    w_hi, w_lo = _split3(wr_ref[...])
    logits = _dot(u_hi, w_hi) + (_dot(u_hi, w_lo) + _dot(u_lo, w_hi))
    lane = lax.broadcasted_iota(jnp.int32, logits.shape, 1)
    lg = jnp.where(lane < N_EXPERTS, logits, -jnp.inf)
    m1 = jnp.max(lg, axis=-1, keepdims=True)
    i1 = jnp.min(jnp.where(lg == m1, lane, LANES), axis=-1, keepdims=True)
    lg2 = jnp.where(lane == i1, -jnp.inf, lg)
    m2 = jnp.max(lg2, axis=-1, keepdims=True)
    i2 = jnp.min(jnp.where(lg2 == m2, lane, LANES), axis=-1, keepdims=True)
    e = jnp.exp(m2 - m1)
    w1 = 1.0 / (1.0 + e)
    w2 = e * w1

    oh1 = lane == i1
    oh2 = lane == i2
    cnt = oh1.astype(F32) + oh2.astype(F32)
    r = lax.broadcasted_iota(jnp.int32, (tm, tm), 0)
    c = lax.broadcasted_iota(jnp.int32, (tm, tm), 1)
    strict = (r > c).astype(BF16)
    prefix = _dot(strict, cnt.astype(BF16)) + run_scr[...]
    rank1 = jnp.sum(jnp.where(oh1, prefix, 0.0), axis=-1, keepdims=True).astype(jnp.int32)
    rank2 = jnp.sum(jnp.where(oh2, prefix, 0.0), axis=-1, keepdims=True).astype(jnp.int32)
    run_scr[...] += jnp.sum(cnt, axis=0, keepdims=True)

    meta_ref[...] = jnp.where(lane == 0, i1, jnp.where(lane == 1, i2,
                              jnp.where(lane == 2, rank1, jnp.where(lane == 3, rank2, 0))))
    wts_ref[...] = jnp.where(lane == 0, w1, jnp.where(lane == 1, w2, 0.0))
    cnt_ref[...] = run_scr[...]


def _moe_pre(h, g, scale, shift, w_router_pad, rows_per_batch, tm=512):
    m, d = h.shape
    bpb = rows_per_batch // tm
    row = lambda w: pl.BlockSpec((tm, w), lambda i: (i, 0))
    vec = lambda w: pl.BlockSpec((1, w), lambda i: (0, 0))
    mod = pl.BlockSpec((1, 1, d), lambda i: (i // bpb, 0, 0))
    return pl.pallas_call(
        _moe_pre_kernel,
        grid=(m // tm,),
        in_specs=[row(d), vec(d), mod, mod, pl.BlockSpec((d, LANES), lambda i: (0, 0))],
        out_specs=[row(d), row(LANES), row(LANES), vec(LANES)],
        out_shape=[jax.ShapeDtypeStruct((m, d), F32),
                   jax.ShapeDtypeStruct((m, LANES), jnp.int32),
                   jax.ShapeDtypeStruct((m, LANES), F32),
                   jax.ShapeDtypeStruct((1, LANES), F32)],
        scratch_shapes=[pltpu.VMEM((1, LANES), F32)],
        compiler_params=_cparams(("arbitrary",)),
        name="moe_route",
    )(h, g, scale, shift, w_router_pad)


ROW_COPY_UNROLL = 8


def _issue_row_copies(n_rows, copies_of_row):
    def trip(t, carry):
        n = 0
        for u in range(ROW_COPY_UNROLL):
            for cp in copies_of_row(t * ROW_COPY_UNROLL + u):
                cp.start(priority=n % 2)
                n += 1
        return carry
    lax.fori_loop(0, n_rows // ROW_COPY_UNROLL, trip, 0)


def _moe_dispatch_kernel(fill_ref, dest_ref, x_ref, xs_ref, zbuf, sem, zsem, *, tmb, n_tail):
    tm = x_ref.shape[0]

    @pl.when(pl.program_id(0) == 0)
    def _():
        zbuf[...] = jnp.zeros_like(zbuf)

        def pad_row(e, r):
            return pltpu.make_async_copy(zbuf.at[pl.ds(0, 1), :],
                                         xs_ref.at[pl.ds(fill_ref[2 * e] + r, 1), :], zsem)

        def tail_block(b):
            start = pl.multiple_of(fill_ref[2 * N_EXPERTS] + b * tmb, tmb)
            return pltpu.make_async_copy(zbuf, xs_ref.at[pl.ds(start, tmb), :], zsem)

        for e in range(N_EXPERTS):
            def start_row(r, carry, e=e):
                pad_row(e, r).start()
                return carry
            lax.fori_loop(0, fill_ref[2 * e + 1], start_row, 0)
        n_blk = fill_ref[2 * N_EXPERTS + 1] // tmb
        for b in range(n_tail):
            @pl.when(b < n_blk)
            def _(b=b):
                tail_block(b).start()
        for e in range(N_EXPERTS):
            def wait_row(r, carry, e=e):
                pad_row(e, r).wait()
                return carry
            lax.fori_loop(0, fill_ref[2 * e + 1], wait_row, 0)
        for b in range(n_tail):
            @pl.when(b < n_blk)
            def _(b=b):
                tail_block(b).wait()

    def copies_of_row(r):
        return [pltpu.make_async_copy(x_ref.at[pl.ds(r, 1), :],
                                      xs_ref.at[pl.ds(dest_ref[0, 0, 2 * r + k], 1), :], sem)
                for k in range(2)]

    _issue_row_copies(tm, copies_of_row)
    for _ in range(2):
        pltpu.make_async_copy(x_ref, x_ref, sem).wait()


def _moe_dispatch(x, dest, fill, n_rows, tmb, tm=256):
    m, w = x.shape
    nb = m // tm
    n_tail = (n_rows - 2 * m) // tmb
    return pl.pallas_call(
        functools.partial(_moe_dispatch_kernel, tmb=tmb, n_tail=n_tail),
        grid_spec=pltpu.PrefetchScalarGridSpec(
            num_scalar_prefetch=1,
            grid=(nb,),
            in_specs=[pl.BlockSpec((1, 1, 2 * tm), lambda i, fl: (i, 0, 0), memory_space=pltpu.SMEM),
                      pl.BlockSpec((tm, w), lambda i, fl: (i, 0))],
            out_specs=pl.BlockSpec(memory_space=pl.ANY),
            scratch_shapes=[pltpu.VMEM((tmb, w), F32), pltpu.SemaphoreType.DMA(()),
                            pltpu.SemaphoreType.DMA(())],
        ),
        out_shape=jax.ShapeDtypeStruct((n_rows, w), F32),
        compiler_params=_cparams(("arbitrary",)),
        name="moe_dispatch",
    )(fill, dest.reshape(nb, 1, 2 * tm), x)


def _moe_ffn_kernel(blk_e_ref, n_used_ref, xs_ref, wg_ref, wu_ref, wd_ref, y_ref, u_scr):
    i = pl.program_id(0)
    j = pl.program_id(1)

    @pl.when(i < n_used_ref[0])
    def _():
        @pl.when(j == 0)
        def _():
            u_scr[...] = xs_ref[...].astype(BF16)
            y_ref[...] = jnp.zeros_like(y_ref)

        u = u_scr[...]
        gt = _dot(u, wg_ref[0])
        up = _dot(u, wu_ref[0])
        a = (gt * _sigmoid(gt) * up).astype(BF16)
        y_ref[...] += _dot(a, wd_ref[0])

    @pl.when((i >= n_used_ref[0]) & (j == 0))
    def _():
        y_ref[...] = jnp.zeros_like(y_ref)


def _moe_ffn(xs, blk_e, n_used, w_gu, w_down, tmb, tf=FFN_TF):
    n_rows, d = xs.shape
    ff = w_down.shape[1]
    nf = ff // tf
    nb = n_rows // tmb

    def blk(i, n_used_ref):
        return jnp.minimum(i, n_used_ref[0] - 1)

    def fidx(i, j, n_used_ref):
        return jnp.where(i < n_used_ref[0], j, nf - 1)

    grid_spec = pltpu.PrefetchScalarGridSpec(
        num_scalar_prefetch=2,
        grid=(nb, nf),
        in_specs=[
            pl.BlockSpec((tmb, d), lambda i, j, be, nu: (blk(i, nu), 0)),
            pl.BlockSpec((1, d, tf), lambda i, j, be, nu: (be[blk(i, nu)], 0, fidx(i, j, nu))),
            pl.BlockSpec((1, d, tf), lambda i, j, be, nu: (be[blk(i, nu)], 0, nf + fidx(i, j, nu))),
            pl.BlockSpec((1, tf, d), lambda i, j, be, nu: (be[blk(i, nu)], fidx(i, j, nu), 0)),
        ],
        out_specs=pl.BlockSpec((tmb, d), lambda i, j, be, nu: (i, 0)),
        scratch_shapes=[pltpu.VMEM((tmb, d), BF16)],
    )
    return pl.pallas_call(
        _moe_ffn_kernel,
        grid_spec=grid_spec,
        out_shape=jax.ShapeDtypeStruct((n_rows, d), F32),
        compiler_params=_cparams(("arbitrary", "arbitrary")),
        name="moe_grouped_ffn",
    )(blk_e, n_used, xs, w_gu, w_gu, w_down)


def _moe_combine_kernel(dest_ref, ys_ref, wts_ref, h_ref, gate_ref, g2_ref, out_ref,
                        buf0, buf1, sem):
    tm = h_ref.shape[0]
    bufs = (buf0, buf1)

    def copies_of_row(r):
        return [pltpu.make_async_copy(ys_ref.at[pl.ds(dest_ref[0, 0, 2 * r + k], 1), :],
                                      bufs[k].at[pl.ds(r, 1), :], sem)
                for k in range(2)]

    _issue_row_copies(tm, copies_of_row)
    for k in range(2):
        pltpu.make_async_copy(bufs[k], bufs[k], sem).wait()

    wts = wts_ref[...]
    y = wts[:, 0:1] * buf0[...] + wts[:, 1:2] * buf1[...]
    out_ref[...] = h_ref[...] + gate_ref[0] * (_rms(y) * g2_ref[...])


def _moe_combine(ys, dest, wts, h, gate, g2, rows_per_batch, tm=256):
    m, d = h.shape
    nb = m // tm
    bpb = rows_per_batch // tm
    row = lambda w: pl.BlockSpec((tm, w), lambda i: (i, 0))
    return pl.pallas_call(
        _moe_combine_kernel,
        grid=(nb,),
        in_specs=[pl.BlockSpec((1, 1, 2 * tm), lambda i: (i, 0, 0), memory_space=pltpu.SMEM),
                  pl.BlockSpec(memory_space=pl.ANY),
                  row(LANES), row(d),
                  pl.BlockSpec((1, 1, d), lambda i: (i // bpb, 0, 0)),
                  pl.BlockSpec((1, d), lambda i: (0, 0))],
        out_specs=row(d),
        out_shape=jax.ShapeDtypeStruct((m, d), F32),
        scratch_shapes=[pltpu.VMEM((tm, d), F32), pltpu.VMEM((tm, d), F32),
                        pltpu.SemaphoreType.DMA(())],
        compiler_params=_cparams(("arbitrary",)),
        name="moe_combine",
    )(dest.reshape(nb, 1, 2 * tm), ys, wts, h, gate, g2)


def _moe(h, g, scale, shift, w_router, w_gu, w_down, gate, g2, rows_per_batch, tmb=MOE_BLOCK_ROWS):
    m, d = h.shape
    n_rows = (2 * m + N_EXPERTS * (tmb - 1)) // tmb * tmb
    nb = n_rows // tmb
    w_router_pad = jnp.pad(w_router, ((0, 0), (0, LANES - N_EXPERTS)))
    u, meta, wts, cnt = _moe_pre(h, g, scale, shift, w_router_pad, rows_per_batch)

    counts = cnt[0, :N_EXPERTS].astype(jnp.int32)
    padded = (counts + tmb - 1) // tmb * tmb
    pend = jnp.cumsum(padded)
    pstart = pend - padded
    dest = (pstart[meta[:, 0:2]] + meta[:, 2:4]).astype(jnp.int32).reshape(-1)
    blk_start = jnp.arange(nb, dtype=jnp.int32) * tmb
    blk_e = jnp.minimum(jnp.sum(pend[None, :] <= blk_start[:, None], axis=1),
                        N_EXPERTS - 1).astype(jnp.int32)
    n_used = (pend[-1:] // tmb).astype(jnp.int32)

    fill = jnp.stack([jnp.concatenate([pstart + counts, pend[-1:]]),
                      jnp.concatenate([padded - counts, n_rows - pend[-1:]])],
                     axis=1).reshape(-1).astype(jnp.int32)
    xs = _moe_dispatch(u, dest, fill, n_rows, tmb)
    ys = _moe_ffn(xs, blk_e, n_used, w_gu, w_down, tmb)
    return _moe_combine(ys, dest, wts, h, gate, g2, rows_per_batch)


def _swap_halves(w):
    half = w.shape[-1] // 2
    return jnp.concatenate([w[..., half:], w[..., :half]], axis=-1)


def _mla_weights(w_q_a, q_norm_g, w_q_b, w_kv_a, kv_norm_g, w_kv_b):
    ql = w_q_b.shape[0]
    kvl = w_kv_b.shape[0]
    wq = w_q_b.reshape(ql, MLA_HEADS, QK_NOPE + QK_ROPE)
    rope = wq[:, :, QK_NOPE:]
    pad_r = LANES - QK_ROPE
    wqm = jnp.concatenate([wq, jnp.zeros((ql, MLA_HEADS, QK_PAD - QK_NOPE - QK_ROPE), F32)], axis=-1)
    wqr = jnp.concatenate([_swap_halves(rope), jnp.zeros((ql, MLA_HEADS, pad_r), F32)], axis=-1)
    wkv = w_kv_b.reshape(kvl, MLA_HEADS, QK_NOPE + V_HEAD)
    kr = w_kv_a[:, kvl:]
    dm = w_kv_a.shape[0]
    wkr = jnp.concatenate([kr, jnp.zeros((dm, pad_r), F32)], axis=-1)
    wkrr = jnp.concatenate([_swap_halves(kr), jnp.zeros((dm, pad_r), F32)], axis=-1)
    bf = lambda a: a.astype(BF16)
    return (bf(w_q_a), q_norm_g.reshape(1, -1), bf(wqm.reshape(ql, -1)), bf(wqr.reshape(ql, -1)),
            bf(w_kv_a[:, :kvl]), bf(wkr), bf(wkrr), kv_norm_g.reshape(1, -1),
            bf(wkv[:, :, :QK_NOPE].reshape(kvl, -1)), bf(wkv[:, :, QK_NOPE:].reshape(kvl, -1)))


def kernel(x, c, positions, ada_w, ada_b, norm_g, hg_w_in, hg_lb_logits, hg_out_norm_g, hg_w_out, kv_src_norm_g, kv_src_ada_w, kv_src_ada_b, mla_w_kv_a, mla_kv_norm_g, mla_w_kv_b, mla_w_q_a, mla_q_norm_g, mla_w_q_b, mla_w_o, ffn_w_gu, ffn_w_down, moe_w_router, moe_w_gu, moe_w_down):
    batch, seq, d = x.shape
    m = batch * seq
    bf = lambda a: a.astype(BF16)

    c8 = jnp.pad(c, ((0, 8 - batch), (0, 0)))
    ada = _modulation(c8, ada_w.reshape(4, d, 3 * d), ada_b.reshape(4, 3 * d))[:, :batch]
    kvm = _modulation(c8, kv_src_ada_w[None], kv_src_ada_b[None])[0, :batch]

    def mods(idx):
        a = ada[idx]
        return [a[:, i * d:(i + 1) * d].reshape(batch, 1, d) for i in range(3)]

    vec = lambda a: a.reshape(1, -1)
    lb_all = jnp.cumsum(jax.nn.softmax(hg_lb_logits.astype(F32), axis=0), axis=0)

    h = x.reshape(m, d)

    shift, scale, gate = mods(0)
    q, f_logit, iv, gg = _hg_proj(h, vec(norm_g[0, 0, 0]), scale, shift, bf(hg_w_in[0]), seq)
    o, (ffn_gu_bf, ffn_down_bf, hg_out_bf, moe_down_bf) = _with_side(
        lambda side: _hgrn2(q, f_logit, iv, vec(lb_all[0]), batch, side=side),
        [ffn_w_gu[0], ffn_w_down[0], hg_w_out[0], moe_w_down[0]], _hgrn2_steps(m))
    shift_f, scale_f, gate_f = mods(1)
    h, u_ffn = _oproj(o, hg_out_bf, h, gate, vec(norm_g[0, 0, 1]), seq,
                      gg=gg, gout=vec(hg_out_norm_g[0]),
                      nxt=(vec(norm_g[0, 1, 0]), scale_f, shift_f))

    y_ffn, (moe_gu_bf, mla_o_bf) = _with_side(
        lambda side: _ffn(u_ffn, ffn_gu_bf, ffn_down_bf, side=side),
        [moe_w_gu[0], mla_w_o[0]], _ffn_steps(m, ffn_w_down.shape[1]))

    shift_k, scale_k = [kvm[:, i * d:(i + 1) * d].reshape(batch, 1, d) for i in range(2)]
    shift, scale, gate = mods(2)
    cos, sin = _rope_tables(positions.reshape(m, 1).astype(F32))
    wts = _mla_weights(mla_w_q_a[0], mla_q_norm_g[0], mla_w_q_b[0],
                       mla_w_kv_a, mla_kv_norm_g, mla_w_kv_b)
    h, qh, kh, vh = _mla_proj(h, y_ffn, gate_f, vec(norm_g[0, 1, 1]),
                              vec(norm_g[1, 0, 0]), scale, shift,
                              vec(kv_src_norm_g), scale_k, shift_k, wts, cos, sin, seq)
    att, _ = _flash(qh, kh, vh, batch)
    h = _oproj(att, mla_o_bf, h, gate, vec(norm_g[1, 0, 1]), seq)

    shift, scale, gate = mods(3)
    h = _moe(h, vec(norm_g[1, 1, 0]), scale, shift, moe_w_router[0],
             moe_gu_bf, moe_down_bf, gate, vec(norm_g[1, 1, 1]), seq)
    return h.reshape(batch, seq, d)
```

```python
import functools

import jax
import jax.numpy as jnp
from jax import lax
from jax.experimental import pallas as pl
from jax.experimental.pallas import tpu as pltpu

F32 = jnp.float32
BF16 = jnp.bfloat16

EPS = 1e-6
LANES = 128
F32_SUBLANES = 8
BF16_SUBLANES = 16
SIDE_BLOCK_BYTES = 3 * 1024 * 1024

HG_HEADS = 16
HG_KDIM = 128
HG_CHUNK = 64
HG_SUB = 16
HG_EXP2_CLAMP = 115.0
HG_HEADS_PER_STEP = 4

MLA_HEADS = 16
QK_NOPE = 128
QK_ROPE = 64
V_HEAD = 128
QK_PAD = 256
MLA_SCALE = (QK_NOPE + QK_ROPE) ** -0.5
Q_SCALE = MLA_SCALE * 1.4426950408889634
ROPE_THETA = 10000.0
FLASH_TQ = 2048
FLASH_TK = 1024

N_EXPERTS = 8
MOE_BLOCK_ROWS = 768
FFN_TF = 512

VMEM_LIMIT = 56 * 1024 * 1024


def _cparams(sem):
    return pltpu.CompilerParams(dimension_semantics=sem, vmem_limit_bytes=VMEM_LIMIT)


def _rms(x):
    return x * lax.rsqrt(jnp.mean(x * x, axis=-1, keepdims=True) + EPS)


def _sigmoid(x):
    return 1.0 / (1.0 + jnp.exp(-x))


def _dot(a, b):
    return jnp.dot(a, b, preferred_element_type=F32)


def _dot_nt(a, b):
    return lax.dot_general(a, b, (((1,), (1,)), ((), ())), preferred_element_type=F32)


def _dot_tn(a, b):
    return lax.dot_general(a, b, (((0,), (0,)), ((), ())), preferred_element_type=F32)


def _side_rows(w, n_steps):
    e, r, c = w.shape
    for rb in range(BF16_SUBLANES, r + 1, BF16_SUBLANES):
        if r % rb == 0 and e * (r // rb) <= n_steps:
            return rb if rb * c * 4 <= SIDE_BLOCK_BYTES else None
    return None


def _side_ok(weights, n_steps):
    return all(_side_rows(w, n_steps) is not None for w in weights)


def _side_plan(weights, n_steps, step_of):
    specs, blocks = [], []
    for w in weights:
        rb = _side_rows(w, n_steps)
        per = w.shape[1] // rb
        n_blk = w.shape[0] * per
        blocks.append(n_blk)

        def idx(*grid_ids, per=per, n_blk=n_blk):
            blk = jnp.minimum(step_of(*grid_ids), n_blk - 1)
            return (blk // per, blk % per, 0)

        specs.append(pl.BlockSpec((1, rb, w.shape[2]), idx))
    return specs, tuple(blocks)


def _side_cast(step_id, w_ins, w_outs, blocks):
    for w_in, w_out, n_blk in zip(w_ins, w_outs, blocks):
        @pl.when(step_id < n_blk)
        def _(w_in=w_in, w_out=w_out):
            w_out[...] = w_in[...].astype(w_out.dtype)


def _grid_step_id(n_axes):
    sid = pl.program_id(0)
    for ax in range(1, n_axes):
        sid = sid * pl.num_programs(ax) + pl.program_id(ax)
    return sid


def _with_side(run, weights, n_steps):
    w3 = tuple(w.reshape((1,) + w.shape) if w.ndim == 2 else w for w in weights)
    if _side_ok(w3, n_steps):
        res, copies = run(w3)
        return res, [cp.reshape(w.shape) for cp, w in zip(copies, weights)]
    res, _ = run(())
    return res, [w.astype(BF16) for w in weights]


def _mod_kernel(c_ref, w_ref, b_ref, o_ref):
    c = c_ref[...]
    sc = (c * _sigmoid(c)).astype(BF16)
    o_ref[0] = _dot(sc, w_ref[0].astype(BF16)) + b_ref[0]


def _modulation(c8, w, b, tn=1024):
    g, d, n = w.shape
    return pl.pallas_call(
        _mod_kernel,
        grid=(g, n // tn),
        in_specs=[pl.BlockSpec((8, d), lambda i, j: (0, 0)),
                  pl.BlockSpec((1, d, tn), lambda i, j: (i, 0, j)),
                  pl.BlockSpec((1, 1, tn), lambda i, j: (i, 0, j))],
        out_specs=pl.BlockSpec((1, 8, tn), lambda i, j: (i, 0, j)),
        out_shape=jax.ShapeDtypeStruct((g, 8, n), F32),
        compiler_params=_cparams(("arbitrary", "arbitrary")),
        name="modulation",
    )(c8, w, b.reshape(g, 1, n))


def _rope_kernel(pos_ref, inv_ref, cos_ref, sin_ref):
    ang = pos_ref[...] * inv_ref[...]
    lane = lax.broadcasted_iota(jnp.int32, ang.shape, 1)
    half = QK_ROPE // 2
    cos_ref[...] = jnp.where(lane < QK_ROPE, jnp.cos(ang), 0.0)
    sin_ref[...] = jnp.where(lane < half, -jnp.sin(ang),
                             jnp.where(lane < QK_ROPE, jnp.sin(ang), 0.0))


def _rope_tables(pos, tm=512):
    m = pos.shape[0]
    half = QK_ROPE // 2
    inv = 1.0 / (ROPE_THETA ** (jnp.arange(0, QK_ROPE, 2, dtype=F32) / QK_ROPE))
    inv128 = jnp.concatenate([inv, inv, jnp.zeros((LANES - 2 * half,), F32)]).reshape(1, LANES)
    return pl.pallas_call(
        _rope_kernel,
        grid=(m // tm,),
        in_specs=[pl.BlockSpec((tm, 1), lambda i: (i, 0)),
                  pl.BlockSpec((1, LANES), lambda i: (0, 0))],
        out_specs=[pl.BlockSpec((tm, LANES), lambda i: (i, 0))] * 2,
        out_shape=[jax.ShapeDtypeStruct((m, LANES), F32)] * 2,
        compiler_params=_cparams(("arbitrary",)),
        name="rope_tables",
    )(pos, inv128)


def _hg_proj_kernel(h_ref, g_ref, sc_ref, sh_ref, w_ref, q_ref, f_ref, i_ref, gg_ref, u_scr):
    j = pl.program_id(1)

    @pl.when(j == 0)
    def _():
        u = _rms(h_ref[...]) * g_ref[...] * (1.0 + sc_ref[0]) + sh_ref[0]
        u_scr[...] = u.astype(BF16)

    r = _dot(u_scr[...], w_ref[...])
    for idx, ref in enumerate((q_ref, f_ref, i_ref, gg_ref)):
        @pl.when(j == idx)
        def _(ref=ref):
            ref[...] = r.astype(ref.dtype)


def _hg_proj(h, g, scale, shift, w_in, rows_per_batch, tm=512):
    m, d = h.shape
    bpb = rows_per_batch // tm
    row = lambda i, j: (i, 0)
    mod = lambda i, j: (i // bpb, 0, 0)
    return pl.pallas_call(
        _hg_proj_kernel,
        grid=(m // tm, 4),
        in_specs=[pl.BlockSpec((tm, d), row),
                  pl.BlockSpec((1, d), lambda i, j: (0, 0)),
                  pl.BlockSpec((1, 1, d), mod),
                  pl.BlockSpec((1, 1, d), mod),
                  pl.BlockSpec((d, d), lambda i, j: (0, j))],
        out_specs=[pl.BlockSpec((tm, d), row)] * 4,
        out_shape=[jax.ShapeDtypeStruct((m, d), BF16), jax.ShapeDtypeStruct((m, d), F32),
                   jax.ShapeDtypeStruct((m, d), BF16), jax.ShapeDtypeStruct((m, d), BF16)],
        scratch_shapes=[pltpu.VMEM((tm, d), BF16)],
        compiler_params=_cparams(("arbitrary", "arbitrary")),
        name="hg_proj",
    )(h, g, scale, shift, w_in)


def _hgrn2_kernel(*refs, n_chunks, side_blocks):
    n_side = len(side_blocks)
    q_ref, f_ref, v_ref, lb_ref = refs[:4]
    o_ref = refs[4 + n_side]
    st_ref = refs[5 + 2 * n_side]
    c, sub = HG_CHUNK, HG_SUB
    n_sub = c // sub

    @pl.when(pl.program_id(2) == 0)
    def _():
        st_ref[...] = jnp.zeros_like(st_ref)

    _side_cast(_grid_step_id(3), refs[4:4 + n_side], refs[5 + n_side:5 + 2 * n_side], side_blocks)

    row = lax.broadcasted_iota(jnp.int32, (c, c), 0)
    col = lax.broadcasted_iota(jnp.int32, (c, c), 1)
    tri = (row >= col).astype(BF16)
    row2 = lax.broadcasted_iota(jnp.int32, (c, 2 * c), 0)
    col2 = lax.broadcasted_iota(jnp.int32, (c, 2 * c), 1)
    c_bits, sub_bits = c.bit_length() - 1, sub.bit_length() - 1
    mask2 = ((col2 >> c_bits) == ((row2 >> sub_bits) & 1)) & ((col2 & (c - 1)) <= row2)

    n_heads = q_ref.shape[1] // HG_KDIM
    units = [(slice(ci * c, (ci + 1) * c), slice(hh * HG_KDIM, (hh + 1) * HG_KDIM))
             for hh in range(n_heads) for ci in range(n_chunks)]
    ks, bs = [], []
    for sl, hs in units:
        lb = lb_ref[:, hs]
        f = lb + (1.0 - lb) * _sigmoid(f_ref[sl, hs])
        lf = jnp.log2(f)
        lf_hi = lf.astype(BF16)
        lf_lo = (lf - lf_hi.astype(F32)).astype(BF16)
        ks.append(1.0 - f)
        bs.append(_dot(tri, lf_hi) + _dot(tri, lf_lo))
    a_fulls, qbs, kdecs, decays = [], [], [], []
    for (sl, hs), k, b in zip(units, ks, bs):
        q = q_ref[sl, hs].astype(F32)
        b_last = b[c - 1:c, :]
        starts = [jnp.zeros_like(b_last)] + [b[j * sub - 1:j * sub, :] for j in range(1, n_sub)]
        rrow = jnp.concatenate([jnp.broadcast_to(r, (sub, r.shape[1])) for r in starts], axis=0)
        qe = q * jnp.exp2(b - rrow)
        qbs.append((qe * jnp.exp2(rrow)).astype(BF16))
        ke = jnp.concatenate(
            [k * jnp.exp2(jnp.minimum(r - b, HG_EXP2_CLAMP)) for r in starts], axis=0)
        a_fulls.append(_dot_nt(qe.astype(BF16), ke.astype(BF16)))
        kdecs.append((k * jnp.exp2(b_last - b)).astype(BF16))
        decays.append(jnp.exp2(b_last))
    o_intras, incs = [], []
    for (sl, hs), a_full, kdec in zip(units, a_fulls, kdecs):
        v = v_ref[sl, hs]
        a2 = jnp.concatenate(
            [a_full[j * sub:(j + 1) * sub, (j // 2) * 2 * c:(j // 2 + 1) * 2 * c]
             for j in range(n_sub)], axis=0)
        a2 = jnp.where(mask2, a2, 0.0).astype(BF16)
        o_intras.append(_dot(a2, jnp.concatenate([v, v], axis=0)))
        incs.append(_dot_tn(v, kdec))
    for hh in range(n_heads):
        st = st_ref[hh]
        for ui in range(hh * n_chunks, (hh + 1) * n_chunks):
            sl, hs = units[ui]
            o_ref[sl, hs] = o_intras[ui] + _dot_nt(qbs[ui], st.astype(BF16))
            st = st * decays[ui] + incs[ui]
        st_ref[hh] = st


def _hgrn2_steps(m, tb=512):
    return (HG_HEADS // HG_HEADS_PER_STEP) * (m // tb)


def _hgrn2(q, f_logit, v, lb, batch, side=(), tb=512):
    m, d = q.shape
    s = m // batch
    nt = s // tb
    hg = HG_HEADS // HG_HEADS_PER_STEP
    wd = HG_HEADS_PER_STEP * HG_KDIM
    blk = lambda b, h, t: (b * nt + t, h)
    side_specs, side_blocks = _side_plan(
        side, batch * hg * nt, lambda b, h, t: (b * hg + h) * nt + t)
    outs = pl.pallas_call(
        functools.partial(_hgrn2_kernel, n_chunks=tb // HG_CHUNK, side_blocks=side_blocks),
        grid=(batch, hg, nt),
        in_specs=[pl.BlockSpec((tb, wd), blk)] * 3
        + [pl.BlockSpec((1, wd), lambda b, h, t: (0, h))] + side_specs,
        out_specs=[pl.BlockSpec((tb, wd), blk)] + side_specs,
        out_shape=[jax.ShapeDtypeStruct((m, d), F32)]
        + [jax.ShapeDtypeStruct(w.shape, BF16) for w in side],
        scratch_shapes=[pltpu.VMEM((HG_HEADS_PER_STEP, HG_KDIM, HG_KDIM), F32)],
        compiler_params=_cparams(("arbitrary", "arbitrary", "arbitrary")),
        name="hgrn2_recurrence",
    )(q, f_logit, v, lb, *side)
    return outs[0], list(outs[1:])


def _oproj_kernel(*refs, gated):
    if gated:
        (o_ref, gg_ref, gout_ref, w_ref, h_ref, gate_ref, g2_ref, gn_ref, scn_ref, shn_ref,
         out_ref, un_ref) = refs
        gg = gg_ref[...].astype(F32)
        x = _rms(o_ref[...]) * gout_ref[...] * (gg * _sigmoid(gg))
    else:
        o_ref, w_ref, h_ref, gate_ref, g2_ref, out_ref = refs
        x = o_ref[...]
    y = _dot(x.astype(BF16), w_ref[...])
    h_new = h_ref[...] + gate_ref[0] * (_rms(y) * g2_ref[...])
    out_ref[...] = h_new
    if gated:
        un_ref[...] = (_rms(h_new) * gn_ref[...] * (1.0 + scn_ref[0]) + shn_ref[0]).astype(BF16)


def _oproj(o, w, h, gate, g2, rows_per_batch, gg=None, gout=None, nxt=None, tm=512):
    m, d = h.shape
    bpb = rows_per_batch // tm
    row = pl.BlockSpec((tm, d), lambda i: (i, 0))
    vec = pl.BlockSpec((1, d), lambda i: (0, 0))
    mod = pl.BlockSpec((1, 1, d), lambda i: (i // bpb, 0, 0))
    wsp = pl.BlockSpec(w.shape, lambda i: (0, 0), pipeline_mode=pl.Buffered(1))
    gated = gg is not None
    if gated:
        args = (o, gg, gout, w, h, gate, g2) + tuple(nxt)
        specs = [row, row, vec, wsp, row, mod, vec, vec, mod, mod]
        out_specs = [row, row]
        out_shape = [jax.ShapeDtypeStruct((m, d), F32), jax.ShapeDtypeStruct((m, d), BF16)]
    else:
        args, specs = (o, w, h, gate, g2), [row, wsp, row, mod, vec]
        out_specs, out_shape = row, jax.ShapeDtypeStruct((m, d), F32)
    return pl.pallas_call(
        functools.partial(_oproj_kernel, gated=gated),
        grid=(m // tm,),
        in_specs=specs,
        out_specs=out_specs,
        out_shape=out_shape,
        compiler_params=_cparams(("arbitrary",)),
        name="hg_out_proj" if gated else "mla_out_proj",
    )(*args)


def _ffn_kernel(*refs, side_blocks):
    n_side = len(side_blocks)
    u_ref, wg_ref, wu_ref, wd_ref = refs[:4]
    y_ref = refs[4 + n_side]

    @pl.when(pl.program_id(1) == 0)
    def _():
        y_ref[...] = jnp.zeros_like(y_ref)

    _side_cast(_grid_step_id(2), refs[4:4 + n_side], refs[5 + n_side:5 + 2 * n_side], side_blocks)
    u = u_ref[...]
    gt = _dot(u, wg_ref[...])
    up = _dot(u, wu_ref[...])
    a = (gt * _sigmoid(gt) * up).astype(BF16)
    y_ref[...] += _dot(a, wd_ref[...])


def _ffn_steps(m, ff, tm=512, tf=FFN_TF):
    return (m // tm) * (ff // tf)


def _ffn(u, w_gu, w_down, side=(), tm=512, tf=FFN_TF):
    m, d = u.shape
    ff = w_down.shape[0]
    nf = ff // tf
    row = pl.BlockSpec((tm, d), lambda i, j: (i, 0))
    side_specs, side_blocks = _side_plan(side, (m // tm) * nf, lambda i, j: i * nf + j)
    outs = pl.pallas_call(
        functools.partial(_ffn_kernel, side_blocks=side_blocks),
        grid=(m // tm, nf),
        in_specs=[row,
                  pl.BlockSpec((d, tf), lambda i, j: (0, j)),
                  pl.BlockSpec((d, tf), lambda i, j: (0, nf + j)),
                  pl.BlockSpec((tf, d), lambda i, j: (j, 0))] + side_specs,
        out_specs=[row] + side_specs,
        out_shape=[jax.ShapeDtypeStruct((m, d), F32)]
        + [jax.ShapeDtypeStruct(w.shape, BF16) for w in side],
        compiler_params=_cparams(("arbitrary", "arbitrary")),
        name="dense_ffn",
    )(u, w_gu, w_gu, w_down, *side)
    return outs[0], list(outs[1:])


def _mla_proj_kernel(h_ref, y_ref, gate_ref, g2_ref, gq_ref, scq_ref, shq_ref, gk_ref, sck_ref, shk_ref,
                     wqa_ref, qg_ref, wqm_ref, wqr_ref,
                     wkva_ref, wkr_ref, wkrr_ref, kg_ref, wk_ref, wv_ref,
                     cos_ref, sin_ref, hn_ref, q_ref, k_ref, v_ref):
    h_new = h_ref[...] + gate_ref[0] * (_rms(y_ref[...]) * g2_ref[...])
    hn_ref[...] = h_new
    xhat = _rms(h_new)
    uq = (xhat * gq_ref[...] * (1.0 + scq_ref[0]) + shq_ref[0]).astype(BF16)
    xk = (xhat * gk_ref[...] * (1.0 + sck_ref[0]) + shk_ref[0]).astype(BF16)
    cos = cos_ref[...]
    sin = sin_ref[...]
    qn = (_rms(_dot(uq, wqa_ref[...])) * qg_ref[...]).astype(BF16)
    cn = (_rms(_dot(xk, wkva_ref[...])) * kg_ref[...]).astype(BF16)
    krope = (_dot(xk, wkr_ref[...]) * cos + _dot(xk, wkrr_ref[...]) * sin).astype(BF16)
    ones = jnp.ones((cn.shape[0], V_HEAD), BF16)
    for hp in range(MLA_HEADS // 2):
        ps = slice(hp * 2 * LANES, (hp + 1) * 2 * LANES)
        qr2 = _dot(qn, wqr_ref[:, ps]) * Q_SCALE
        kn2 = _dot(cn, wk_ref[:, ps]).astype(BF16)
        v2 = _dot(cn, wv_ref[:, ps]).astype(BF16)
        for sub in range(2):
            hd = 2 * hp + sub
            ls = slice(sub * LANES, (sub + 1) * LANES)
            qm = _dot(qn, wqm_ref[:, hd * QK_PAD:(hd + 1) * QK_PAD]) * Q_SCALE
            q_ref[:, hd * QK_PAD:hd * QK_PAD + QK_NOPE] = qm[:, :QK_NOPE].astype(BF16)
            q_ref[:, hd * QK_PAD + QK_NOPE:(hd + 1) * QK_PAD] = (
                qm[:, QK_NOPE:] * cos + qr2[:, ls] * sin).astype(BF16)
            k_ref[:, hd * QK_PAD:hd * QK_PAD + QK_NOPE] = kn2[:, ls]
            k_ref[:, hd * QK_PAD + QK_NOPE:(hd + 1) * QK_PAD] = krope
            v_ref[:, 2 * hd * V_HEAD:(2 * hd + 1) * V_HEAD] = v2[:, ls]
            v_ref[:, (2 * hd + 1) * V_HEAD:(2 * hd + 2) * V_HEAD] = ones


def _mla_proj(h, y, gate, g2, gq, scq, shq, gk, sck, shk, wts, cos, sin, rows_per_batch, tm=256):
    m, d = h.shape
    bpb = rows_per_batch // tm
    row = lambda w: pl.BlockSpec((tm, w), lambda i: (i, 0))
    vec = lambda w: pl.BlockSpec((1, w), lambda i: (0, 0))
    mod = pl.BlockSpec((1, 1, d), lambda i: (i // bpb, 0, 0))
    full = lambda a: pl.BlockSpec(a.shape, lambda i: (0, 0), pipeline_mode=pl.Buffered(1))
    wqa, qg, wqm, wqr, wkva, wkr, wkrr, kg, wk, wv = wts
    return pl.pallas_call(
        _mla_proj_kernel,
        grid=(m // tm,),
        in_specs=[row(d), row(d), mod, vec(d), vec(d), mod, mod, vec(d), mod, mod,
                  full(wqa), full(qg), full(wqm), full(wqr),
                  full(wkva), full(wkr), full(wkrr), full(kg), full(wk), full(wv),
                  row(LANES), row(LANES)],
        out_specs=[row(d), row(MLA_HEADS * QK_PAD), row(MLA_HEADS * QK_PAD),
                   row(MLA_HEADS * 2 * V_HEAD)],
        out_shape=[jax.ShapeDtypeStruct((m, d), F32),
                   jax.ShapeDtypeStruct((m, MLA_HEADS * QK_PAD), BF16),
                   jax.ShapeDtypeStruct((m, MLA_HEADS * QK_PAD), BF16),
                   jax.ShapeDtypeStruct((m, MLA_HEADS * 2 * V_HEAD), BF16)],
        compiler_params=_cparams(("arbitrary",)),
        name="mla_proj",
    )(h, y, gate, g2, gq, scq, shq, gk, sck, shk, wqa, qg, wqm, wqr, wkva, wkr, wkrr, kg, wk, wv,
      cos, sin)


def _flash_kernel(*refs, tq, tk, slab, side_blocks):
    n_side = len(side_blocks)
    q_ref, k_ref, v_ref = refs[:3]
    o_ref = refs[3 + n_side]
    s_a, s_b, p_scr, m_scr, al_scr, acc_scr = refs[4 + 2 * n_side:]
    qi = pl.program_id(2)
    _side_cast(_grid_step_id(3), refs[3:3 + n_side], refs[4 + n_side:4 + 2 * n_side], side_blocks)

    s_bufs = (s_a, s_b)
    m_scr[...] = jnp.full_like(m_scr, -jnp.inf)
    acc_scr[...] = jnp.zeros_like(acc_scr)
    n_col = tk // LANES
    n_diag = tq // tk

    def kv_rows(j):
        return pl.ds(pl.multiple_of(j * tk, tk), tk)

    half = tk // 2

    def kv_half(j):
        return pl.ds(pl.multiple_of(j * tk, half), half)

    def scores(j, dst, row0, diag=False):
        if diag:
            dst[row0:row0 + half, :half] = _dot_nt(q_ref[row0:row0 + half, :], k_ref[kv_half(j), :])
            row0 += half
        dst[row0:, :] = _dot_nt(q_ref[row0:, :], k_ref[kv_rows(j), :])

    def rescale(rows, pv):
        al = al_scr[rows, :]
        acc_scr[rows, :V_HEAD] = acc_scr[rows, :V_HEAD] * al + pv[:, :V_HEAD]
        acc_scr[rows, V_HEAD:] = acc_scr[rows, V_HEAD:] * al + pv[:, V_HEAD:]

    def consume(j, src, row0, diag):
        for r0 in range(row0, tq, slab):
            rows = slice(r0, r0 + slab)
            rel = r0 - row0
            top = diag and rel < half
            n_use = n_col // 2 if top else n_col
            cols = [src[rows, cb * LANES:(cb + 1) * LANES] for cb in range(n_use)]
            if diag and rel < tk:
                rowp = rel + lax.broadcasted_iota(jnp.int32, (slab, LANES), 0)
                lane = lax.broadcasted_iota(jnp.int32, (slab, LANES), 1)
                first = 0 if top else n_col // 2
                cols = [cols[cb] if cb < first else
                        jnp.where(cb * LANES + lane <= rowp, cols[cb], -jnp.inf)
                        for cb in range(n_use)]
            mx = functools.reduce(jnp.maximum, cols)
            m_old = m_scr[rows, :]
            m_new = jnp.maximum(m_old, jnp.max(mx, axis=-1, keepdims=True))
            al_scr[rows, :] = jnp.exp2(m_old - m_new)
            m_scr[rows, :] = m_new
            for cb in range(n_use):
                p_scr[rows, cb * LANES:(cb + 1) * LANES] = jnp.exp2(cols[cb] - m_new).astype(BF16)
        if diag:
            top_rows = slice(row0, row0 + half)
            rescale(top_rows, _dot(p_scr[top_rows, :half], v_ref[kv_half(j), :]))
            row0 += half
        rescale(slice(row0, tq), _dot(p_scr[row0:, :], v_ref[kv_rows(j), :]))

    n_full = qi * n_diag
    scores(0, s_a, 0)

    def pair(pi, carry):
        for u in range(2):
            scores(2 * pi + u + 1, s_bufs[1 - u], 0)
            consume(2 * pi + u, s_bufs[u], 0, False)
        return carry

    lax.fori_loop(0, n_full // 2, pair, 0)
    for dg in range(n_diag):
        if dg + 1 < n_diag:
            scores(n_full + dg + 1, s_bufs[(dg + 1) % 2], (dg + 1) * tk, diag=True)
        consume(n_full + dg, s_bufs[dg % 2], dg * tk, True)

    o_ref[...] = (acc_scr[:, :V_HEAD] / acc_scr[:, V_HEAD:]).astype(o_ref.dtype)


def _flash_tiles(seq):
    tq = min(FLASH_TQ, seq)
    return tq, min(FLASH_TK, tq // 2)


def _flash(q, k, v, batch, side=(), slab=64):
    m = q.shape[0]
    s = m // batch
    tq, tk = _flash_tiles(s)
    nq = s // tq
    side_specs, side_blocks = _side_plan(
        side, batch * MLA_HEADS * nq, lambda b, h, i: (b * MLA_HEADS + h) * nq + i)
    outs = pl.pallas_call(
        functools.partial(_flash_kernel, tq=tq, tk=tk, slab=slab, side_blocks=side_blocks),
        grid=(batch, MLA_HEADS, nq),
        in_specs=[pl.BlockSpec((tq, QK_PAD), lambda b, h, i: (b * nq + i, h)),
                  pl.BlockSpec((s, QK_PAD), lambda b, h, i: (b, h)),
                  pl.BlockSpec((s, 2 * V_HEAD), lambda b, h, i: (b, h))] + side_specs,
        out_specs=[pl.BlockSpec((tq, V_HEAD), lambda b, h, i: (b * nq + i, h))] + side_specs,
        out_shape=[jax.ShapeDtypeStruct((m, MLA_HEADS * V_HEAD), BF16)]
        + [jax.ShapeDtypeStruct(w.shape, BF16) for w in side],
        scratch_shapes=[pltpu.VMEM((tq, tk), F32), pltpu.VMEM((tq, tk), F32),
                        pltpu.VMEM((tq, tk), BF16), pltpu.VMEM((tq, LANES), F32),
                        pltpu.VMEM((tq, LANES), F32), pltpu.VMEM((tq, 2 * V_HEAD), F32)],
        compiler_params=_cparams(("arbitrary", "arbitrary", "arbitrary")),
        name="mla_flash",
    )(q, k, v, *side)
    return outs[0], list(outs[1:])


def _split3(x):
    hi = x.astype(BF16)
    lo = (x - hi.astype(F32)).astype(BF16)
    return hi, lo


def _moe_pre_kernel(h_ref, g_ref, sc_ref, sh_ref, wr_ref, xs_ref, meta_ref, wts_ref, cnt_ref,
                    run_scr):
    tm, d = h_ref.shape

    @pl.when(pl.program_id(0) == 0)
    def _():
        run_scr[...] = jnp.zeros_like(run_scr)

    u = _rms(h_ref[...]) * g_ref[...] * (1.0 + sc_ref[0]) + sh_ref[0]
    xs_ref[...] = u

    u_hi, u_lo = _split3(u)
    w_hi, w_lo = _split3(wr_ref[...])
    logits = _dot(u_hi, w_hi) + (_dot(u_hi, w_lo) + _dot(u_lo, w_hi))
    lane = lax.broadcasted_iota(jnp.int32, logits.shape, 1)
    lg = jnp.where(lane < N_EXPERTS, logits, -jnp.inf)
    m1 = jnp.max(lg, axis=-1, keepdims=True)
    i1 = jnp.min(jnp.where(lg == m1, lane, LANES), axis=-1, keepdims=True)
    lg2 = jnp.where(lane == i1, -jnp.inf, lg)
    m2 = jnp.max(lg2, axis=-1, keepdims=True)
    i2 = jnp.min(jnp.where(lg2 == m2, lane, LANES), axis=-1, keepdims=True)
    e = jnp.exp(m2 - m1)
    w1 = 1.0 / (1.0 + e)
    w2 = e * w1

    oh1 = lane == i1
    oh2 = lane == i2
    cnt = oh1.astype(F32) + oh2.astype(F32)
    r = lax.broadcasted_iota(jnp.int32, (tm, tm), 0)
    c = lax.broadcasted_iota(jnp.int32, (tm, tm), 1)
    strict = (r > c).astype(BF16)
    prefix = _dot(strict, cnt.astype(BF16)) + run_scr[...]
    rank1 = jnp.sum(jnp.where(oh1, prefix, 0.0), axis=-1, keepdims=True).astype(jnp.int32)
    rank2 = jnp.sum(jnp.where(oh2, prefix, 0.0), axis=-1, keepdims=True).astype(jnp.int32)
    run_scr[...] += jnp.sum(cnt, axis=0, keepdims=True)

    meta_ref[...] = jnp.where(lane == 0, i1, jnp.where(lane == 1, i2,
                              jnp.where(lane == 2, rank1, jnp.where(lane == 3, rank2, 0))))
    wts_ref[...] = jnp.where(lane == 0, w1, jnp.where(lane == 1, w2, 0.0))
    cnt_ref[...] = run_scr[...]


def _moe_pre(h, g, scale, shift, w_router_pad, rows_per_batch, tm=512):
    m, d = h.shape
    bpb = rows_per_batch // tm
    row = lambda w: pl.BlockSpec((tm, w), lambda i: (i, 0))
    vec = lambda w: pl.BlockSpec((1, w), lambda i: (0, 0))
    mod = pl.BlockSpec((1, 1, d), lambda i: (i // bpb, 0, 0))
    return pl.pallas_call(
        _moe_pre_kernel,
        grid=(m // tm,),
        in_specs=[row(d), vec(d), mod, mod, pl.BlockSpec((d, LANES), lambda i: (0, 0))],
        out_specs=[row(d), row(LANES), row(LANES), vec(LANES)],
        out_shape=[jax.ShapeDtypeStruct((m, d), F32),
                   jax.ShapeDtypeStruct((m, LANES), jnp.int32),
                   jax.ShapeDtypeStruct((m, LANES), F32),
                   jax.ShapeDtypeStruct((1, LANES), F32)],
        scratch_shapes=[pltpu.VMEM((1, LANES), F32)],
        compiler_params=_cparams(("arbitrary",)),
        name="moe_route",
    )(h, g, scale, shift, w_router_pad)


ROW_COPY_UNROLL = 8


def _issue_row_copies(n_rows, copies_of_row):
    def trip(t, carry):
        n = 0
        for u in range(ROW_COPY_UNROLL):
            for cp in copies_of_row(t * ROW_COPY_UNROLL + u):
                cp.start(priority=n % 2)
                n += 1
        return carry
    lax.fori_loop(0, n_rows // ROW_COPY_UNROLL, trip, 0)


def _moe_dispatch_kernel(fill_ref, dest_ref, x_ref, xs_ref, zbuf, sem, zsem, *, tmb, n_tail):
    tm = x_ref.shape[0]

    @pl.when(pl.program_id(0) == 0)
    def _():
        zbuf[...] = jnp.zeros_like(zbuf)

        def pad_row(e, r):
            return pltpu.make_async_copy(zbuf.at[pl.ds(0, 1), :],
                                         xs_ref.at[pl.ds(fill_ref[2 * e] + r, 1), :], zsem)

        def tail_block(b):
            start = pl.multiple_of(fill_ref[2 * N_EXPERTS] + b * tmb, tmb)
            return pltpu.make_async_copy(zbuf, xs_ref.at[pl.ds(start, tmb), :], zsem)

        for e in range(N_EXPERTS):
            def start_row(r, carry, e=e):
                pad_row(e, r).start()
                return carry
            lax.fori_loop(0, fill_ref[2 * e + 1], start_row, 0)
        n_blk = fill_ref[2 * N_EXPERTS + 1] // tmb
        for b in range(n_tail):
            @pl.when(b < n_blk)
            def _(b=b):
                tail_block(b).start()
        for e in range(N_EXPERTS):
            def wait_row(r, carry, e=e):
                pad_row(e, r).wait()
                return carry
            lax.fori_loop(0, fill_ref[2 * e + 1], wait_row, 0)
        for b in range(n_tail):
            @pl.when(b < n_blk)
            def _(b=b):
                tail_block(b).wait()

    def copies_of_row(r):
        return [pltpu.make_async_copy(x_ref.at[pl.ds(r, 1), :],
                                      xs_ref.at[pl.ds(dest_ref[0, 0, 2 * r + k], 1), :], sem)
                for k in range(2)]

    _issue_row_copies(tm, copies_of_row)
    for _ in range(2):
        pltpu.make_async_copy(x_ref, x_ref, sem).wait()


def _moe_dispatch(x, dest, fill, n_rows, tmb, tm=256):
    m, w = x.shape
    nb = m // tm
    n_tail = (n_rows - 2 * m) // tmb
    return pl.pallas_call(
        functools.partial(_moe_dispatch_kernel, tmb=tmb, n_tail=n_tail),
        grid_spec=pltpu.PrefetchScalarGridSpec(
            num_scalar_prefetch=1,
            grid=(nb,),
            in_specs=[pl.BlockSpec((1, 1, 2 * tm), lambda i, fl: (i, 0, 0), memory_space=pltpu.SMEM),
                      pl.BlockSpec((tm, w), lambda i, fl: (i, 0))],
            out_specs=pl.BlockSpec(memory_space=pl.ANY),
            scratch_shapes=[pltpu.VMEM((tmb, w), F32), pltpu.SemaphoreType.DMA(()),
                            pltpu.SemaphoreType.DMA(())],
        ),
        out_shape=jax.ShapeDtypeStruct((n_rows, w), F32),
        compiler_params=_cparams(("arbitrary",)),
        name="moe_dispatch",
    )(fill, dest.reshape(nb, 1, 2 * tm), x)


def _moe_ffn_kernel(blk_e_ref, n_used_ref, xs_ref, wg_ref, wu_ref, wd_ref, y_ref, u_scr):
    i = pl.program_id(0)
    j = pl.program_id(1)

    @pl.when(i < n_used_ref[0])
    def _():
        @pl.when(j == 0)
        def _():
            u_scr[...] = xs_ref[...].astype(BF16)
            y_ref[...] = jnp.zeros_like(y_ref)

        u = u_scr[...]
        gt = _dot(u, wg_ref[0])
        up = _dot(u, wu_ref[0])
        a = (gt * _sigmoid(gt) * up).astype(BF16)
        y_ref[...] += _dot(a, wd_ref[0])

    @pl.when((i >= n_used_ref[0]) & (j == 0))
    def _():
        y_ref[...] = jnp.zeros_like(y_ref)


def _moe_ffn(xs, blk_e, n_used, w_gu, w_down, tmb, tf=FFN_TF):
    n_rows, d = xs.shape
    ff = w_down.shape[1]
    nf = ff // tf
    nb = n_rows // tmb

    def blk(i, n_used_ref):
        return jnp.minimum(i, n_used_ref[0] - 1)

    def fidx(i, j, n_used_ref):
        return jnp.where(i < n_used_ref[0], j, nf - 1)

    grid_spec = pltpu.PrefetchScalarGridSpec(
        num_scalar_prefetch=2,
        grid=(nb, nf),
        in_specs=[
            pl.BlockSpec((tmb, d), lambda i, j, be, nu: (blk(i, nu), 0)),
            pl.BlockSpec((1, d, tf), lambda i, j, be, nu: (be[blk(i, nu)], 0, fidx(i, j, nu))),
            pl.BlockSpec((1, d, tf), lambda i, j, be, nu: (be[blk(i, nu)], 0, nf + fidx(i, j, nu))),
            pl.BlockSpec((1, tf, d), lambda i, j, be, nu: (be[blk(i, nu)], fidx(i, j, nu), 0)),
        ],
        out_specs=pl.BlockSpec((tmb, d), lambda i, j, be, nu: (i, 0)),
        scratch_shapes=[pltpu.VMEM((tmb, d), BF16)],
    )
    return pl.pallas_call(
        _moe_ffn_kernel,
        grid_spec=grid_spec,
        out_shape=jax.ShapeDtypeStruct((n_rows, d), F32),
        compiler_params=_cparams(("arbitrary", "arbitrary")),
        name="moe_grouped_ffn",
    )(blk_e, n_used, xs, w_gu, w_gu, w_down)


def _moe_combine_kernel(dest_ref, ys_ref, wts_ref, h_ref, gate_ref, g2_ref, out_ref,
                        buf0, buf1, sem):
    tm = h_ref.shape[0]
    bufs = (buf0, buf1)

    def copies_of_row(r):
        return [pltpu.make_async_copy(ys_ref.at[pl.ds(dest_ref[0, 0, 2 * r + k], 1), :],
                                      bufs[k].at[pl.ds(r, 1), :], sem)
                for k in range(2)]

    _issue_row_copies(tm, copies_of_row)
    for k in range(2):
        pltpu.make_async_copy(bufs[k], bufs[k], sem).wait()

    wts = wts_ref[...]
    y = wts[:, 0:1] * buf0[...] + wts[:, 1:2] * buf1[...]
    out_ref[...] = h_ref[...] + gate_ref[0] * (_rms(y) * g2_ref[...])


def _moe_combine(ys, dest, wts, h, gate, g2, rows_per_batch, tm=256):
    m, d = h.shape
    nb = m // tm
    bpb = rows_per_batch // tm
    row = lambda w: pl.BlockSpec((tm, w), lambda i: (i, 0))
    return pl.pallas_call(
        _moe_combine_kernel,
        grid=(nb,),
        in_specs=[pl.BlockSpec((1, 1, 2 * tm), lambda i: (i, 0, 0), memory_space=pltpu.SMEM),
                  pl.BlockSpec(memory_space=pl.ANY),
                  row(LANES), row(d),
                  pl.BlockSpec((1, 1, d), lambda i: (i // bpb, 0, 0)),
                  pl.BlockSpec((1, d), lambda i: (0, 0))],
        out_specs=row(d),
        out_shape=jax.ShapeDtypeStruct((m, d), F32),
        scratch_shapes=[pltpu.VMEM((tm, d), F32), pltpu.VMEM((tm, d), F32),
                        pltpu.SemaphoreType.DMA(())],
        compiler_params=_cparams(("arbitrary",)),
        name="moe_combine",
    )(dest.reshape(nb, 1, 2 * tm), ys, wts, h, gate, g2)


def _moe(h, g, scale, shift, w_router, w_gu, w_down, gate, g2, rows_per_batch, tmb=MOE_BLOCK_ROWS):
    m, d = h.shape
    n_rows = (2 * m + N_EXPERTS * (tmb - 1)) // tmb * tmb
    nb = n_rows // tmb
    w_router_pad = jnp.pad(w_router, ((0, 0), (0, LANES - N_EXPERTS)))
    u, meta, wts, cnt = _moe_pre(h, g, scale, shift, w_router_pad, rows_per_batch)

    counts = cnt[0, :N_EXPERTS].astype(jnp.int32)
    padded = (counts + tmb - 1) // tmb * tmb
    pend = jnp.cumsum(padded)
    pstart = pend - padded
    dest = (pstart[meta[:, 0:2]] + meta[:, 2:4]).astype(jnp.int32).reshape(-1)
    blk_start = jnp.arange(nb, dtype=jnp.int32) * tmb
    blk_e = jnp.minimum(jnp.sum(pend[None, :] <= blk_start[:, None], axis=1),
                        N_EXPERTS - 1).astype(jnp.int32)
    n_used = (pend[-1:] // tmb).astype(jnp.int32)

    fill = jnp.stack([jnp.concatenate([pstart + counts, pend[-1:]]),
                      jnp.concatenate([padded - counts, n_rows - pend[-1:]])],
                     axis=1).reshape(-1).astype(jnp.int32)
    xs = _moe_dispatch(u, dest, fill, n_rows, tmb)
    ys = _moe_ffn(xs, blk_e, n_used, w_gu, w_down, tmb)
    return _moe_combine(ys, dest, wts, h, gate, g2, rows_per_batch)


def _swap_halves(w):
    half = w.shape[-1] // 2
    return jnp.concatenate([w[..., half:], w[..., :half]], axis=-1)


def _mla_weights(w_q_a, q_norm_g, w_q_b, w_kv_a, kv_norm_g, w_kv_b):
    ql = w_q_b.shape[0]
    kvl = w_kv_b.shape[0]
    wq = w_q_b.reshape(ql, MLA_HEADS, QK_NOPE + QK_ROPE)
    rope = wq[:, :, QK_NOPE:]
    pad_r = LANES - QK_ROPE
    wqm = jnp.concatenate([wq, jnp.zeros((ql, MLA_HEADS, QK_PAD - QK_NOPE - QK_ROPE), F32)], axis=-1)
    wqr = jnp.concatenate([_swap_halves(rope), jnp.zeros((ql, MLA_HEADS, pad_r), F32)], axis=-1)
    wkv = w_kv_b.reshape(kvl, MLA_HEADS, QK_NOPE + V_HEAD)
    kr = w_kv_a[:, kvl:]
    dm = w_kv_a.shape[0]
    wkr = jnp.concatenate([kr, jnp.zeros((dm, pad_r), F32)], axis=-1)
    wkrr = jnp.concatenate([_swap_halves(kr), jnp.zeros((dm, pad_r), F32)], axis=-1)
    bf = lambda a: a.astype(BF16)
    return (bf(w_q_a), q_norm_g.reshape(1, -1), bf(wqm.reshape(ql, -1)), bf(wqr.reshape(ql, -1)),
            bf(w_kv_a[:, :kvl]), bf(wkr), bf(wkrr), kv_norm_g.reshape(1, -1),
            bf(wkv[:, :, :QK_NOPE].reshape(kvl, -1)), bf(wkv[:, :, QK_NOPE:].reshape(kvl, -1)))


def kernel(x, c, positions, ada_w, ada_b, norm_g, hg_w_in, hg_lb_logits, hg_out_norm_g, hg_w_out, kv_src_norm_g, kv_src_ada_w, kv_src_ada_b, mla_w_kv_a, mla_kv_norm_g, mla_w_kv_b, mla_w_q_a, mla_q_norm_g, mla_w_q_b, mla_w_o, ffn_w_gu, ffn_w_down, moe_w_router, moe_w_gu, moe_w_down):
    batch, seq, d = x.shape
    m = batch * seq
    bf = lambda a: a.astype(BF16)

    c8 = jnp.pad(c, ((0, 8 - batch), (0, 0)))
    ada = _modulation(c8, ada_w.reshape(4, d, 3 * d), ada_b.reshape(4, 3 * d))[:, :batch]
    kvm = _modulation(c8, kv_src_ada_w[None], kv_src_ada_b[None])[0, :batch]

    def mods(idx):
        a = ada[idx]
        return [a[:, i * d:(i + 1) * d].reshape(batch, 1, d) for i in range(3)]

    vec = lambda a: a.reshape(1, -1)
    lb_all = jnp.cumsum(jax.nn.softmax(hg_lb_logits.astype(F32), axis=0), axis=0)

    h = x.reshape(m, d)

    shift, scale, gate = mods(0)
    q, f_logit, iv, gg = _hg_proj(h, vec(norm_g[0, 0, 0]), scale, shift, bf(hg_w_in[0]), seq)
    o, (ffn_gu_bf, ffn_down_bf, hg_out_bf, moe_down_bf) = _with_side(
        lambda side: _hgrn2(q, f_logit, iv, vec(lb_all[0]), batch, side=side),
        [ffn_w_gu[0], ffn_w_down[0], hg_w_out[0], moe_w_down[0]], _hgrn2_steps(m))
    shift_f, scale_f, gate_f = mods(1)
    h, u_ffn = _oproj(o, hg_out_bf, h, gate, vec(norm_g[0, 0, 1]), seq,
                      gg=gg, gout=vec(hg_out_norm_g[0]),
                      nxt=(vec(norm_g[0, 1, 0]), scale_f, shift_f))

    y_ffn, (moe_gu_bf, mla_o_bf) = _with_side(
        lambda side: _ffn(u_ffn, ffn_gu_bf, ffn_down_bf, side=side),
        [moe_w_gu[0], mla_w_o[0]], _ffn_steps(m, ffn_w_down.shape[1]))

    shift_k, scale_k = [kvm[:, i * d:(i + 1) * d].reshape(batch, 1, d) for i in range(2)]
    shift, scale, gate = mods(2)
    cos, sin = _rope_tables(positions.reshape(m, 1).astype(F32))
    wts = _mla_weights(mla_w_q_a[0], mla_q_norm_g[0], mla_w_q_b[0],
                       mla_w_kv_a, mla_kv_norm_g, mla_w_kv_b)
    h, qh, kh, vh = _mla_proj(h, y_ffn, gate_f, vec(norm_g[0, 1, 1]),
                              vec(norm_g[1, 0, 0]), scale, shift,
                              vec(kv_src_norm_g), scale_k, shift_k, wts, cos, sin, seq)
    att, _ = _flash(qh, kh, vh, batch)
    h = _oproj(att, mla_o_bf, h, gate, vec(norm_g[1, 0, 1]), seq)

    shift, scale, gate = mods(3)
    h = _moe(h, vec(norm_g[1, 1, 0]), scale, shift, moe_w_router[0],
             moe_gu_bf, moe_down_bf, gate, vec(norm_g[1, 1, 1]), seq)
    return h.reshape(batch, seq, d)
```

```python
import functools

import jax
import jax.numpy as jnp
from jax import lax
from jax.experimental import pallas as pl
from jax.experimental.pallas import tpu as pltpu

F32 = jnp.float32
BF16 = jnp.bfloat16

EPS = 1e-6
LANES = 128
F32_SUBLANES = 8
BF16_SUBLANES = 16
SIDE_BLOCK_BYTES = 3 * 1024 * 1024

HG_HEADS = 16
HG_KDIM = 128
HG_CHUNK = 64
HG_SUB = 16
HG_EXP2_CLAMP = 115.0
HG_HEADS_PER_STEP = 4

MLA_HEADS = 16
QK_NOPE = 128
QK_ROPE = 64
V_HEAD = 128
QK_PAD = 256
MLA_SCALE = (QK_NOPE + QK_ROPE) ** -0.5
Q_SCALE = MLA_SCALE * 1.4426950408889634
ROPE_THETA = 10000.0
FLASH_TQ = 2048
FLASH_TK = 1024

N_EXPERTS = 8
MOE_BLOCK_ROWS = 768
FFN_TF = 512

VMEM_LIMIT = 56 * 1024 * 1024


def _cparams(sem):
    return pltpu.CompilerParams(dimension_semantics=sem, vmem_limit_bytes=VMEM_LIMIT)


def _rms(x):
    return x * lax.rsqrt(jnp.mean(x * x, axis=-1, keepdims=True) + EPS)


def _sigmoid(x):
    return 1.0 / (1.0 + jnp.exp(-x))


def _dot(a, b):
    return jnp.dot(a, b, preferred_element_type=F32)


def _dot_nt(a, b):
    return lax.dot_general(a, b, (((1,), (1,)), ((), ())), preferred_element_type=F32)


def _dot_tn(a, b):
    return lax.dot_general(a, b, (((0,), (0,)), ((), ())), preferred_element_type=F32)


def _side_rows(w, n_steps):
    e, r, c = w.shape
    for rb in range(BF16_SUBLANES, r + 1, BF16_SUBLANES):
        if r % rb == 0 and e * (r // rb) <= n_steps:
            return rb if rb * c * 4 <= SIDE_BLOCK_BYTES else None
    return None


def _side_ok(weights, n_steps):
    return all(_side_rows(w, n_steps) is not None for w in weights)


def _side_plan(weights, n_steps, step_of):
    specs, blocks = [], []
    for w in weights:
        rb = _side_rows(w, n_steps)
        per = w.shape[1] // rb
        n_blk = w.shape[0] * per
        blocks.append(n_blk)

        def idx(*grid_ids, per=per, n_blk=n_blk):
            blk = jnp.minimum(step_of(*grid_ids), n_blk - 1)
            return (blk // per, blk % per, 0)

        specs.append(pl.BlockSpec((1, rb, w.shape[2]), idx))
    return specs, tuple(blocks)


def _side_cast(step_id, w_ins, w_outs, blocks):
    for w_in, w_out, n_blk in zip(w_ins, w_outs, blocks):
        @pl.when(step_id < n_blk)
        def _(w_in=w_in, w_out=w_out):
            w_out[...] = w_in[...].astype(w_out.dtype)


def _grid_step_id(n_axes):
    sid = pl.program_id(0)
    for ax in range(1, n_axes):
        sid = sid * pl.num_programs(ax) + pl.program_id(ax)
    return sid


def _with_side(run, weights, n_steps):
    w3 = tuple(w.reshape((1,) + w.shape) if w.ndim == 2 else w for w in weights)
    if _side_ok(w3, n_steps):
        res, copies = run(w3)
        return res, [cp.reshape(w.shape) for cp, w in zip(copies, weights)]
    res, _ = run(())
    return res, [w.astype(BF16) for w in weights]


def _mod_kernel(c_ref, w_ref, b_ref, o_ref):
    c = c_ref[...]
    sc = (c * _sigmoid(c)).astype(BF16)
    o_ref[0] = _dot(sc, w_ref[0].astype(BF16)) + b_ref[0]


def _modulation(c8, w, b, tn=1024):
    g, d, n = w.shape
    return pl.pallas_call(
        _mod_kernel,
        grid=(g, n // tn),
        in_specs=[pl.BlockSpec((8, d), lambda i, j: (0, 0)),
                  pl.BlockSpec((1, d, tn), lambda i, j: (i, 0, j)),
                  pl.BlockSpec((1, 1, tn), lambda i, j: (i, 0, j))],
        out_specs=pl.BlockSpec((1, 8, tn), lambda i, j: (i, 0, j)),
        out_shape=jax.ShapeDtypeStruct((g, 8, n), F32),
        compiler_params=_cparams(("arbitrary", "arbitrary")),
        name="modulation",
    )(c8, w, b.reshape(g, 1, n))


def _rope_kernel(pos_ref, inv_ref, cos_ref, sin_ref):
    ang = pos_ref[...] * inv_ref[...]
    lane = lax.broadcasted_iota(jnp.int32, ang.shape, 1)
    half = QK_ROPE // 2
    cos_ref[...] = jnp.where(lane < QK_ROPE, jnp.cos(ang), 0.0)
    sin_ref[...] = jnp.where(lane < half, -jnp.sin(ang),
                             jnp.where(lane < QK_ROPE, jnp.sin(ang), 0.0))


def _rope_tables(pos, tm=512):
    m = pos.shape[0]
    half = QK_ROPE // 2
    inv = 1.0 / (ROPE_THETA ** (jnp.arange(0, QK_ROPE, 2, dtype=F32) / QK_ROPE))
    inv128 = jnp.concatenate([inv, inv, jnp.zeros((LANES - 2 * half,), F32)]).reshape(1, LANES)
    return pl.pallas_call(
        _rope_kernel,
        grid=(m // tm,),
        in_specs=[pl.BlockSpec((tm, 1), lambda i: (i, 0)),
                  pl.BlockSpec((1, LANES), lambda i: (0, 0))],
        out_specs=[pl.BlockSpec((tm, LANES), lambda i: (i, 0))] * 2,
        out_shape=[jax.ShapeDtypeStruct((m, LANES), F32)] * 2,
        compiler_params=_cparams(("arbitrary",)),
        name="rope_tables",
    )(pos, inv128)


def _hg_proj_kernel(h_ref, g_ref, sc_ref, sh_ref, w_ref, q_ref, f_ref, i_ref, gg_ref, u_scr):
    j = pl.program_id(1)

    @pl.when(j == 0)
    def _():
        u = _rms(h_ref[...]) * g_ref[...] * (1.0 + sc_ref[0]) + sh_ref[0]
        u_scr[...] = u.astype(BF16)

    r = _dot(u_scr[...], w_ref[...])
    for idx, ref in enumerate((q_ref, f_ref, i_ref, gg_ref)):
        @pl.when(j == idx)
        def _(ref=ref):
            ref[...] = r.astype(ref.dtype)


def _hg_proj(h, g, scale, shift, w_in, rows_per_batch, tm=512):
    m, d = h.shape
    bpb = rows_per_batch // tm
    row = lambda i, j: (i, 0)
    mod = lambda i, j: (i // bpb, 0, 0)
    return pl.pallas_call(
        _hg_proj_kernel,
        grid=(m // tm, 4),
        in_specs=[pl.BlockSpec((tm, d), row),
                  pl.BlockSpec((1, d), lambda i, j: (0, 0)),
                  pl.BlockSpec((1, 1, d), mod),
                  pl.BlockSpec((1, 1, d), mod),
                  pl.BlockSpec((d, d), lambda i, j: (0, j))],
        out_specs=[pl.BlockSpec((tm, d), row)] * 4,
        out_shape=[jax.ShapeDtypeStruct((m, d), BF16), jax.ShapeDtypeStruct((m, d), F32),
                   jax.ShapeDtypeStruct((m, d), BF16), jax.ShapeDtypeStruct((m, d), BF16)],
        scratch_shapes=[pltpu.VMEM((tm, d), BF16)],
        compiler_params=_cparams(("arbitrary", "arbitrary")),
        name="hg_proj",
    )(h, g, scale, shift, w_in)


def _hgrn2_kernel(*refs, n_chunks, side_blocks):
    n_side = len(side_blocks)
    q_ref, f_ref, v_ref, lb_ref = refs[:4]
    o_ref = refs[4 + n_side]
    st_ref = refs[5 + 2 * n_side]
    c, sub = HG_CHUNK, HG_SUB
    n_sub = c // sub

    @pl.when(pl.program_id(2) == 0)
    def _():
        st_ref[...] = jnp.zeros_like(st_ref)

    _side_cast(_grid_step_id(3), refs[4:4 + n_side], refs[5 + n_side:5 + 2 * n_side], side_blocks)

    row = lax.broadcasted_iota(jnp.int32, (c, c), 0)
    col = lax.broadcasted_iota(jnp.int32, (c, c), 1)
    tri = (row >= col).astype(BF16)
    row2 = lax.broadcasted_iota(jnp.int32, (c, 2 * c), 0)
    col2 = lax.broadcasted_iota(jnp.int32, (c, 2 * c), 1)
    c_bits, sub_bits = c.bit_length() - 1, sub.bit_length() - 1
    mask2 = ((col2 >> c_bits) == ((row2 >> sub_bits) & 1)) & ((col2 & (c - 1)) <= row2)

    n_heads = q_ref.shape[1] // HG_KDIM
    units = [(slice(ci * c, (ci + 1) * c), slice(hh * HG_KDIM, (hh + 1) * HG_KDIM))
             for hh in range(n_heads) for ci in range(n_chunks)]
    ks, bs = [], []
    for sl, hs in units:
        lb = lb_ref[:, hs]
        f = lb + (1.0 - lb) * _sigmoid(f_ref[sl, hs])
        lf = jnp.log2(f)
        lf_hi = lf.astype(BF16)
        lf_lo = (lf - lf_hi.astype(F32)).astype(BF16)
        ks.append(1.0 - f)
        bs.append(_dot(tri, lf_hi) + _dot(tri, lf_lo))
    a_fulls, qbs, kdecs, decays = [], [], [], []
    for (sl, hs), k, b in zip(units, ks, bs):
        q = q_ref[sl, hs].astype(F32)
        b_last = b[c - 1:c, :]
        starts = [jnp.zeros_like(b_last)] + [b[j * sub - 1:j * sub, :] for j in range(1, n_sub)]
        rrow = jnp.concatenate([jnp.broadcast_to(r, (sub, r.shape[1])) for r in starts], axis=0)
        qe = q * jnp.exp2(b - rrow)
        qbs.append((qe * jnp.exp2(rrow)).astype(BF16))
        ke = jnp.concatenate(
            [k * jnp.exp2(jnp.minimum(r - b, HG_EXP2_CLAMP)) for r in starts], axis=0)
        a_fulls.append(_dot_nt(qe.astype(BF16), ke.astype(BF16)))
        kdecs.append((k * jnp.exp2(b_last - b)).astype(BF16))
        decays.append(jnp.exp2(b_last))
    o_intras, incs = [], []
    for (sl, hs), a_full, kdec in zip(units, a_fulls, kdecs):
        v = v_ref[sl, hs]
        a2 = jnp.concatenate(
            [a_full[j * sub:(j + 1) * sub, (j // 2) * 2 * c:(j // 2 + 1) * 2 * c]
             for j in range(n_sub)], axis=0)
        a2 = jnp.where(mask2, a2, 0.0).astype(BF16)
        o_intras.append(_dot(a2, jnp.concatenate([v, v], axis=0)))
        incs.append(_dot_tn(v, kdec))
    for hh in range(n_heads):
        st = st_ref[hh]
        for ui in range(hh * n_chunks, (hh + 1) * n_chunks):
            sl, hs = units[ui]
            o_ref[sl, hs] = o_intras[ui] + _dot_nt(qbs[ui], st.astype(BF16))
            st = st * decays[ui] + incs[ui]
        st_ref[hh] = st


def _hgrn2_steps(m, tb=512):
    return (HG_HEADS // HG_HEADS_PER_STEP) * (m // tb)


def _hgrn2(q, f_logit, v, lb, batch, side=(), tb=512):
    m, d = q.shape
    s = m // batch
    nt = s // tb
    hg = HG_HEADS // HG_HEADS_PER_STEP
    wd = HG_HEADS_PER_STEP * HG_KDIM
    blk = lambda b, h, t: (b * nt + t, h)
    side_specs, side_blocks = _side_plan(
        side, batch * hg * nt, lambda b, h, t: (b * hg + h) * nt + t)
    outs = pl.pallas_call(
        functools.partial(_hgrn2_kernel, n_chunks=tb // HG_CHUNK, side_blocks=side_blocks),
        grid=(batch, hg, nt),
        in_specs=[pl.BlockSpec((tb, wd), blk)] * 3
        + [pl.BlockSpec((1, wd), lambda b, h, t: (0, h))] + side_specs,
        out_specs=[pl.BlockSpec((tb, wd), blk)] + side_specs,
        out_shape=[jax.ShapeDtypeStruct((m, d), F32)]
        + [jax.ShapeDtypeStruct(w.shape, BF16) for w in side],
        scratch_shapes=[pltpu.VMEM((HG_HEADS_PER_STEP, HG_KDIM, HG_KDIM), F32)],
        compiler_params=_cparams(("arbitrary", "arbitrary", "arbitrary")),
        name="hgrn2_recurrence",
    )(q, f_logit, v, lb, *side)
    return outs[0], list(outs[1:])


def _oproj_kernel(*refs, gated):
    if gated:
        (o_ref, gg_ref, gout_ref, w_ref, h_ref, gate_ref, g2_ref, gn_ref, scn_ref, shn_ref,
         out_ref, un_ref) = refs
        gg = gg_ref[...].astype(F32)
        x = _rms(o_ref[...]) * gout_ref[...] * (gg * _sigmoid(gg))
    else:
        o_ref, w_ref, h_ref, gate_ref, g2_ref, out_ref = refs
        x = o_ref[...]
    y = _dot(x.astype(BF16), w_ref[...])
    h_new = h_ref[...] + gate_ref[0] * (_rms(y) * g2_ref[...])
    out_ref[...] = h_new
    if gated:
        un_ref[...] = (_rms(h_new) * gn_ref[...] * (1.0 + scn_ref[0]) + shn_ref[0]).astype(BF16)


def _oproj(o, w, h, gate, g2, rows_per_batch, gg=None, gout=None, nxt=None, tm=512):
    m, d = h.shape
    bpb = rows_per_batch // tm
    row = pl.BlockSpec((tm, d), lambda i: (i, 0))
    vec = pl.BlockSpec((1, d), lambda i: (0, 0))
    mod = pl.BlockSpec((1, 1, d), lambda i: (i // bpb, 0, 0))
    wsp = pl.BlockSpec(w.shape, lambda i: (0, 0), pipeline_mode=pl.Buffered(1))
    gated = gg is not None
    if gated:
        args = (o, gg, gout, w, h, gate, g2) + tuple(nxt)
        specs = [row, row, vec, wsp, row, mod, vec, vec, mod, mod]
        out_specs = [row, row]
        out_shape = [jax.ShapeDtypeStruct((m, d), F32), jax.ShapeDtypeStruct((m, d), BF16)]
    else:
        args, specs = (o, w, h, gate, g2), [row, wsp, row, mod, vec]
        out_specs, out_shape = row, jax.ShapeDtypeStruct((m, d), F32)
    return pl.pallas_call(
        functools.partial(_oproj_kernel, gated=gated),
        grid=(m // tm,),
        in_specs=specs,
        out_specs=out_specs,
        out_shape=out_shape,
        compiler_params=_cparams(("arbitrary",)),
        name="hg_out_proj" if gated else "mla_out_proj",
    )(*args)


def _ffn_kernel(*refs, side_blocks):
    n_side = len(side_blocks)
    u_ref, wg_ref, wu_ref, wd_ref = refs[:4]
    y_ref = refs[4 + n_side]

    @pl.when(pl.program_id(1) == 0)
    def _():
        y_ref[...] = jnp.zeros_like(y_ref)

    _side_cast(_grid_step_id(2), refs[4:4 + n_side], refs[5 + n_side:5 + 2 * n_side], side_blocks)
    u = u_ref[...]
    gt = _dot(u, wg_ref[...])
    up = _dot(u, wu_ref[...])
    a = (gt * _sigmoid(gt) * up).astype(BF16)
    y_ref[...] += _dot(a, wd_ref[...])


def _ffn_steps(m, ff, tm=512, tf=FFN_TF):
    return (m // tm) * (ff // tf)


def _ffn(u, w_gu, w_down, side=(), tm=512, tf=FFN_TF):
    m, d = u.shape
    ff = w_down.shape[0]
    nf = ff // tf
    row = pl.BlockSpec((tm, d), lambda i, j: (i, 0))
    side_specs, side_blocks = _side_plan(side, (m // tm) * nf, lambda i, j: i * nf + j)
    outs = pl.pallas_call(
        functools.partial(_ffn_kernel, side_blocks=side_blocks),
        grid=(m // tm, nf),
        in_specs=[row,
                  pl.BlockSpec((d, tf), lambda i, j: (0, j)),
                  pl.BlockSpec((d, tf), lambda i, j: (0, nf + j)),
                  pl.BlockSpec((tf, d), lambda i, j: (j, 0))] + side_specs,
        out_specs=[row] + side_specs,
        out_shape=[jax.ShapeDtypeStruct((m, d), F32)]
        + [jax.ShapeDtypeStruct(w.shape, BF16) for w in side],
        compiler_params=_cparams(("arbitrary", "arbitrary")),
        name="dense_ffn",
    )(u, w_gu, w_gu, w_down, *side)
    return outs[0], list(outs[1:])


def _mla_proj_kernel(h_ref, y_ref, gate_ref, g2_ref, gq_ref, scq_ref, shq_ref, gk_ref, sck_ref, shk_ref,
                     wqa_ref, qg_ref, wqm_ref, wqr_ref,
                     wkva_ref, wkr_ref, wkrr_ref, kg_ref, wk_ref, wv_ref,
                     cos_ref, sin_ref, hn_ref, q_ref, k_ref, v_ref):
    h_new = h_ref[...] + gate_ref[0] * (_rms(y_ref[...]) * g2_ref[...])
    hn_ref[...] = h_new
    xhat = _rms(h_new)
    uq = (xhat * gq_ref[...] * (1.0 + scq_ref[0]) + shq_ref[0]).astype(BF16)
    xk = (xhat * gk_ref[...] * (1.0 + sck_ref[0]) + shk_ref[0]).astype(BF16)
    cos = cos_ref[...]
    sin = sin_ref[...]
    qn = (_rms(_dot(uq, wqa_ref[...])) * qg_ref[...]).astype(BF16)
    cn = (_rms(_dot(xk, wkva_ref[...])) * kg_ref[...]).astype(BF16)
    krope = (_dot(xk, wkr_ref[...]) * cos + _dot(xk, wkrr_ref[...]) * sin).astype(BF16)
    ones = jnp.ones((cn.shape[0], V_HEAD), BF16)
    for hp in range(MLA_HEADS // 2):
        ps = slice(hp * 2 * LANES, (hp + 1) * 2 * LANES)
        qr2 = _dot(qn, wqr_ref[:, ps]) * Q_SCALE
        kn2 = _dot(cn, wk_ref[:, ps]).astype(BF16)
        v2 = _dot(cn, wv_ref[:, ps]).astype(BF16)
        for sub in range(2):
            hd = 2 * hp + sub
            ls = slice(sub * LANES, (sub + 1) * LANES)
            qm = _dot(qn, wqm_ref[:, hd * QK_PAD:(hd + 1) * QK_PAD]) * Q_SCALE
            q_ref[:, hd * QK_PAD:hd * QK_PAD + QK_NOPE] = qm[:, :QK_NOPE].astype(BF16)
            q_ref[:, hd * QK_PAD + QK_NOPE:(hd + 1) * QK_PAD] = (
                qm[:, QK_NOPE:] * cos + qr2[:, ls] * sin).astype(BF16)
            k_ref[:, hd * QK_PAD:hd * QK_PAD + QK_NOPE] = kn2[:, ls]
            k_ref[:, hd * QK_PAD + QK_NOPE:(hd + 1) * QK_PAD] = krope
            v_ref[:, 2 * hd * V_HEAD:(2 * hd + 1) * V_HEAD] = v2[:, ls]
            v_ref[:, (2 * hd + 1) * V_HEAD:(2 * hd + 2) * V_HEAD] = ones


def _mla_proj(h, y, gate, g2, gq, scq, shq, gk, sck, shk, wts, cos, sin, rows_per_batch, tm=256):
    m, d = h.shape
    bpb = rows_per_batch // tm
    row = lambda w: pl.BlockSpec((tm, w), lambda i: (i, 0))
    vec = lambda w: pl.BlockSpec((1, w), lambda i: (0, 0))
    mod = pl.BlockSpec((1, 1, d), lambda i: (i // bpb, 0, 0))
    full = lambda a: pl.BlockSpec(a.shape, lambda i: (0, 0), pipeline_mode=pl.Buffered(1))
    wqa, qg, wqm, wqr, wkva, wkr, wkrr, kg, wk, wv = wts
    return pl.pallas_call(
        _mla_proj_kernel,
        grid=(m // tm,),
        in_specs=[row(d), row(d), mod, vec(d), vec(d), mod, mod, vec(d), mod, mod,
                  full(wqa), full(qg), full(wqm), full(wqr),
                  full(wkva), full(wkr), full(wkrr), full(kg), full(wk), full(wv),
                  row(LANES), row(LANES)],
        out_specs=[row(d), row(MLA_HEADS * QK_PAD), row(MLA_HEADS * QK_PAD),
                   row(MLA_HEADS * 2 * V_HEAD)],
        out_shape=[jax.ShapeDtypeStruct((m, d), F32),
                   jax.ShapeDtypeStruct((m, MLA_HEADS * QK_PAD), BF16),
                   jax.ShapeDtypeStruct((m, MLA_HEADS * QK_PAD), BF16),
                   jax.ShapeDtypeStruct((m, MLA_HEADS * 2 * V_HEAD), BF16)],
        compiler_params=_cparams(("arbitrary",)),
        name="mla_proj",
    )(h, y, gate, g2, gq, scq, shq, gk, sck, shk, wqa, qg, wqm, wqr, wkva, wkr, wkrr, kg, wk, wv,
      cos, sin)


def _flash_kernel(*refs, tq, tk, slab, side_blocks):
    n_side = len(side_blocks)
    q_ref, k_ref, v_ref = refs[:3]
    o_ref = refs[3 + n_side]
    s_a, s_b, p_scr, m_scr, al_scr, acc_scr = refs[4 + 2 * n_side:]
    qi = pl.program_id(2)
    _side_cast(_grid_step_id(3), refs[3:3 + n_side], refs[4 + n_side:4 + 2 * n_side], side_blocks)

    s_bufs = (s_a, s_b)
    m_scr[...] = jnp.full_like(m_scr, -jnp.inf)
    acc_scr[...] = jnp.zeros_like(acc_scr)
    n_col = tk // LANES
    n_diag = tq // tk

    def kv_rows(j):
        return pl.ds(pl.multiple_of(j * tk, tk), tk)

    half = tk // 2

    def kv_half(j):
        return pl.ds(pl.multiple_of(j * tk, half), half)

    def scores(j, dst, row0, diag=False):
        if diag:
            dst[row0:row0 + half, :half] = _dot_nt(q_ref[row0:row0 + half, :], k_ref[kv_half(j), :])
            row0 += half
        dst[row0:, :] = _dot_nt(q_ref[row0:, :], k_ref[kv_rows(j), :])

    def rescale(rows, pv):
        al = al_scr[rows, :]
        acc_scr[rows, :V_HEAD] = acc_scr[rows, :V_HEAD] * al + pv[:, :V_HEAD]
        acc_scr[rows, V_HEAD:] = acc_scr[rows, V_HEAD:] * al + pv[:, V_HEAD:]

    def consume(j, src, row0, diag):
        for r0 in range(row0, tq, slab):
            rows = slice(r0, r0 + slab)
            rel = r0 - row0
            top = diag and rel < half
            n_use = n_col // 2 if top else n_col
            cols = [src[rows, cb * LANES:(cb + 1) * LANES] for cb in range(n_use)]
            if diag and rel < tk:
                rowp = rel + lax.broadcasted_iota(jnp.int32, (slab, LANES), 0)
                lane = lax.broadcasted_iota(jnp.int32, (slab, LANES), 1)
                first = 0 if top else n_col // 2
                cols = [cols[cb] if cb < first else
                        jnp.where(cb * LANES + lane <= rowp, cols[cb], -jnp.inf)
                        for cb in range(n_use)]
            mx = functools.reduce(jnp.maximum, cols)
            m_old = m_scr[rows, :]
            m_new = jnp.maximum(m_old, jnp.max(mx, axis=-1, keepdims=True))
            al_scr[rows, :] = jnp.exp2(m_old - m_new)
            m_scr[rows, :] = m_new
            for cb in range(n_use):
                p_scr[rows, cb * LANES:(cb + 1) * LANES] = jnp.exp2(cols[cb] - m_new).astype(BF16)
        if diag:
            top_rows = slice(row0, row0 + half)
            rescale(top_rows, _dot(p_scr[top_rows, :half], v_ref[kv_half(j), :]))
            row0 += half
        rescale(slice(row0, tq), _dot(p_scr[row0:, :], v_ref[kv_rows(j), :]))

    n_full = qi * n_diag
    scores(0, s_a, 0)

    def pair(pi, carry):
        for u in range(2):
            scores(2 * pi + u + 1, s_bufs[1 - u], 0)
            consume(2 * pi + u, s_bufs[u], 0, False)
        return carry

    lax.fori_loop(0, n_full // 2, pair, 0)
    for dg in range(n_diag):
        if dg + 1 < n_diag:
            scores(n_full + dg + 1, s_bufs[(dg + 1) % 2], (dg + 1) * tk, diag=True)
        consume(n_full + dg, s_bufs[dg % 2], dg * tk, True)

    o_ref[...] = (acc_scr[:, :V_HEAD] / acc_scr[:, V_HEAD:]).astype(o_ref.dtype)


def _flash_tiles(seq):
    tq = min(FLASH_TQ, seq)
    return tq, min(FLASH_TK, tq // 2)


def _flash(q, k, v, batch, side=(), slab=64):
    m = q.shape[0]
    s = m // batch
    tq, tk = _flash_tiles(s)
    nq = s // tq
    side_specs, side_blocks = _side_plan(
        side, batch * MLA_HEADS * nq, lambda b, h, i: (b * MLA_HEADS + h) * nq + i)
    outs = pl.pallas_call(
        functools.partial(_flash_kernel, tq=tq, tk=tk, slab=slab, side_blocks=side_blocks),
        grid=(batch, MLA_HEADS, nq),
        in_specs=[pl.BlockSpec((tq, QK_PAD), lambda b, h, i: (b * nq + i, h)),
                  pl.BlockSpec((s, QK_PAD), lambda b, h, i: (b, h)),
                  pl.BlockSpec((s, 2 * V_HEAD), lambda b, h, i: (b, h))] + side_specs,
        out_specs=[pl.BlockSpec((tq, V_HEAD), lambda b, h, i: (b * nq + i, h))] + side_specs,
        out_shape=[jax.ShapeDtypeStruct((m, MLA_HEADS * V_HEAD), BF16)]
        + [jax.ShapeDtypeStruct(w.shape, BF16) for w in side],
        scratch_shapes=[pltpu.VMEM((tq, tk), F32), pltpu.VMEM((tq, tk), F32),
                        pltpu.VMEM((tq, tk), BF16), pltpu.VMEM((tq, LANES), F32),
                        pltpu.VMEM((tq, LANES), F32), pltpu.VMEM((tq, 2 * V_HEAD), F32)],
        compiler_params=_cparams(("arbitrary", "arbitrary", "arbitrary")),
        name="mla_flash",
    )(q, k, v, *side)
    return outs[0], list(outs[1:])


def _split3(x):
    hi = x.astype(BF16)
    lo = (x - hi.astype(F32)).astype(BF16)
    return hi, lo


def _moe_pre_kernel(h_ref, g_ref, sc_ref, sh_ref, wr_ref, xs_ref, meta_ref, wts_ref, cnt_ref,
                    run_scr):
    tm, d = h_ref.shape

    @pl.when(pl.program_id(0) == 0)
    def _():
        run_scr[...] = jnp.zeros_like(run_scr)

    u = _rms(h_ref[...]) * g_ref[...] * (1.0 + sc_ref[0]) + sh_ref[0]
    xs_ref[...] = u

    u_hi, u_lo = _split3(u)
    w_hi, w_lo = _split3(wr_ref[...])
    logits = _dot(u_hi, w_hi) + (_dot(u_hi, w_lo) + _dot(u_lo, w_hi))
    lane = lax.broadcasted_iota(jnp.int32, logits.shape, 1)
    lg = jnp.where(lane < N_EXPERTS, logits, -jnp.inf)
    m1 = jnp.max(lg, axis=-1, keepdims=True)
    i1 = jnp.min(jnp.where(lg == m1, lane, LANES), axis=-1, keepdims=True)
    lg2 = jnp.where(lane == i1, -jnp.inf, lg)
    m2 = jnp.max(lg2, axis=-1, keepdims=True)
    i2 = jnp.min(jnp.where(lg2 == m2, lane, LANES), axis=-1, keepdims=True)
    e = jnp.exp(m2 - m1)
    w1 = 1.0 / (1.0 + e)
    w2 = e * w1

    oh1 = lane == i1
    oh2 = lane == i2
    cnt = oh1.astype(F32) + oh2.astype(F32)
    r = lax.broadcasted_iota(jnp.int32, (tm, tm), 0)
    c = lax.broadcasted_iota(jnp.int32, (tm, tm), 1)
    strict = (r > c).astype(BF16)
    prefix = _dot(strict, cnt.astype(BF16)) + run_scr[...]
    rank1 = jnp.sum(jnp.where(oh1, prefix, 0.0), axis=-1, keepdims=True).astype(jnp.int32)
    rank2 = jnp.sum(jnp.where(oh2, prefix, 0.0), axis=-1, keepdims=True).astype(jnp.int32)
    run_scr[...] += jnp.sum(cnt, axis=0, keepdims=True)

    meta_ref[...] = jnp.where(lane == 0, i1, jnp.where(lane == 1, i2,
                              jnp.where(lane == 2, rank1, jnp.where(lane == 3, rank2, 0))))
    wts_ref[...] = jnp.where(lane == 0, w1, jnp.where(lane == 1, w2, 0.0))
    cnt_ref[...] = run_scr[...]


def _moe_pre(h, g, scale, shift, w_router_pad, rows_per_batch, tm=512):
    m, d = h.shape
    bpb = rows_per_batch // tm
    row = lambda w: pl.BlockSpec((tm, w), lambda i: (i, 0))
    vec = lambda w: pl.BlockSpec((1, w), lambda i: (0, 0))
    mod = pl.BlockSpec((1, 1, d), lambda i: (i // bpb, 0, 0))
    return pl.pallas_call(
        _moe_pre_kernel,
        grid=(m // tm,),
        in_specs=[row(d), vec(d), mod, mod, pl.BlockSpec((d, LANES), lambda i: (0, 0))],
        out_specs=[row(d), row(LANES), row(LANES), vec(LANES)],
        out_shape=[jax.ShapeDtypeStruct((m, d), F32),
                   jax.ShapeDtypeStruct((m, LANES), jnp.int32),
                   jax.ShapeDtypeStruct((m, LANES), F32),
                   jax.ShapeDtypeStruct((1, LANES), F32)],
        scratch_shapes=[pltpu.VMEM((1, LANES), F32)],
        compiler_params=_cparams(("arbitrary",)),
        name="moe_route",
    )(h, g, scale, shift, w_router_pad)


ROW_COPY_UNROLL = 8


def _issue_row_copies(n_rows, copies_of_row):
    def trip(t, carry):
        n = 0
        for u in range(ROW_COPY_UNROLL):
            for cp in copies_of_row(t * ROW_COPY_UNROLL + u):
                cp.start(priority=n % 2)
                n += 1
        return carry
    lax.fori_loop(0, n_rows // ROW_COPY_UNROLL, trip, 0)


def _moe_dispatch_kernel(fill_ref, dest_ref, x_ref, xs_ref, zbuf, sem, zsem, *, tmb, n_tail):
    tm = x_ref.shape[0]

    @pl.when(pl.program_id(0) == 0)
    def _():
        zbuf[...] = jnp.zeros_like(zbuf)

        def pad_row(e, r):
            return pltpu.make_async_copy(zbuf.at[pl.ds(0, 1), :],
                                         xs_ref.at[pl.ds(fill_ref[2 * e] + r, 1), :], zsem)

        def tail_block(b):
            start = pl.multiple_of(fill_ref[2 * N_EXPERTS] + b * tmb, tmb)
            return pltpu.make_async_copy(zbuf, xs_ref.at[pl.ds(start, tmb), :], zsem)

        for e in range(N_EXPERTS):
            def start_row(r, carry, e=e):
                pad_row(e, r).start()
                return carry
            lax.fori_loop(0, fill_ref[2 * e + 1], start_row, 0)
        n_blk = fill_ref[2 * N_EXPERTS + 1] // tmb
        for b in range(n_tail):
            @pl.when(b < n_blk)
            def _(b=b):
                tail_block(b).start()
        for e in range(N_EXPERTS):
            def wait_row(r, carry, e=e):
                pad_row(e, r).wait()
                return carry
            lax.fori_loop(0, fill_ref[2 * e + 1], wait_row, 0)
        for b in range(n_tail):
            @pl.when(b < n_blk)
            def _(b=b):
                tail_block(b).wait()

    def copies_of_row(r):
        return [pltpu.make_async_copy(x_ref.at[pl.ds(r, 1), :],
                                      xs_ref.at[pl.ds(dest_ref[0, 0, 2 * r + k], 1), :], sem)
                for k in range(2)]

    _issue_row_copies(tm, copies_of_row)
    for _ in range(2):
        pltpu.make_async_copy(x_ref, x_ref, sem).wait()


def _moe_dispatch(x, dest, fill, n_rows, tmb, tm=512):
    m, w = x.shape
    nb = m // tm
    n_tail = (n_rows - 2 * m) // tmb
    return pl.pallas_call(
        functools.partial(_moe_dispatch_kernel, tmb=tmb, n_tail=n_tail),
        grid_spec=pltpu.PrefetchScalarGridSpec(
            num_scalar_prefetch=1,
            grid=(nb,),
            in_specs=[pl.BlockSpec((1, 1, 2 * tm), lambda i, fl: (i, 0, 0), memory_space=pltpu.SMEM),
                      pl.BlockSpec((tm, w), lambda i, fl: (i, 0))],
            out_specs=pl.BlockSpec(memory_space=pl.ANY),
            scratch_shapes=[pltpu.VMEM((tmb, w), F32), pltpu.SemaphoreType.DMA(()),
                            pltpu.SemaphoreType.DMA(())],
        ),
        out_shape=jax.ShapeDtypeStruct((n_rows, w), F32),
        compiler_params=_cparams(("arbitrary",)),
        name="moe_dispatch",
    )(fill, dest.reshape(nb, 1, 2 * tm), x)


def _moe_ffn_kernel(blk_e_ref, n_used_ref, xs_ref, wg_ref, wu_ref, wd_ref, y_ref, u_scr):
    i = pl.program_id(0)
    j = pl.program_id(1)

    @pl.when(i < n_used_ref[0])
    def _():
        @pl.when(j == 0)
        def _():
            u_scr[...] = xs_ref[...].astype(BF16)
            y_ref[...] = jnp.zeros_like(y_ref)

        u = u_scr[...]
        gt = _dot(u, wg_ref[0])
        up = _dot(u, wu_ref[0])
        a = (gt * _sigmoid(gt) * up).astype(BF16)
        y_ref[...] += _dot(a, wd_ref[0])

    @pl.when((i >= n_used_ref[0]) & (j == 0))
    def _():
        y_ref[...] = jnp.zeros_like(y_ref)


def _moe_ffn(xs, blk_e, n_used, w_gu, w_down, tmb, tf=FFN_TF):
    n_rows, d = xs.shape
    ff = w_down.shape[1]
    nf = ff // tf
    nb = n_rows // tmb

    def blk(i, n_used_ref):
        return jnp.minimum(i, n_used_ref[0] - 1)

    def fidx(i, j, n_used_ref):
        return jnp.where(i < n_used_ref[0], j, nf - 1)

    grid_spec = pltpu.PrefetchScalarGridSpec(
        num_scalar_prefetch=2,
        grid=(nb, nf),
        in_specs=[
            pl.BlockSpec((tmb, d), lambda i, j, be, nu: (blk(i, nu), 0)),
            pl.BlockSpec((1, d, tf), lambda i, j, be, nu: (be[blk(i, nu)], 0, fidx(i, j, nu))),
            pl.BlockSpec((1, d, tf), lambda i, j, be, nu: (be[blk(i, nu)], 0, nf + fidx(i, j, nu))),
            pl.BlockSpec((1, tf, d), lambda i, j, be, nu: (be[blk(i, nu)], fidx(i, j, nu), 0)),
        ],
        out_specs=pl.BlockSpec((tmb, d), lambda i, j, be, nu: (i, 0)),
        scratch_shapes=[pltpu.VMEM((tmb, d), BF16)],
    )
    return pl.pallas_call(
        _moe_ffn_kernel,
        grid_spec=grid_spec,
        out_shape=jax.ShapeDtypeStruct((n_rows, d), F32),
        compiler_params=_cparams(("arbitrary", "arbitrary")),
        name="moe_grouped_ffn",
    )(blk_e, n_used, xs, w_gu, w_gu, w_down)


def _moe_combine_kernel(dest_ref, ys_ref, wts_ref, h_ref, gate_ref, g2_ref, out_ref,
                        buf0, buf1, sem):
    tm = h_ref.shape[0]
    bufs = (buf0, buf1)

    def copies_of_row(r):
        return [pltpu.make_async_copy(ys_ref.at[pl.ds(dest_ref[0, 0, 2 * r + k], 1), :],
                                      bufs[k].at[pl.ds(r, 1), :], sem)
                for k in range(2)]

    _issue_row_copies(tm, copies_of_row)
    for k in range(2):
        pltpu.make_async_copy(bufs[k], bufs[k], sem).wait()

    wts = wts_ref[...]
    y = wts[:, 0:1] * buf0[...] + wts[:, 1:2] * buf1[...]
    out_ref[...] = h_ref[...] + gate_ref[0] * (_rms(y) * g2_ref[...])


def _moe_combine(ys, dest, wts, h, gate, g2, rows_per_batch, tm=512):
    m, d = h.shape
    nb = m // tm
    bpb = rows_per_batch // tm
    row = lambda w: pl.BlockSpec((tm, w), lambda i: (i, 0))
    return pl.pallas_call(
        _moe_combine_kernel,
        grid=(nb,),
        in_specs=[pl.BlockSpec((1, 1, 2 * tm), lambda i: (i, 0, 0), memory_space=pltpu.SMEM),
                  pl.BlockSpec(memory_space=pl.ANY),
                  row(LANES), row(d),
                  pl.BlockSpec((1, 1, d), lambda i: (i // bpb, 0, 0)),
                  pl.BlockSpec((1, d), lambda i: (0, 0))],
        out_specs=row(d),
        out_shape=jax.ShapeDtypeStruct((m, d), F32),
        scratch_shapes=[pltpu.VMEM((tm, d), F32), pltpu.VMEM((tm, d), F32),
                        pltpu.SemaphoreType.DMA(())],
        compiler_params=_cparams(("arbitrary",)),
        name="moe_combine",
    )(dest.reshape(nb, 1, 2 * tm), ys, wts, h, gate, g2)


def _moe(h, g, scale, shift, w_router, w_gu, w_down, gate, g2, rows_per_batch, tmb=MOE_BLOCK_ROWS):
    m, d = h.shape
    n_rows = (2 * m + N_EXPERTS * (tmb - 1)) // tmb * tmb
    nb = n_rows // tmb
    w_router_pad = jnp.pad(w_router, ((0, 0), (0, LANES - N_EXPERTS)))
    u, meta, wts, cnt = _moe_pre(h, g, scale, shift, w_router_pad, rows_per_batch)

    counts = cnt[0, :N_EXPERTS].astype(jnp.int32)
    padded = (counts + tmb - 1) // tmb * tmb
    pend = jnp.cumsum(padded)
    pstart = pend - padded
    dest = (pstart[meta[:, 0:2]] + meta[:, 2:4]).astype(jnp.int32).reshape(-1)
    blk_start = jnp.arange(nb, dtype=jnp.int32) * tmb
    blk_e = jnp.minimum(jnp.sum(pend[None, :] <= blk_start[:, None], axis=1),
                        N_EXPERTS - 1).astype(jnp.int32)
    n_used = (pend[-1:] // tmb).astype(jnp.int32)

    fill = jnp.stack([jnp.concatenate([pstart + counts, pend[-1:]]),
                      jnp.concatenate([padded - counts, n_rows - pend[-1:]])],
                     axis=1).reshape(-1).astype(jnp.int32)
    xs = _moe_dispatch(u, dest, fill, n_rows, tmb)
    ys = _moe_ffn(xs, blk_e, n_used, w_gu, w_down, tmb)
    return _moe_combine(ys, dest, wts, h, gate, g2, rows_per_batch)


def _swap_halves(w):
    half = w.shape[-1] // 2
    return jnp.concatenate([w[..., half:], w[..., :half]], axis=-1)


def _mla_weights(w_q_a, q_norm_g, w_q_b, w_kv_a, kv_norm_g, w_kv_b):
    ql = w_q_b.shape[0]
    kvl = w_kv_b.shape[0]
    wq = w_q_b.reshape(ql, MLA_HEADS, QK_NOPE + QK_ROPE)
    rope = wq[:, :, QK_NOPE:]
    pad_r = LANES - QK_ROPE
    wqm = jnp.concatenate([wq, jnp.zeros((ql, MLA_HEADS, QK_PAD - QK_NOPE - QK_ROPE), F32)], axis=-1)
    wqr = jnp.concatenate([_swap_halves(rope), jnp.zeros((ql, MLA_HEADS, pad_r), F32)], axis=-1)
    wkv = w_kv_b.reshape(kvl, MLA_HEADS, QK_NOPE + V_HEAD)
    kr = w_kv_a[:, kvl:]
    dm = w_kv_a.shape[0]
    wkr = jnp.concatenate([kr, jnp.zeros((dm, pad_r), F32)], axis=-1)
    wkrr = jnp.concatenate([_swap_halves(kr), jnp.zeros((dm, pad_r), F32)], axis=-1)
    bf = lambda a: a.astype(BF16)
    return (bf(w_q_a), q_norm_g.reshape(1, -1), bf(wqm.reshape(ql, -1)), bf(wqr.reshape(ql, -1)),
            bf(w_kv_a[:, :kvl]), bf(wkr), bf(wkrr), kv_norm_g.reshape(1, -1),
            bf(wkv[:, :, :QK_NOPE].reshape(kvl, -1)), bf(wkv[:, :, QK_NOPE:].reshape(kvl, -1)))


def kernel(x, c, positions, ada_w, ada_b, norm_g, hg_w_in, hg_lb_logits, hg_out_norm_g, hg_w_out, kv_src_norm_g, kv_src_ada_w, kv_src_ada_b, mla_w_kv_a, mla_kv_norm_g, mla_w_kv_b, mla_w_q_a, mla_q_norm_g, mla_w_q_b, mla_w_o, ffn_w_gu, ffn_w_down, moe_w_router, moe_w_gu, moe_w_down):
    batch, seq, d = x.shape
    m = batch * seq
    bf = lambda a: a.astype(BF16)

    c8 = jnp.pad(c, ((0, 8 - batch), (0, 0)))
    ada = _modulation(c8, ada_w.reshape(4, d, 3 * d), ada_b.reshape(4, 3 * d))[:, :batch]
    kvm = _modulation(c8, kv_src_ada_w[None], kv_src_ada_b[None])[0, :batch]

    def mods(idx):
        a = ada[idx]
        return [a[:, i * d:(i + 1) * d].reshape(batch, 1, d) for i in range(3)]

    vec = lambda a: a.reshape(1, -1)
    lb_all = jnp.cumsum(jax.nn.softmax(hg_lb_logits.astype(F32), axis=0), axis=0)

    h = x.reshape(m, d)

    shift, scale, gate = mods(0)
    q, f_logit, iv, gg = _hg_proj(h, vec(norm_g[0, 0, 0]), scale, shift, bf(hg_w_in[0]), seq)
    o, (ffn_gu_bf, ffn_down_bf, hg_out_bf, moe_down_bf) = _with_side(
        lambda side: _hgrn2(q, f_logit, iv, vec(lb_all[0]), batch, side=side),
        [ffn_w_gu[0], ffn_w_down[0], hg_w_out[0], moe_w_down[0]], _hgrn2_steps(m))
    shift_f, scale_f, gate_f = mods(1)
    h, u_ffn = _oproj(o, hg_out_bf, h, gate, vec(norm_g[0, 0, 1]), seq,
                      gg=gg, gout=vec(hg_out_norm_g[0]),
                      nxt=(vec(norm_g[0, 1, 0]), scale_f, shift_f))

    y_ffn, (moe_gu_bf, mla_o_bf) = _with_side(
        lambda side: _ffn(u_ffn, ffn_gu_bf, ffn_down_bf, side=side),
        [moe_w_gu[0], mla_w_o[0]], _ffn_steps(m, ffn_w_down.shape[1]))

    shift_k, scale_k = [kvm[:, i * d:(i + 1) * d].reshape(batch, 1, d) for i in range(2)]
    shift, scale, gate = mods(2)
    cos, sin = _rope_tables(positions.reshape(m, 1).astype(F32))
    wts = _mla_weights(mla_w_q_a[0], mla_q_norm_g[0], mla_w_q_b[0],
                       mla_w_kv_a, mla_kv_norm_g, mla_w_kv_b)
    h, qh, kh, vh = _mla_proj(h, y_ffn, gate_f, vec(norm_g[0, 1, 1]),
                              vec(norm_g[1, 0, 0]), scale, shift,
                              vec(kv_src_norm_g), scale_k, shift_k, wts, cos, sin, seq)
    att, _ = _flash(qh, kh, vh, batch)
    h = _oproj(att, mla_o_bf, h, gate, vec(norm_g[1, 0, 1]), seq)

    shift, scale, gate = mods(3)
    h = _moe(h, vec(norm_g[1, 1, 0]), scale, shift, moe_w_router[0],
             moe_gu_bf, moe_down_bf, gate, vec(norm_g[1, 1, 1]), seq)
    return h.reshape(batch, seq, d)
```
